```python
import jax, jax.numpy as jnp
from jax import lax
import numpy as np

D_MODEL = 1024
BATCH = 8
SEQ = 4096
DEPTH = 1
DEC_BATCH = 128
DEC_SEQ = 4
PAST_LEN = 8192
PAGE_SIZE = 128

NSA_HEADS = 8
NSA_KV = 2
HEAD_DIM = 64
NSA_GROUP = NSA_HEADS // NSA_KV
NSA_W = NSA_HEADS * HEAD_DIM
CMP_STRIDE = 16
CMP_LEN = 2 * CMP_STRIDE
CMP_HID = 128
SEL_BLK = 64
SEL_TOPK = 16
WINDOW = 512
Q_BLOCK = 64
ROPE_THETA = 500000.0
ROPE_DIM = HEAD_DIM // 4
HG_HEADS = 4
HG_DK = 128
HG_DV = 128
HG_W = HG_HEADS * HG_DV
HG_CHUNK = 32
MIX_W = NSA_W + HG_W
MEM_LEN = 256
CA_HEADS = 4
CA_HD = D_MODEL // CA_HEADS
FFN_HID = -(-8 * D_MODEL // (3 * 256)) * 256
IN_SIZES = [NSA_W, 3 * NSA_HEADS] + [NSA_KV * HEAD_DIM] * 6 + [HG_W] * 4
N_IN = sum(IN_SIZES)
EPS = 1e-6
FORCE_SCORE = 1e4
NEG = -1e30

kernel_name = "hymba_nsa_hgrn2_decoder_step"


def rmsnorm(x, g):
    xf = x.astype(jnp.float32)
    y = xf * lax.rsqrt(jnp.mean(xf * xf, axis=-1, keepdims=True) + EPS)
    return (y * g.astype(jnp.float32)).astype(x.dtype)


def rope(x, pos):
    half = ROPE_DIM // 2
    inv = ROPE_THETA ** (-jnp.arange(half, dtype=jnp.float32) / half)
    ang = pos.astype(jnp.float32)[:, None] * inv[None, :]
    cos = jnp.cos(ang)[:, None, :]
    sin = jnp.sin(ang)[:, None, :]
    xf = x.astype(jnp.float32)
    x1 = xf[..., :half]
    x2 = xf[..., half:ROPE_DIM]
    out = jnp.concatenate([x1 * cos - x2 * sin, x1 * sin + x2 * cos, xf[..., ROPE_DIM:]], axis=-1)
    return out.astype(x.dtype)


def masked_softmax(s, mask, axis=-1):
    s = jnp.where(mask, s.astype(jnp.float32), NEG)
    m = jnp.max(s, axis=axis, keepdims=True)
    e = jnp.where(mask, jnp.exp(s - m), 0.0)
    return e / jnp.maximum(jnp.sum(e, axis=axis, keepdims=True), 1e-20)


def compress(k, pe, w1, w2):
    B, T = k.shape[:2]
    nseg = T // CMP_STRIDE
    seg = k[:, :nseg * CMP_STRIDE].reshape(B, nseg, CMP_STRIDE, NSA_KV, HEAD_DIM)
    seg = seg.transpose(0, 1, 3, 2, 4).reshape(B, nseg, NSA_KV, CMP_STRIDE * HEAD_DIM)
    pe = pe.reshape(2, CMP_STRIDE * HEAD_DIM)
    w1 = w1.reshape(2, CMP_STRIDE * HEAD_DIM, CMP_HID)
    ha = (seg + pe[0]) @ w1[0]
    hb = (seg + pe[1]) @ w1[1]
    h = jax.nn.gelu(ha[:, :-1] + hb[:, 1:])
    kc = h @ w2
    cend = jnp.arange(nseg - 1) * CMP_STRIDE + CMP_LEN - 1
    return kc, cend


def nsa_attend(q, gates, qpos, kc, vc, cend, fetch, nsel, kw, vw, kwpos):
    B, Tq = q.shape[:2]
    scale = HEAD_DIM ** -0.5
    qg = q.reshape(B, Tq, NSA_KV, NSA_GROUP, HEAD_DIM)
    qr = rope(q, qpos).reshape(B, Tq, NSA_KV, NSA_GROUP, HEAD_DIM)
    s = jnp.einsum('btgrd,bngd->bgrtn', qg, kc).astype(jnp.float32) * scale
    p_c = masked_softmax(s, cend[None, :] <= qpos[:, None])
    o_c = jnp.einsum('bgrtn,bngd->btgrd', p_c.astype(vc.dtype), vc)
    nc = kc.shape[1]
    cps = SEL_BLK // CMP_STRIDE
    imp = jnp.sum(p_c, axis=2)
    imp = jnp.pad(imp, ((0, 0), (0, 0), (0, 0), (0, nsel * cps - nc)))
    imp = imp.reshape(B, NSA_KV, Tq, nsel, cps).sum(-1)
    cur = qpos // SEL_BLK
    j = jnp.arange(nsel)
    avail = j[None, :] <= cur[:, None]
    forced = (j[None, :] == 0) | (j[None, :] == cur[:, None]) | (j[None, :] == cur[:, None] - 1)
    score = jnp.where(avail, jnp.where(forced, FORCE_SCORE, imp), -FORCE_SCORE)
    _, idx = lax.top_k(score, min(SEL_TOPK, nsel))
    ks, vs = fetch(idx)
    kpos = idx[..., None] * SEL_BLK + jnp.arange(SEL_BLK)
    smask = (idx <= cur[None, None, :, None])[..., None] & (kpos <= qpos[None, None, :, None, None])
    s = jnp.einsum('btgrd,bgtksd->bgrtks', qr, ks).astype(jnp.float32) * scale
    p_s = masked_softmax(s, smask[:, :, None], axis=(-2, -1))
    o_s = jnp.einsum('bgrtks,bgtksd->btgrd', p_s.astype(vs.dtype), vs)
    wmask = ((kwpos[None, :] <= qpos[:, None]) & (kwpos[None, :] > qpos[:, None] - WINDOW)
             & (kwpos[None, :] >= 0))
    s = jnp.einsum('btgrd,bkgd->bgrtk', qr, kw).astype(jnp.float32) * scale
    p_w = masked_softmax(s, wmask)
    o_w = jnp.einsum('bgrtk,bkgd->btgrd', p_w.astype(vw.dtype), vw)
    g = jax.nn.sigmoid(gates.astype(jnp.float32)).reshape(B, Tq, NSA_KV, NSA_GROUP, 3, 1)
    o = g[..., 0, :] * o_c + g[..., 1, :] * o_s + g[..., 2, :] * o_w
    return o.reshape(B, Tq, NSA_W).astype(q.dtype)


def nsa_prompt(q, gates, kcn, vcn, ksn, vsn, kwn, vwn, pe_k, w1_k, w2_k, pe_v, w1_v, w2_v):
    B, T = q.shape[:2]
    kc, cend = compress(kcn, pe_k, w1_k, w2_k)
    vc, _ = compress(vcn, pe_v, w1_v, w2_v)
    nsel = -(-T // SEL_BLK)
    padn = nsel * SEL_BLK - T
    ksr = jnp.pad(ksn, ((0, 0), (0, padn), (0, 0), (0, 0))).reshape(B, nsel, SEL_BLK, NSA_KV, HEAD_DIM)
    vsr = jnp.pad(vsn, ((0, 0), (0, padn), (0, 0), (0, 0))).reshape(B, nsel, SEL_BLK, NSA_KV, HEAD_DIM)
    bi = jnp.arange(B)[:, None, None, None]
    gi = jnp.arange(NSA_KV)[None, :, None, None]

    def fetch(idx):
        return ksr[bi, idx, :, gi], vsr[bi, idx, :, gi]

    kw_pad = jnp.pad(kwn, ((0, 0), (WINDOW, 0), (0, 0), (0, 0)))
    vw_pad = jnp.pad(vwn, ((0, 0), (WINDOW, 0), (0, 0), (0, 0)))

    def block(i):
        s0 = i * Q_BLOCK
        qpos = s0 + jnp.arange(Q_BLOCK)
        qb = lax.dynamic_slice_in_dim(q, s0, Q_BLOCK, 1)
        gb = lax.dynamic_slice_in_dim(gates, s0, Q_BLOCK, 1)
        kwb = lax.dynamic_slice_in_dim(kw_pad, s0, WINDOW + Q_BLOCK, 1)
        vwb = lax.dynamic_slice_in_dim(vw_pad, s0, WINDOW + Q_BLOCK, 1)
        kwpos = s0 - WINDOW + jnp.arange(WINDOW + Q_BLOCK)
        return nsa_attend(qb, gb, qpos, kc, vc, cend, fetch, nsel, kwb, vwb, kwpos)

    o = lax.map(block, jnp.arange(T // Q_BLOCK))
    return o.transpose(1, 0, 2, 3).reshape(B, T, NSA_W)


def nsa_sample(q, gates, kcn, vcn, ksn, vsn, kwn, vwn, pos, pool_ck, pool_cv, pool_sk, pool_sv,
               win_k, win_v, page_table, pe_k, w1_k, w2_k, pe_v, w1_v, w2_v):
    DB, TS = q.shape[:2]
    n_pages = page_table.shape[1]
    past = n_pages * PAGE_SIZE
    kc_all = jnp.concatenate([pool_ck[page_table].reshape(DB, past, NSA_KV, HEAD_DIM), kcn], axis=1)
    vc_all = jnp.concatenate([pool_cv[page_table].reshape(DB, past, NSA_KV, HEAD_DIM), vcn], axis=1)
    kc, cend = compress(kc_all, pe_k, w1_k, w2_k)
    vc, _ = compress(vc_all, pe_v, w1_v, w2_v)
    nsel = -(-(past + TS) // SEL_BLK)
    bpp = PAGE_SIZE // SEL_BLK
    npb = past // SEL_BLK
    n_new = -(-TS // SEL_BLK)
    padn = n_new * SEL_BLK - TS
    new_k = jnp.pad(ksn, ((0, 0), (0, padn), (0, 0), (0, 0))).reshape(DB, n_new, SEL_BLK, NSA_KV, HEAD_DIM)
    new_v = jnp.pad(vsn, ((0, 0), (0, padn), (0, 0), (0, 0))).reshape(DB, n_new, SEL_BLK, NSA_KV, HEAD_DIM)
    pool_kb = pool_sk.reshape(-1, SEL_BLK, NSA_KV, HEAD_DIM)
    pool_vb = pool_sv.reshape(-1, SEL_BLK, NSA_KV, HEAD_DIM)
    bi = jnp.arange(DB)[:, None, None, None]
    gi = jnp.arange(NSA_KV)[None, :, None, None]

    def fetch(idx):
        jp = jnp.minimum(idx, npb - 1)
        phys = page_table[bi, jp // bpp] * bpp + jp % bpp
        jn = jnp.clip(idx - npb, 0, n_new - 1)
        is_past = (idx < npb)[..., None, None]
        k = jnp.where(is_past, pool_kb[phys, :, gi], new_k[bi, jn, :, gi])
        v = jnp.where(is_past, pool_vb[phys, :, gi], new_v[bi, jn, :, gi])
        return k, v

    wb = win_k.shape[1]
    kw = jnp.concatenate([win_k, kwn], axis=1)
    vw = jnp.concatenate([win_v, vwn], axis=1)
    kwpos = past - wb + jnp.arange(wb + TS)
    o = nsa_attend(q, gates, pos, kc, vc, cend, fetch, nsel, kw, vw, kwpos)
    return o, kw[:, -wb:], vw[:, -wb:]


def hgrn2_scan(q, log_f, k, v, S0):
    B, T = q.shape[:2]
    Tp = -(-T // HG_CHUNK) * HG_CHUNK
    padn = Tp - T

    def to_chunks(a):
        a = jnp.pad(a, ((0, 0), (0, padn), (0, 0), (0, 0)))
        return a.reshape(B, Tp // HG_CHUNK, HG_CHUNK, HG_HEADS, -1).transpose(1, 0, 3, 2, 4)

    mask = jnp.tril(jnp.ones((HG_CHUNK, HG_CHUNK), dtype=bool))[:, :, None]

    def step(S, inp):
        qc, lfc, kc, vc = inp
        b = jnp.cumsum(lfc, axis=2)
        o_inter = jnp.einsum('bhtd,bhde->bhte', qc * jnp.exp(b), S)
        diff = b[:, :, :, None, :] - b[:, :, None, :, :]
        decay = jnp.exp(jnp.where(mask, diff, -jnp.inf))
        A = jnp.einsum('bhtd,bhsd,bhtsd->bhts', qc, kc, decay)
        o = o_inter + jnp.einsum('bhts,bhse->bhte', A, vc)
        bl = b[:, :, -1:, :]
        S = jnp.exp(bl[:, :, 0, :])[..., None] * S + jnp.einsum('bhsd,bhse->bhde', kc * jnp.exp(bl - b), vc)
        return S, o

    S, o = lax.scan(step, S0, (to_chunks(q), to_chunks(log_f), to_chunks(k), to_chunks(v)))
    o = o.transpose(1, 0, 3, 2, 4).reshape(B, Tp, HG_HEADS, HG_DV)[:, :T]
    return o, S


def hgrn2_mix(hq, hf, hi, hg, lb, out_g, S0):
    B, T = hq.shape[:2]
    q = hq.reshape(B, T, HG_HEADS, HG_DK).astype(jnp.float32)
    z = hf.reshape(B, T, HG_HEADS, HG_DK).astype(jnp.float32)
    log_f = jnp.log(lb + (1.0 - lb) * jax.nn.sigmoid(z))
    k = (1.0 - lb) * jax.nn.sigmoid(-z)
    v = hi.reshape(B, T, HG_HEADS, HG_DV).astype(jnp.float32)
    o, S = hgrn2_scan(q, log_f, k, v, S0)
    o = rmsnorm(o, out_g.reshape(HG_HEADS, HG_DV))
    o = o * jax.nn.silu(hg.astype(jnp.float32).reshape(B, T, HG_HEADS, HG_DV))
    return o.reshape(B, T, HG_W).astype(hq.dtype), S


def cross_attn(h, mk, mv, wq, wo):
    B, T = h.shape[:2]
    q = (h @ wq).reshape(B, T, CA_HEADS, CA_HD)
    s = jnp.einsum('bthd,bmhd->bhtm', q, mk).astype(jnp.float32) * CA_HD ** -0.5
    p = jax.nn.softmax(s, axis=-1)
    o = jnp.einsum('bhtm,bmhd->bthd', p.astype(mv.dtype), mv).reshape(B, T, CA_HEADS * CA_HD)
    return o @ wo


def swiglu(h, wg, wu, wd):
    return (jax.nn.silu(h @ wg) * (h @ wu)) @ wd


def split_proj(h, w_in):
    offs = [int(o) for o in np.cumsum(IN_SIZES)[:-1]]
    return jnp.split(h @ w_in, offs, axis=-1)


def setup_inputs(seed: int = 0) -> dict:
    key = jax.random.key(seed)
    ks = iter(jax.random.split(key, 48))
    nrm = lambda shape, s=1.0: jax.random.normal(next(ks), shape, jnp.float32) * s
    w = lambda shape, fan_in: nrm(shape, fan_in ** -0.5)
    gain = lambda shape: 1.0 + nrm(shape, 0.05)
    n_pages = PAST_LEN // PAGE_SIZE
    n_pool = (DEC_BATCH * n_pages * 5) // 4
    wb = min(WINDOW, PAST_LEN)
    pool = (DEPTH, n_pool, PAGE_SIZE, NSA_KV, HEAD_DIM)
    page_table = jax.random.permutation(next(ks), n_pool)[:DEC_BATCH * n_pages]
    page_table = page_table.reshape(DEC_BATCH, n_pages).astype(jnp.int32)
    return {
        "x_prompt": nrm((BATCH, SEQ, D_MODEL)),
        "x_sample": nrm((DEC_BATCH, DEC_SEQ, D_MODEL)),
        "cache_cmp_k": nrm(pool),
        "cache_cmp_v": nrm(pool),
        "cache_sel_k": nrm(pool),
        "cache_sel_v": nrm(pool),
        "cache_win_k": nrm((DEPTH, DEC_BATCH, wb, NSA_KV, HEAD_DIM)),
        "cache_win_v": nrm((DEPTH, DEC_BATCH, wb, NSA_KV, HEAD_DIM)),
        "state_hgrn": nrm((DEPTH, DEC_BATCH, HG_HEADS, HG_DK, HG_DV), 0.5),
        "cache_mem_k": nrm((DEPTH, DEC_BATCH, MEM_LEN, CA_HEADS, CA_HD)),
        "cache_mem_v": nrm((DEPTH, DEC_BATCH, MEM_LEN, CA_HEADS, CA_HD)),
        "page_table": page_table,
        "mem_prompt": nrm((BATCH, MEM_LEN, D_MODEL)),
        "norm_mix": gain((DEPTH, D_MODEL)),
        "w_in": w((DEPTH, D_MODEL, N_IN), D_MODEL),
        "cmp_pe_k": nrm((DEPTH, CMP_LEN, HEAD_DIM), 0.1),
        "cmp_w1_k": w((DEPTH, CMP_LEN * HEAD_DIM, CMP_HID), CMP_LEN * HEAD_DIM),
        "cmp_w2_k": w((DEPTH, CMP_HID, HEAD_DIM), CMP_HID),
        "cmp_pe_v": nrm((DEPTH, CMP_LEN, HEAD_DIM), 0.1),
        "cmp_w1_v": w((DEPTH, CMP_LEN * HEAD_DIM, CMP_HID), CMP_LEN * HEAD_DIM),
        "cmp_w2_v": w((DEPTH, CMP_HID, HEAD_DIM), CMP_HID),
        "nsa_out_norm": gain((DEPTH, NSA_W)),
        "hg_lb_logits": nrm((DEPTH + 1, HG_HEADS * HG_DK), 0.5),
        "hg_out_norm": gain((DEPTH, HG_W)),
        "w_out": w((DEPTH, MIX_W, D_MODEL), MIX_W),
        "norm_ca": gain((DEPTH, D_MODEL)),
        "norm_mem": gain((DEPTH, D_MODEL)),
        "ca_wq": w((DEPTH, D_MODEL, CA_HEADS * CA_HD), D_MODEL),
        "ca_wk": w((DEPTH, D_MODEL, CA_HEADS * CA_HD), D_MODEL),
        "ca_wv": w((DEPTH, D_MODEL, CA_HEADS * CA_HD), D_MODEL),
        "ca_wo": w((DEPTH, CA_HEADS * CA_HD, D_MODEL), CA_HEADS * CA_HD),
        "norm_ffn": gain((DEPTH, D_MODEL)),
        "ffn_w_gate": w((DEPTH, D_MODEL, FFN_HID), D_MODEL),
        "ffn_w_up": w((DEPTH, D_MODEL, FFN_HID), D_MODEL),
        "ffn_w_down": w((DEPTH, FFN_HID, D_MODEL), FFN_HID),
        "final_norm": gain((D_MODEL,)),
    }


def reference(x_prompt, x_sample, cache_cmp_k, cache_cmp_v, cache_sel_k, cache_sel_v, cache_win_k, cache_win_v,
              state_hgrn, cache_mem_k, cache_mem_v, page_table, mem_prompt, norm_mix, w_in, cmp_pe_k, cmp_w1_k,
              cmp_w2_k, cmp_pe_v, cmp_w1_v, cmp_w2_v, nsa_out_norm, hg_lb_logits, hg_out_norm, w_out, norm_ca,
              norm_mem, ca_wq, ca_wk, ca_wv, ca_wo, norm_ffn, ffn_w_gate, ffn_w_up, ffn_w_down, final_norm):
    B, T = x_prompt.shape[:2]
    DB, TS = x_sample.shape[:2]
    past = page_table.shape[1] * PAGE_SIZE
    pos_p = jnp.arange(T)
    pos_s = past + jnp.arange(TS)
    lb_all = jnp.cumsum(jax.nn.softmax(hg_lb_logits.astype(jnp.float32), axis=0), axis=0)
    kvs = lambda a, n: a.reshape(a.shape[0], n, NSA_KV, HEAD_DIM)
    xp, xs = x_prompt, x_sample
    P = {n: [] for n in ['ck', 'cv', 'sk', 'sv', 'wk', 'wv', 'hg', 'mk', 'mv']}
    S = {n: [] for n in ['ck', 'cv', 'sk', 'sv', 'wk', 'wv', 'hg']}
    for l in range(DEPTH):
        lb = lb_all[l].reshape(HG_HEADS, HG_DK)
        cmpw = (cmp_pe_k[l], cmp_w1_k[l], cmp_w2_k[l], cmp_pe_v[l], cmp_w1_v[l], cmp_w2_v[l])
        qp, gp, kcp, vcp, ksp, vsp, kwp, vwp, hqp, hfp, hip, hgp = split_proj(rmsnorm(xp, norm_mix[l]), w_in[l])
        kcp, vcp, vsp, vwp = kvs(kcp, T), kvs(vcp, T), kvs(vsp, T), kvs(vwp, T)
        ksp = rope(kvs(ksp, T), pos_p)
        kwp = rope(kvs(kwp, T), pos_p)
        o_nsa = nsa_prompt(qp.reshape(B, T, NSA_HEADS, HEAD_DIM), gp.reshape(B, T, NSA_HEADS, 3),
                           kcp, vcp, ksp, vsp, kwp, vwp, *cmpw)
        o_hg, S_p = hgrn2_mix(hqp, hfp, hip, hgp, lb, hg_out_norm[l],
                              jnp.zeros((B, HG_HEADS, HG_DK, HG_DV), jnp.float32))
        xp = xp + jnp.concatenate([rmsnorm(o_nsa, nsa_out_norm[l]), o_hg], axis=-1) @ w_out[l]
        mn = rmsnorm(mem_prompt, norm_mem[l])
        mk = (mn @ ca_wk[l]).reshape(B, MEM_LEN, CA_HEADS, CA_HD)
        mv = (mn @ ca_wv[l]).reshape(B, MEM_LEN, CA_HEADS, CA_HD)
        xp = xp + cross_attn(rmsnorm(xp, norm_ca[l]), mk, mv, ca_wq[l], ca_wo[l])
        xp = xp + swiglu(rmsnorm(xp, norm_ffn[l]), ffn_w_gate[l], ffn_w_up[l], ffn_w_down[l])
        wbp = min(WINDOW, T)
        for n, a in zip(['ck', 'cv', 'sk', 'sv', 'wk', 'wv', 'hg', 'mk', 'mv'],
                        [kcp, vcp, ksp, vsp, kwp[:, -wbp:], vwp[:, -wbp:], S_p.astype(state_hgrn.dtype), mk, mv]):
            P[n].append(a)
        qs, gs, kcs, vcs, kss, vss, kws, vws, hqs, hfs, his, hgs = split_proj(rmsnorm(xs, norm_mix[l]), w_in[l])
        kcs, vcs, vss, vws = kvs(kcs, TS), kvs(vcs, TS), kvs(vss, TS), kvs(vws, TS)
        kss = rope(kvs(kss, TS), pos_s)
        kws = rope(kvs(kws, TS), pos_s)
        o_nsa_s, nwk, nwv = nsa_sample(qs.reshape(DB, TS, NSA_HEADS, HEAD_DIM), gs.reshape(DB, TS, NSA_HEADS, 3),
                                       kcs, vcs, kss, vss, kws, vws, pos_s, cache_cmp_k[l], cache_cmp_v[l],
                                       cache_sel_k[l], cache_sel_v[l], cache_win_k[l], cache_win_v[l],
                                       page_table, *cmpw)
        o_hg_s, S_s = hgrn2_mix(hqs, hfs, his, hgs, lb, hg_out_norm[l], state_hgrn[l].astype(jnp.float32))
        xs = xs + jnp.concatenate([rmsnorm(o_nsa_s, nsa_out_norm[l]), o_hg_s], axis=-1) @ w_out[l]
        xs = xs + cross_attn(rmsnorm(xs, norm_ca[l]), cache_mem_k[l], cache_mem_v[l], ca_wq[l], ca_wo[l])
        xs = xs + swiglu(rmsnorm(xs, norm_ffn[l]), ffn_w_gate[l], ffn_w_up[l], ffn_w_down[l])
        for n, a in zip(['ck', 'cv', 'sk', 'sv', 'wk', 'wv', 'hg'],
                        [kcs, vcs, kss, vss, nwk, nwv, S_s.astype(state_hgrn.dtype)]):
            S[n].append(a)
    y_prompt = rmsnorm(xp, final_norm)
    y_sample = rmsnorm(xs, final_norm)
    st = lambda lst: jnp.stack(lst, axis=0)
    return (y_prompt, y_sample,
            st(P['ck']), st(P['cv']), st(P['sk']), st(P['sv']), st(P['wk']), st(P['wv']), st(P['hg']),
            st(P['mk']), st(P['mv']),
            st(S['ck']), st(S['cv']), st(S['sk']), st(S['sv']), st(S['wk']), st(S['wv']), st(S['hg']))
```

```python
import functools

import numpy as np
import jax
import jax.numpy as jnp
from jax import lax
from jax.experimental import pallas as pl
from jax.experimental.pallas import tpu as pltpu

F32 = jnp.float32
BF16 = jnp.bfloat16
I32 = jnp.int32

D_MODEL = 1024
NSA_HEADS = 8
NSA_KV = 2
HEAD_DIM = 64
NSA_GROUP = NSA_HEADS // NSA_KV
NSA_W = NSA_HEADS * HEAD_DIM
CMP_STRIDE = 16
CMP_LEN = 32
CMP_HID = 128
SEL_BLK = 64
SEL_TOPK = 16
WINDOW = 512
ROPE_THETA = 500000.0
ROPE_DIM = HEAD_DIM // 4
HG_HEADS = 4
HG_DK = 128
HG_DV = 128
HG_W = HG_HEADS * HG_DV
CA_HEADS = 4
CA_HD = D_MODEL // CA_HEADS
PAGE_SIZE = 128
EPS = 1e-6
FORCE_SCORE = 1e4
NEG = -1e30
MASK_BIAS = -1e9
LANES = 128
VMEM_LIMIT = 56 * 1024 * 1024

C_Q = 0
C_KC, C_VC, C_KS, C_VS, C_KW, C_VW = 512, 640, 768, 896, 1024, 1152
C_AUG = 1280
C_HG = 1536
C_GT = 3584
N_PROJ = 3712


def _cparams(sem):
    return pltpu.CompilerParams(dimension_semantics=sem, vmem_limit_bytes=VMEM_LIMIT)


def _nt(a, b):
    return lax.dot_general(a, b, (((1,), (1,)), ((), ())), preferred_element_type=F32)


def _tn(a, b):
    return lax.dot_general(a, b, (((0,), (0,)), ((), ())), preferred_element_type=F32)


def _mm(a, b):
    return jnp.dot(a, b, preferred_element_type=F32)


def _rms(x, g):
    return x * lax.rsqrt(jnp.mean(x * x, axis=-1, keepdims=True) + EPS) * g


def _masked_softmax(s, mask, axis):
    s = jnp.where(mask, s, NEG)
    m = jnp.max(s, axis=axis, keepdims=True)
    e = jnp.where(mask, jnp.exp(s - m), 0.0)
    return e / jnp.maximum(jnp.sum(e, axis=axis, keepdims=True), 1e-20)


def _log2(n):
    l = int(n).bit_length() - 1
    assert (1 << l) == n, n
    return l


def _proj_in_kernel(x_ref, g_ref, w_ref, c_ref, sa_ref, sb_ref, oh_ref,
                    q_ref, qr_ref, kc_ref, vc_ref, ks_ref, vs_ref, kw_ref, vw_ref,
                    ksa_ref, vsb_ref, kwb_ref, vwb_ref, hm_ref, gt_ref):
    xn = _rms(x_ref[...], g_ref[...]).astype(BF16)
    c, sa, sb = c_ref[...], sa_ref[...], sb_ref[...]

    def mm(lo, hi):
        return _mm(xn, w_ref[:, lo:hi])

    def rope(y):
        n = y.shape[1] // LANES
        cc, aa, bb = (jnp.concatenate([t] * n, axis=1) if n > 1 else t for t in (c, sa, sb))
        w = y.shape[1]
        return y * cc + pltpu.roll(y, w - ROPE_DIM // 2, 1) * aa + pltpu.roll(y, ROPE_DIM // 2, 1) * bb

    yq = mm(C_Q, C_Q + NSA_W) * (HEAD_DIM ** -0.5)
    q_ref[...] = yq.astype(BF16)
    qr_ref[...] = rope(yq).astype(BF16)
    kc_ref[...] = mm(C_KC, C_KC + LANES)
    vc_ref[...] = mm(C_VC, C_VC + LANES)
    ks_ref[...] = rope(mm(C_KS, C_KS + LANES))
    vs = mm(C_VS, C_VS + LANES)
    vs_ref[...] = vs
    vsb_ref[...] = vs.astype(BF16)
    kw = rope(mm(C_KW, C_KW + LANES))
    kw_ref[...] = kw
    kwb_ref[...] = kw.astype(BF16)
    vw = mm(C_VW, C_VW + LANES)
    vw_ref[...] = vw
    vwb_ref[...] = vw.astype(BF16)
    ksa_ref[...] = (rope(mm(C_AUG, C_AUG + 2 * LANES)) + oh_ref[...]).astype(BF16)
    hm_ref[...] = mm(C_HG, C_HG + 4 * HG_W)
    gt_ref[...] = mm(C_GT, C_GT + LANES)


def _proj_in(x2d, g, w, tabs, oh, tm):
    m = x2d.shape[0]
    nt_tab = tabs[0].shape[0] // tm
    row = lambda i: (i, 0)
    tab = lambda i: (i % nt_tab, 0)
    const = lambda i: (0, 0)
    widths = [(NSA_W, BF16), (NSA_W, BF16)] + [(LANES, F32)] * 6 + [(2 * LANES, BF16)] + [(LANES, BF16)] * 3 \
        + [(4 * HG_W, F32), (LANES, F32)]
    return pl.pallas_call(
        _proj_in_kernel,
        grid=(m // tm,),
        in_specs=[pl.BlockSpec((tm, D_MODEL), row), pl.BlockSpec((1, D_MODEL), const),
                  pl.BlockSpec((D_MODEL, N_PROJ), const),
                  pl.BlockSpec((tm, LANES), tab), pl.BlockSpec((tm, LANES), tab), pl.BlockSpec((tm, LANES), tab),
                  pl.BlockSpec((tm, 2 * LANES), tab)],
        out_specs=[pl.BlockSpec((tm, wd), row) for wd, _ in widths],
        out_shape=[jax.ShapeDtypeStruct((m, wd), dt) for wd, dt in widths],
        compiler_params=_cparams(("parallel",)),
        name="proj_in",
    )(x2d, g, w, *tabs, oh)


def _page_copy(pool_hbm, page, buf, slot, p, rows, sem):
    return pltpu.make_async_copy(pool_hbm.at[page], buf.at[slot, pl.ds(p * rows, rows)], sem)


def _gather_start(pt_ref, pools, bufs, sems, b, slot, n_pages, rows):
    def body(p, carry):
        page = pt_ref[b * n_pages + p]
        for pool, buf, sem in zip(pools, bufs, sems):
            _page_copy(pool, page, buf, slot, p, rows, sem.at[slot]).start()
        return carry
    lax.fori_loop(0, n_pages, body, 0)


def _gather_wait(pools, bufs, sems, slot, n_pages, rows):
    def body(p, carry):
        for pool, buf, sem in zip(pools, bufs, sems):
            _page_copy(pool, 0, buf, slot, p, rows, sem.at[slot]).wait()
        return carry
    lax.fori_loop(0, n_pages, body, 0)


def _gather_step(pt_ref, pools, bufs, sems, n_pages, rows):
    b = pl.program_id(0)
    nb = pl.num_programs(0)
    slot = b % 2

    @pl.when(b == 0)
    def _():
        _gather_start(pt_ref, pools, bufs, sems, 0, 0, n_pages, rows)

    @pl.when(b + 1 < nb)
    def _():
        _gather_start(pt_ref, pools, bufs, sems, b + 1, 1 - slot, n_pages, rows)

    _gather_wait(pools, bufs, sems, slot, n_pages, rows)
    return slot


def _compress_kernel(pt_ref, poolk_hbm, poolv_hbm, pek_ref, pev_ref, w1k_ref, w1v_ref, w2k_ref, w2v_ref,
                     kc_ref, vc_ref, xk_buf, xv_buf, semk, semv, *, n_pages):
    segs_per_page = PAGE_SIZE // CMP_STRIDE
    slot = _gather_step(pt_ref, (poolk_hbm, poolv_hbm), (xk_buf, xv_buf), (semk, semv), n_pages, segs_per_page)
    nseg = n_pages * segs_per_page
    row = lax.broadcasted_iota(I32, (nseg, 1), 0)
    for xbuf, pe, w1, w2, out in ((xk_buf, pek_ref, w1k_ref, w2k_ref, kc_ref),
                                  (xv_buf, pev_ref, w1v_ref, w2v_ref, vc_ref)):
        x = xbuf[slot]
        ha = _mm((x + pe[0]).astype(BF16), w1[0])
        hb = _mm((x + pe[1]).astype(BF16), w1[1])
        h = jax.nn.gelu(ha + pltpu.roll(hb, nseg - 1, 0))
        h = jnp.where(row < nseg - 1, h, 0.0)
        out[0] = _mm(h.astype(BF16), w2[...]).astype(BF16)


def _compress(page_table_flat, pool_k, pool_v, pe_k, pe_v, w1_k, w1_v, w2_k, w2_v, n_batch, n_pages):
    segs = PAGE_SIZE // CMP_STRIDE
    width = CMP_STRIDE * LANES
    nseg = n_pages * segs
    const2 = lambda b, pt: (0, 0)
    const3 = lambda b, pt: (0, 0, 0)
    gs = pltpu.PrefetchScalarGridSpec(
        num_scalar_prefetch=1,
        grid=(n_batch,),
        in_specs=[pl.BlockSpec(memory_space=pl.ANY), pl.BlockSpec(memory_space=pl.ANY),
                  pl.BlockSpec((2, 1, width), const3), pl.BlockSpec((2, 1, width), const3),
                  pl.BlockSpec((2, width, 2 * CMP_HID), const3), pl.BlockSpec((2, width, 2 * CMP_HID), const3),
                  pl.BlockSpec((2 * CMP_HID, LANES), const2), pl.BlockSpec((2 * CMP_HID, LANES), const2)],
        out_specs=[pl.BlockSpec((1, nseg, LANES), lambda b, pt: (b, 0, 0))] * 2,
        scratch_shapes=[pltpu.VMEM((2, nseg, width), F32), pltpu.VMEM((2, nseg, width), F32),
                        pltpu.SemaphoreType.DMA((2,)), pltpu.SemaphoreType.DMA((2,))],
    )
    return pl.pallas_call(
        functools.partial(_compress_kernel, n_pages=n_pages),
        grid_spec=gs,
        out_shape=[jax.ShapeDtypeStruct((n_batch, nseg, LANES), BF16)] * 2,
        compiler_params=_cparams(("arbitrary",)),
        name="compress",
    )(page_table_flat, pool_k, pool_v, pe_k, pe_v, w1_k, w1_v, w2_k, w2_v)


def _nsa_prompt_kernel(q_ref, qr_ref, gt_ref, kc_ref, vc_ref, ksa_ref, vs_ref, kw_ref, vw_ref, o_ref,
                       psum_scr, sc_scr, *, tq, tk, ncp, nsel, topk):
    i = pl.program_id(1)
    s0 = i * tq
    rows = NSA_GROUP * tq
    lane = lax.broadcasted_iota(I32, (tq, LANES), 1)
    upper = lane >= HEAD_DIM
    gsig = jax.nn.sigmoid(gt_ref[0])
    kc = kc_ref[0]
    vc = vc_ref[0]
    t_rows = s0 + (lax.broadcasted_iota(I32, (rows, 1), 0) & (tq - 1))
    out_slots = [None] * NSA_GROUP
    for g in range(NSA_KV):
        mine = upper if g == 1 else jnp.logical_not(upper)

        def stack(ref, fill):
            return jnp.concatenate(
                [jnp.where(mine, ref[0, :, r * LANES:(r + 1) * LANES], fill) for r in range(NSA_GROUP)], axis=0)

        qc = stack(q_ref, 0)
        cend = lax.broadcasted_iota(I32, (rows, ncp), 1) * CMP_STRIDE + (CMP_LEN - 1)
        p_c = _masked_softmax(_nt(qc, kc), cend <= t_rows, 1)
        o_c = _mm(p_c.astype(BF16), vc)
        cend_t = lax.broadcasted_iota(I32, (ncp, rows), 0) * CMP_STRIDE + (CMP_LEN - 1)
        t_cols = s0 + (lax.broadcasted_iota(I32, (ncp, rows), 1) & (tq - 1))
        p_t = _masked_softmax(_nt(kc, qc), cend_t <= t_cols, 0)
        psum = p_t[:, 0:tq]
        for r in range(1, NSA_GROUP):
            psum = psum + p_t[:, r * tq:(r + 1) * tq]
        psum_scr[...] = psum
        cps = SEL_BLK // CMP_STRIDE
        imp = psum_scr[pl.ds(0, nsel, stride=cps), :]
        for u in range(1, cps):
            imp = imp + psum_scr[pl.ds(u, nsel, stride=cps), :]
        j = lax.broadcasted_iota(I32, (nsel, tq), 0)
        cur = (s0 + lax.broadcasted_iota(I32, (nsel, tq), 1)) >> _log2(SEL_BLK)
        avail = j <= cur
        forced = (j == 0) | (j == cur) | (j == cur - 1)
        score = jnp.where(avail, jnp.where(forced, FORCE_SCORE, imp), -FORCE_SCORE)
        sc_scr[...] = score

        def rank_body(jp, cnt):
            other = sc_scr[pl.ds(jp, 1), :]
            beats = (other > score) | ((other == score) & (jp < j))
            return cnt + beats.astype(I32)

        cnt = lax.fori_loop(0, nsel, rank_body, jnp.zeros((nsel, tq), I32))
        bias_t = jnp.where((cnt < topk) & avail, 0.0, MASK_BIAS)
        if nsel < HEAD_DIM:
            bias_t = jnp.concatenate([bias_t, jnp.full((HEAD_DIM - nsel, tq), MASK_BIAS, F32)], axis=0)
        bias = jnp.concatenate([bias_t, bias_t], axis=0).T.astype(BF16)
        qs = stack(qr_ref, bias)
        glanes = slice(g * LANES, (g + 1) * LANES)

        def sel_tile(k0, carry, causal):
            m, l, acc = carry
            s = _nt(qs, ksa_ref[0, pl.ds(k0, tk), glanes])
            if causal:
                kpos = k0 + lax.broadcasted_iota(I32, (rows, tk), 1)
                s = jnp.where(kpos <= t_rows, s, NEG)
            m_new = jnp.maximum(m, jnp.max(s, axis=1, keepdims=True))
            a = jnp.exp(m - m_new)
            p = jnp.exp(s - m_new)
            l = a * l + jnp.sum(p, axis=1, keepdims=True)
            acc = a * acc + _mm(p.astype(BF16), vs_ref[0, pl.ds(k0, tk), :])
            return m_new, l, acc

        n_full = s0 // tk
        carry = (jnp.full((rows, 1), NEG, F32), jnp.zeros((rows, 1), F32), jnp.zeros((rows, LANES), F32))
        carry = lax.fori_loop(0, n_full, lambda kk, c: sel_tile(pl.multiple_of(kk * tk, tk), c, False), carry)
        _, l_s, acc_s = sel_tile(pl.multiple_of(n_full * tk, tk), carry, True)
        o_s = acc_s / jnp.maximum(l_s, 1e-20)
        qw = stack(qr_ref, 0)
        wl = WINDOW + tq
        ws = pl.multiple_of(jnp.maximum(s0 - WINDOW, 0), tq)
        kpos = ws + lax.broadcasted_iota(I32, (rows, wl), 1)
        wmask = (kpos <= t_rows) & (kpos > t_rows - WINDOW)
        p_w = _masked_softmax(_nt(qw, kw_ref[0, pl.ds(ws, wl), :]), wmask, 1)
        o_w = _mm(p_w.astype(BF16), vw_ref[0, pl.ds(ws, wl), :])
        for r in range(NSA_GROUP):
            h = g * NSA_GROUP + r
            rs = slice(r * tq, (r + 1) * tq)
            res = (gsig[:, 3 * h:3 * h + 1] * o_c[rs] + gsig[:, 3 * h + 1:3 * h + 2] * o_s[rs]
                   + gsig[:, 3 * h + 2:3 * h + 3] * o_w[rs])
            out_slots[r] = res if g == 0 else jnp.where(upper, res, out_slots[r])
    o_ref[0] = jnp.concatenate(out_slots, axis=1)


def _nsa_prompt(q, qr, gt, kc, vc, ksa, vsb, kwb, vwb, tq, tk):
    b, t = q.shape[:2]
    ncp = t // CMP_STRIDE
    nsel = t // SEL_BLK
    assert t % tk == 0 and tk % tq == 0 and t >= WINDOW + tq and nsel <= HEAD_DIM and WINDOW % tq == 0
    blk = lambda bb, i: (bb, i, 0)
    whole = lambda bb, i: (bb, 0, 0)
    return pl.pallas_call(
        functools.partial(_nsa_prompt_kernel, tq=tq, tk=tk, ncp=ncp, nsel=nsel, topk=min(SEL_TOPK, nsel)),
        grid=(b, t // tq),
        in_specs=[pl.BlockSpec((1, tq, NSA_W), blk), pl.BlockSpec((1, tq, NSA_W), blk),
                  pl.BlockSpec((1, tq, LANES), blk),
                  pl.BlockSpec((1, ncp, LANES), whole), pl.BlockSpec((1, ncp, LANES), whole),
                  pl.BlockSpec((1, t, 2 * LANES), whole), pl.BlockSpec((1, t, LANES), whole),
                  pl.BlockSpec((1, t, LANES), whole), pl.BlockSpec((1, t, LANES), whole)],
        out_specs=pl.BlockSpec((1, tq, NSA_W), blk),
        out_shape=jax.ShapeDtypeStruct((b, t, NSA_W), F32),
        scratch_shapes=[pltpu.VMEM((ncp, tq), F32), pltpu.VMEM((nsel, tq), F32)],
        compiler_params=_cparams(("parallel", "parallel")),
        name="nsa_prompt",
    )(q, qr, gt, kc, vc, ksa, vsb, kwb, vwb)


def _nsa_sample_kernel(pt_ref, q_ref, qr_ref, gt_ref, kc_ref, vc_ref, ksn_ref, vsn_ref, kwn_ref, vwn_ref,
                       wink_ref, winv_ref, oh_ref, gmat_ref, poolk_hbm, poolv_hbm,
                       o_ref, nwk_ref, nwv_ref, kbuf, vbuf, semk, semv,
                       *, n_pages, past, ts, tp, tn, wb, topk_past):
    slot = _gather_step(pt_ref, (poolk_hbm, poolv_hbm), (kbuf, vbuf), (semk, semv), n_pages, PAGE_SIZE)
    nseg = n_pages * (PAGE_SIZE // CMP_STRIDE)
    npb = past // SEL_BLK
    rows = NSA_HEADS * tp
    lane = lax.broadcasted_iota(I32, (tp, LANES), 1)
    upper = lane >= HEAD_DIM

    def stack(ref):
        parts = []
        for g in range(NSA_KV):
            mine = upper if g == 1 else jnp.logical_not(upper)
            parts += [jnp.where(mine, ref[0, :, r * LANES:(r + 1) * LANES], 0) for r in range(NSA_GROUP)]
        return jnp.concatenate(parts, axis=0)

    t_rows = lax.broadcasted_iota(I32, (rows, 1), 0) & (tp - 1)
    qc = stack(q_ref)
    cend = lax.broadcasted_iota(I32, (rows, nseg), 1) * CMP_STRIDE + (CMP_LEN - 1)
    p_c = _masked_softmax(_nt(qc, kc_ref[0]), cend <= past + t_rows, 1)
    o_c = _mm(p_c.astype(BF16), vc_ref[0])
    impn = []
    for g in range(NSA_KV):
        acc = p_c[(g * NSA_GROUP) * tp:(g * NSA_GROUP + 1) * tp]
        for r in range(1, NSA_GROUP):
            acc = acc + p_c[(g * NSA_GROUP + r) * tp:(g * NSA_GROUP + r + 1) * tp]
        impn.append(acc)
    impn = jnp.concatenate(impn, axis=0)
    imp = jnp.dot(impn, gmat_ref[...], precision=lax.Precision.HIGHEST, preferred_element_type=F32)
    j = lax.broadcasted_iota(I32, (NSA_KV * tp, LANES), 1)
    avail = j < npb
    forced = (j == 0) | (j == npb - 1)
    score = jnp.where(avail, jnp.where(forced, FORCE_SCORE, imp), -FORCE_SCORE)
    cnt = jnp.zeros((NSA_KV * tp, LANES), I32)
    for jp in range(npb):
        other = score[:, jp:jp + 1]
        beats = (other > score) | ((other == score) & (jp < j))
        cnt = cnt + beats.astype(I32)
    bias = jnp.where((cnt < topk_past) & avail, 0.0, MASK_BIAS).astype(BF16)
    bias = jnp.concatenate([bias[0:tp]] * NSA_GROUP + [bias[tp:2 * tp]] * NSA_GROUP, axis=0)
    qs = stack(qr_ref)
    qa = jnp.concatenate([qs, bias], axis=1)
    key_t = lax.broadcasted_iota(I32, (rows, tn), 1)
    new_mask = (key_t <= t_rows) & (key_t < ts)
    s_n = jnp.where(new_mask, _nt(qs, ksn_ref[0].astype(BF16)), NEG)
    m0 = jnp.max(s_n, axis=1, keepdims=True)
    p_n = jnp.where(new_mask, jnp.exp(s_n - m0), 0.0)
    l0 = jnp.sum(p_n, axis=1, keepdims=True)
    acc0 = _mm(p_n.astype(BF16), vsn_ref[0].astype(BF16))

    def page_body(p, carry):
        m, l, acc = carry
        r0 = pl.multiple_of(p * PAGE_SIZE, PAGE_SIZE)
        ka = jnp.concatenate([kbuf[slot, pl.ds(r0, PAGE_SIZE), :].astype(BF16), oh_ref[pl.ds(r0, PAGE_SIZE), :]],
                             axis=1)
        s = _nt(qa, ka)
        m_new = jnp.maximum(m, jnp.max(s, axis=1, keepdims=True))
        a = jnp.exp(m - m_new)
        pp = jnp.exp(s - m_new)
        l = a * l + jnp.sum(pp, axis=1, keepdims=True)
        acc = a * acc + _mm(pp.astype(BF16), vbuf[slot, pl.ds(r0, PAGE_SIZE), :].astype(BF16))
        return m_new, l, acc

    _, l_s, acc_s = lax.fori_loop(0, n_pages, page_body, (m0, l0, acc0))
    o_s = acc_s / jnp.maximum(l_s, 1e-20)
    wk = wink_ref[0]
    wv = winv_ref[0]
    iw = lax.broadcasted_iota(I32, (rows, wb), 1)
    wmask = (iw > t_rows + (wb - WINDOW)) & (iw >= wb - past)
    s_w = jnp.where(wmask, _nt(qs, wk.astype(BF16)), NEG)
    s_wn = jnp.where(new_mask, _nt(qs, kwn_ref[0].astype(BF16)), NEG)
    m_w = jnp.maximum(jnp.max(s_w, axis=1, keepdims=True), jnp.max(s_wn, axis=1, keepdims=True))
    e_w = jnp.where(wmask, jnp.exp(s_w - m_w), 0.0)
    e_wn = jnp.where(new_mask, jnp.exp(s_wn - m_w), 0.0)
    inv = 1.0 / jnp.maximum(jnp.sum(e_w, axis=1, keepdims=True) + jnp.sum(e_wn, axis=1, keepdims=True), 1e-20)
    o_w = _mm((e_w * inv).astype(BF16), wv.astype(BF16)) + _mm((e_wn * inv).astype(BF16), vwn_ref[0].astype(BF16))
    gsig = jax.nn.sigmoid(gt_ref[0])
    slots = []
    for r in range(NSA_GROUP):
        per_g = []
        for g in range(NSA_KV):
            h = g * NSA_GROUP + r
            rs = slice(h * tp, (h + 1) * tp)
            per_g.append(gsig[:, 3 * h:3 * h + 1] * o_c[rs] + gsig[:, 3 * h + 1:3 * h + 2] * o_s[rs]
                         + gsig[:, 3 * h + 2:3 * h + 3] * o_w[rs])
        slots.append(jnp.where(upper, per_g[1], per_g[0]))
    o_ref[0] = jnp.concatenate(slots, axis=1)
    sub = lax.broadcasted_iota(I32, (8, LANES), 0)
    for win, new_ref, out in ((wk, kwn_ref, nwk_ref), (wv, vwn_ref, nwv_ref)):
        rolled = pltpu.roll(win, wb - ts, 0)
        shifted = pltpu.roll(new_ref[0, 0:8, :], 8 - ts, 0)
        out[0, 0:wb - 8, :] = rolled[0:wb - 8]
        out[0, wb - 8:wb, :] = jnp.where(sub >= 8 - ts, shifted, rolled[wb - 8:wb])


def _nsa_sample(page_table_flat, q, qr, gt, kc, vc, ksn, vsn, kwn, vwn, win_k, win_v, oh, gmat, pool_k, pool_v,
                n_pages, past, ts):
    db, tp = q.shape[:2]
    tn = ksn.shape[1]
    wb = win_k.shape[1]
    nseg = n_pages * (PAGE_SIZE // CMP_STRIDE)
    npb = past // SEL_BLK
    assert ts <= 8 and npb <= LANES and wb >= 8 and past % PAGE_SIZE == 0
    per_b = lambda b, pt: (b, 0, 0)
    const2 = lambda b, pt: (0, 0)
    gs = pltpu.PrefetchScalarGridSpec(
        num_scalar_prefetch=1,
        grid=(db,),
        in_specs=[pl.BlockSpec((1, tp, NSA_W), per_b), pl.BlockSpec((1, tp, NSA_W), per_b),
                  pl.BlockSpec((1, tp, LANES), per_b),
                  pl.BlockSpec((1, nseg, LANES), per_b), pl.BlockSpec((1, nseg, LANES), per_b),
                  pl.BlockSpec((1, tn, LANES), per_b), pl.BlockSpec((1, tn, LANES), per_b),
                  pl.BlockSpec((1, tn, LANES), per_b), pl.BlockSpec((1, tn, LANES), per_b),
                  pl.BlockSpec((1, wb, LANES), per_b), pl.BlockSpec((1, wb, LANES), per_b),
                  pl.BlockSpec((past, LANES), const2), pl.BlockSpec((nseg, LANES), const2),
                  pl.BlockSpec(memory_space=pl.ANY), pl.BlockSpec(memory_space=pl.ANY)],
        out_specs=[pl.BlockSpec((1, tp, NSA_W), per_b), pl.BlockSpec((1, wb, LANES), per_b),
                   pl.BlockSpec((1, wb, LANES), per_b)],
        scratch_shapes=[pltpu.VMEM((2, past, LANES), F32), pltpu.VMEM((2, past, LANES), F32),
                        pltpu.SemaphoreType.DMA((2,)), pltpu.SemaphoreType.DMA((2,))],
    )
    return pl.pallas_call(
        functools.partial(_nsa_sample_kernel, n_pages=n_pages, past=past, ts=ts, tp=tp, tn=tn, wb=wb,
                          topk_past=min(SEL_TOPK, npb + 1) - 1),
        grid_spec=gs,
        out_shape=[jax.ShapeDtypeStruct((db, tp, NSA_W), F32), jax.ShapeDtypeStruct((db, wb, LANES), F32),
                   jax.ShapeDtypeStruct((db, wb, LANES), F32)],
        compiler_params=_cparams(("arbitrary",)),
        name="nsa_sample",
    )(page_table_flat, q, qr, gt, kc, vc, ksn, vsn, kwn, vwn, win_k, win_v, oh, gmat, pool_k, pool_v)


def _hgrn_kernel(hm_ref, lb_ref, og_ref, s0_ref, o_ref, sout_ref, st_scr, *, chunk, t_real, levels):
    c = pl.program_id(1)
    nc = pl.num_programs(1)

    @pl.when(c == 0)
    def _():
        for hd in range(HG_HEADS):
            st_scr[hd] = s0_ref[0, hd].T

    row = lax.broadcasted_iota(I32, (chunk, 1), 0)
    valid = (c * chunk + row) < t_real
    row_c = lax.broadcasted_iota(I32, (chunk, chunk), 0)
    col_c = lax.broadcasted_iota(I32, (chunk, chunk), 1)
    outs = []
    for hd in range(HG_HEADS):
        sl = slice(hd * HG_DK, (hd + 1) * HG_DK)
        q = hm_ref[0, :, hd * HG_DK:(hd + 1) * HG_DK]
        z = hm_ref[0, :, HG_W + hd * HG_DK:HG_W + (hd + 1) * HG_DK]
        v = hm_ref[0, :, 2 * HG_W + hd * HG_DV:2 * HG_W + (hd + 1) * HG_DV]
        gate = hm_ref[0, :, 3 * HG_W + hd * HG_DV:3 * HG_W + (hd + 1) * HG_DV]
        lb = lb_ref[:, sl]
        logf = jnp.where(valid, jnp.log(lb + (1.0 - lb) * jax.nn.sigmoid(z)), 0.0)
        k = jnp.where(valid, (1.0 - lb) * jax.nn.sigmoid(-z), 0.0)
        b = logf
        step = 1
        while step < chunk:
            b = b + jnp.where(row >= step, pltpu.roll(b, step, 0), 0.0)
            step *= 2
        st = st_scr[hd]
        o = _nt((q * jnp.exp(b)).astype(BF16), st.astype(BF16))
        if levels:
            a_mat = jnp.zeros((chunk, chunk), F32)
            for h in levels:
                m = jnp.concatenate(
                    [jnp.broadcast_to(b[blk * 2 * h + h - 1:blk * 2 * h + h, :], (2 * h, HG_DK))
                     for blk in range(chunk // (2 * h))], axis=0)
                second = ((row >> _log2(h)) & 1) == 1
                qs = jnp.where(second, q * jnp.exp(jnp.minimum(b - m, 0.0)), 0.0)
                ks = jnp.where(second, 0.0, k * jnp.exp(jnp.minimum(m - b, 0.0)))
                same = (row_c >> _log2(2 * h)) == (col_c >> _log2(2 * h))
                a_mat = a_mat + jnp.where(same, _nt(qs.astype(BF16), ks.astype(BF16)), 0.0)
            o = o + _mm(a_mat.astype(BF16), v.astype(BF16))
        for d in range(min(8, chunk)):
            kd, bd, vd = (x if d == 0 else pltpu.roll(x, d, 0) for x in (k, b, v))
            p = jnp.where((row & 7) >= d, q * kd * jnp.exp(jnp.minimum(b - bd, 0.0)), 0.0)
            o = o + jnp.sum(p, axis=1, keepdims=True) * vd
        bl = b[chunk - 1:chunk, :]
        st_scr[hd] = jnp.exp(bl) * st + _tn(v.astype(BF16), (k * jnp.exp(bl - b)).astype(BF16))
        outs.append(_rms(o, og_ref[:, sl]) * (gate * jax.nn.sigmoid(gate)))
    o_ref[0] = jnp.concatenate(outs, axis=1)

    @pl.when(c == nc - 1)
    def _():
        for hd in range(HG_HEADS):
            sout_ref[0, hd] = st_scr[hd].T


def _hgrn(hm, lb, og, s0, chunk, t_real):
    b, tpad = hm.shape[:2]
    levels = tuple(h for h in (64, 32, 16, 8) if 2 * h <= chunk)
    return pl.pallas_call(
        functools.partial(_hgrn_kernel, chunk=chunk, t_real=t_real, levels=levels),
        grid=(b, tpad // chunk),
        in_specs=[pl.BlockSpec((1, chunk, 4 * HG_W), lambda bb, c: (bb, c, 0)),
                  pl.BlockSpec((1, HG_W), lambda bb, c: (0, 0)), pl.BlockSpec((1, HG_W), lambda bb, c: (0, 0)),
                  pl.BlockSpec((1, HG_HEADS, HG_DK, HG_DV), lambda bb, c: (bb, 0, 0, 0))],
        out_specs=[pl.BlockSpec((1, chunk, HG_W), lambda bb, c: (bb, c, 0)),
                   pl.BlockSpec((1, HG_HEADS, HG_DK, HG_DV), lambda bb, c: (bb, 0, 0, 0))],
        out_shape=[jax.ShapeDtypeStruct((b, tpad, HG_W), F32),
                   jax.ShapeDtypeStruct((b, HG_HEADS, HG_DK, HG_DV), F32)],
        scratch_shapes=[pltpu.VMEM((HG_HEADS, HG_DV, HG_DK), F32)],
        compiler_params=_cparams(("parallel", "arbitrary")),
        name="hgrn",
    )(hm, lb, og, s0)


def _mixout_kernel(x_ref, on_ref, oh_ref, gn_ref, woa_ref, wob_ref, gca_ref, wq_ref, x1_ref, q_ref):
    a = _rms(on_ref[...], gn_ref[...]).astype(BF16)
    x1 = x_ref[...] + _mm(a, woa_ref[...]) + _mm(oh_ref[...].astype(BF16), wob_ref[...])
    x1_ref[...] = x1
    hn = _rms(x1, gca_ref[...]).astype(BF16)
    q_ref[...] = (_mm(hn, wq_ref[...]) * (CA_HD ** -0.5)).astype(BF16)


def _mixout(x, o_nsa, o_hg, gn, woa, wob, gca, wq, tm):
    m = x.shape[0]
    row = lambda i: (i, 0)
    const = lambda i: (0, 0)
    return pl.pallas_call(
        _mixout_kernel,
        grid=(m // tm,),
        in_specs=[pl.BlockSpec((tm, D_MODEL), row), pl.BlockSpec((tm, NSA_W), row), pl.BlockSpec((tm, HG_W), row),
                  pl.BlockSpec((1, NSA_W), const), pl.BlockSpec((NSA_W, D_MODEL), const),
                  pl.BlockSpec((HG_W, D_MODEL), const), pl.BlockSpec((1, D_MODEL), const),
                  pl.BlockSpec((D_MODEL, D_MODEL), const)],
        out_specs=[pl.BlockSpec((tm, D_MODEL), row), pl.BlockSpec((tm, D_MODEL), row)],
        out_shape=[jax.ShapeDtypeStruct((m, D_MODEL), F32), jax.ShapeDtypeStruct((m, D_MODEL), BF16)],
        compiler_params=_cparams(("parallel",)),
        name="mixout",
    )(x, o_nsa, o_hg, gn, woa, wob, gca, wq)


def _norm_mm_kernel(x_ref, g_ref, w_ref, o_ref):
    o_ref[...] = _mm(_rms(x_ref[...], g_ref[...]).astype(BF16), w_ref[...])


def _norm_mm(x, g, w, tm):
    m, n = x.shape[0], w.shape[1]
    return pl.pallas_call(
        _norm_mm_kernel,
        grid=(m // tm,),
        in_specs=[pl.BlockSpec((tm, D_MODEL), lambda i: (i, 0)), pl.BlockSpec((1, D_MODEL), lambda i: (0, 0)),
                  pl.BlockSpec((D_MODEL, n), lambda i: (0, 0))],
        out_specs=pl.BlockSpec((tm, n), lambda i: (i, 0)),
        out_shape=jax.ShapeDtypeStruct((m, n), F32),
        compiler_params=_cparams(("parallel",)),
        name="norm_mm",
    )(x, g, w)


def _ca_kernel(q_ref, mk_ref, mv_ref, o_ref):
    outs = []
    for hh in range(CA_HEADS):
        sl = slice(hh * CA_HD, (hh + 1) * CA_HD)
        s = _nt(q_ref[0, :, sl], mk_ref[0, :, sl].astype(BF16))
        e = jnp.exp(s - jnp.max(s, axis=1, keepdims=True))
        p = e / jnp.sum(e, axis=1, keepdims=True)
        outs.append(_mm(p.astype(BF16), mv_ref[0, :, sl].astype(BF16)))
    o_ref[0] = jnp.concatenate(outs, axis=1).astype(BF16)


def _cross_attn(q, mk, mv, tm):
    b, t = q.shape[:2]
    ml = mk.shape[1]
    return pl.pallas_call(
        _ca_kernel,
        grid=(b, t // tm),
        in_specs=[pl.BlockSpec((1, tm, D_MODEL), lambda bb, i: (bb, i, 0)),
                  pl.BlockSpec((1, ml, D_MODEL), lambda bb, i: (bb, 0, 0)),
                  pl.BlockSpec((1, ml, D_MODEL), lambda bb, i: (bb, 0, 0))],
        out_specs=pl.BlockSpec((1, tm, D_MODEL), lambda bb, i: (bb, i, 0)),
        out_shape=jax.ShapeDtypeStruct((b, t, D_MODEL), BF16),
        compiler_params=_cparams(("parallel", "parallel")),
        name="cross_attn",
    )(q, mk, mv)


def _mm_res_kernel(a_ref, w_ref, r_ref, o_ref):
    o_ref[...] = r_ref[...] + _mm(a_ref[...], w_ref[...])


def _mm_res(a, w, res, tm):
    m, k = a.shape
    n = w.shape[1]
    return pl.pallas_call(
        _mm_res_kernel,
        grid=(m // tm,),
        in_specs=[pl.BlockSpec((tm, k), lambda i: (i, 0)), pl.BlockSpec((k, n), lambda i: (0, 0)),
                  pl.BlockSpec((tm, n), lambda i: (i, 0))],
        out_specs=pl.BlockSpec((tm, n), lambda i: (i, 0)),
        out_shape=jax.ShapeDtypeStruct((m, n), F32),
        compiler_params=_cparams(("parallel",)),
        name="mm_res",
    )(a, w, res)


def _ffn_kernel(x_ref, gn_ref, wg_ref, wu_ref, wd_ref, gf_ref, y_ref, h_scr, acc_scr):
    j = pl.program_id(1)

    @pl.when(j == 0)
    def _():
        h_scr[...] = _rms(x_ref[...], gn_ref[...]).astype(BF16)
        acc_scr[...] = jnp.zeros_like(acc_scr)

    h = h_scr[...]
    gate = _mm(h, wg_ref[...])
    act = (gate * jax.nn.sigmoid(gate)) * _mm(h, wu_ref[...])
    acc_scr[...] += _mm(act.astype(BF16), wd_ref[...])

    @pl.when(j == pl.num_programs(1) - 1)
    def _():
        y_ref[...] = _rms(x_ref[...] + acc_scr[...], gf_ref[...])


def _ffn(x, gn, wg, wu, wd, gf, tm, th):
    m = x.shape[0]
    hid = wg.shape[1]
    return pl.pallas_call(
        _ffn_kernel,
        grid=(m // tm, hid // th),
        in_specs=[pl.BlockSpec((tm, D_MODEL), lambda i, j: (i, 0)), pl.BlockSpec((1, D_MODEL), lambda i, j: (0, 0)),
                  pl.BlockSpec((D_MODEL, th), lambda i, j: (0, j)), pl.BlockSpec((D_MODEL, th), lambda i, j: (0, j)),
                  pl.BlockSpec((th, D_MODEL), lambda i, j: (j, 0)), pl.BlockSpec((1, D_MODEL), lambda i, j: (0, 0))],
        out_specs=pl.BlockSpec((tm, D_MODEL), lambda i, j: (i, 0)),
        out_shape=jax.ShapeDtypeStruct((m, D_MODEL), F32),
        scratch_shapes=[pltpu.VMEM((tm, D_MODEL), BF16), pltpu.VMEM((tm, D_MODEL), F32)],
        compiler_params=_cparams(("parallel", "arbitrary")),
        name="ffn",
    )(x, gn, wg, wu, wd, gf)


def _pair(a, axis):
    shp = a.shape
    a = a.reshape(shp[:axis] + (NSA_KV, NSA_GROUP, HEAD_DIM) + shp[axis + 1:])
    a = jnp.swapaxes(a, axis, axis + 1)
    return a.reshape(shp)


def _rope_tables(pos):
    half = ROPE_DIM // 2
    inv = ROPE_THETA ** (-jnp.arange(half, dtype=F32) / half)
    ang = pos.astype(F32)[:, None] * inv[None, :]
    cos, sin = jnp.cos(ang), jnp.sin(ang)
    l = np.arange(LANES) % HEAD_DIM
    idx = l % half
    c = jnp.where(l < ROPE_DIM, cos[:, idx], 1.0)
    sa = jnp.where(l < half, -sin[:, idx], 0.0)
    sb = jnp.where((l >= half) & (l < ROPE_DIM), sin[:, idx], 0.0)
    return c, sa, sb


def _prep_w_in(w):
    offs = np.cumsum([0, NSA_W, 3 * NSA_HEADS] + [NSA_KV * HEAD_DIM] * 6 + [HG_W] * 4)
    seg = lambda i: w[:, offs[i]:offs[i + 1]]
    wq = _pair(seg(0), 1)
    wks = seg(4)
    z = jnp.zeros((D_MODEL, HEAD_DIM), w.dtype)
    aug = jnp.concatenate([wks[:, :HEAD_DIM], z, z, wks[:, HEAD_DIM:]], axis=1)
    gates = jnp.pad(seg(1), ((0, 0), (0, LANES - 3 * NSA_HEADS)))
    out = jnp.concatenate([wq] + [seg(i) for i in range(2, 8)] + [aug] + [seg(i) for i in range(8, 12)] + [gates],
                          axis=1)
    assert out.shape[1] == N_PROJ
    return out.astype(BF16)


def _prep_compress(pe, w1, w2):
    pe_big = jnp.broadcast_to(pe.reshape(2, CMP_STRIDE, 1, HEAD_DIM), (2, CMP_STRIDE, NSA_KV, HEAD_DIM))
    pe_big = pe_big.reshape(2, 1, CMP_STRIDE * LANES)
    w1r = w1.reshape(2, CMP_STRIDE, HEAD_DIM, CMP_HID)
    eye = jnp.eye(NSA_KV, dtype=w1.dtype)
    w1_big = jnp.einsum('cjdh,ge->cjgdeh', w1r, eye).reshape(2, CMP_STRIDE * LANES, NSA_KV * CMP_HID)
    w2_big = jnp.einsum('hd,ge->ghed', w2, eye).reshape(NSA_KV * CMP_HID, LANES)
    return pe_big, w1_big.astype(BF16), w2_big.astype(BF16)


def _pad_rows(a, n):
    return jnp.pad(a, ((0, 0), (0, n - a.shape[1]), (0, 0)))


def kernel(x_prompt, x_sample, cache_cmp_k, cache_cmp_v, cache_sel_k, cache_sel_v, cache_win_k, cache_win_v,
           state_hgrn, cache_mem_k, cache_mem_v, page_table, mem_prompt, norm_mix, w_in, cmp_pe_k, cmp_w1_k,
           cmp_w2_k, cmp_pe_v, cmp_w1_v, cmp_w2_v, nsa_out_norm, hg_lb_logits, hg_out_norm, w_out, norm_ca,
           norm_mem, ca_wq, ca_wk, ca_wv, ca_wo, norm_ffn, ffn_w_gate, ffn_w_up, ffn_w_down, final_norm):
    B, T = x_prompt.shape[:2]
    DB, TS = x_sample.shape[:2]
    n_pages = page_table.shape[1]
    past = n_pages * PAGE_SIZE
    n_pool = cache_cmp_k.shape[1]
    wb = cache_win_k.shape[2]
    ml = mem_prompt.shape[1]
    assert w_in.shape[0] == 1, "single layer"
    row2 = lambda a: a.reshape(1, -1)

    w_in_p = _prep_w_in(w_in[0])
    pek, w1k, w2k = _prep_compress(cmp_pe_k[0], cmp_w1_k[0], cmp_w2_k[0])
    pev, w1v, w2v = _prep_compress(cmp_pe_v[0], cmp_w1_v[0], cmp_w2_v[0])
    lb = jnp.cumsum(jax.nn.softmax(hg_lb_logits.astype(F32), axis=0), axis=0)[0].reshape(1, HG_W)
    gn_nsa = row2(_pair(nsa_out_norm[0], 0))
    wo_a = _pair(w_out[0][:NSA_W], 0).astype(BF16)
    wo_b = w_out[0][NSA_W:].astype(BF16)
    wq_ca = ca_wq[0].astype(BF16)
    wo_ca = ca_wo[0].astype(BF16)
    w_mem = jnp.concatenate([ca_wk[0], ca_wv[0]], axis=1).astype(BF16)
    wg, wu, wd = ffn_w_gate[0].astype(BF16), ffn_w_up[0].astype(BF16), ffn_w_down[0].astype(BF16)
    g_mix, g_ca, g_ffn, g_fin = row2(norm_mix[0]), row2(norm_ca[0]), row2(norm_ffn[0]), row2(final_norm)
    g_hg = row2(hg_out_norm[0])

    tabs_p = _rope_tables(jnp.arange(T))
    tabs_s = _rope_tables(past + (jnp.arange(DB * TS) % TS))
    blk = np.arange(T) // SEL_BLK
    lanes2 = np.arange(2 * LANES)
    oh_np = ((lanes2[None, :] >= HEAD_DIM) & (lanes2[None, :] < 3 * HEAD_DIM)
             & ((lanes2[None, :] - HEAD_DIM) % HEAD_DIM == blk[:, None]))
    oh_p = jnp.asarray(oh_np.astype(np.float32))
    oh_s = jnp.zeros((DB * TS, 2 * LANES), F32)

    def tail(x, o_nsa, o_hg, mk, mv, nb, tm):
        rows = x.shape[0]
        x1, qca = _mixout(x, o_nsa, o_hg, gn_nsa, wo_a, wo_b, g_ca, wq_ca, tm)
        per = rows // nb
        tq = min(per, 256)
        if per % 16:
            tq = -(-per // 16) * 16
            qca3 = _pad_rows(qca.reshape(nb, per, D_MODEL), tq)
        else:
            qca3 = qca.reshape(nb, per, D_MODEL)
        oca = _cross_attn(qca3, mk, mv, tq)[:, :per].reshape(rows, D_MODEL)
        x2 = _mm_res(oca, wo_ca, x1, tm)
        return _ffn(x2, g_ffn, wg, wu, wd, g_fin, min(rows, 512), wg.shape[1] // 2)

    M = B * T
    (q, qr, kcp, vcp, ksp, vsp, kwp, vwp, ksa, vsb, kwb, vwb, hm, gt) = _proj_in(
        x_prompt.reshape(M, D_MODEL), g_mix, w_in_p, tabs_p, oh_p, 256)
    pages_p = T // PAGE_SIZE
    ident = jnp.arange(B * pages_p, dtype=I32)
    as_pool = lambda a: a.reshape(-1, PAGE_SIZE // CMP_STRIDE, CMP_STRIDE * LANES)
    kc_p, vc_p = _compress(ident, as_pool(kcp), as_pool(vcp), pek, pev, w1k, w1v, w2k, w2v, B, pages_p)
    r3 = lambda a: a.reshape(B, T, -1)
    o_nsa = _nsa_prompt(r3(q), r3(qr), r3(gt), kc_p, vc_p, r3(ksa), r3(vsb), r3(kwb), r3(vwb), 128, 256)
    o_hg, s_p = _hgrn(r3(hm), lb, g_hg, jnp.zeros((B, HG_HEADS, HG_DK, HG_DV), F32), 128, T)
    mkv = _norm_mm(mem_prompt.reshape(B * ml, D_MODEL), row2(norm_mem[0]), w_mem, 256)
    mk_p = mkv[:, :D_MODEL].reshape(B, ml, D_MODEL)
    mv_p = mkv[:, D_MODEL:].reshape(B, ml, D_MODEL)
    y_p = tail(x_prompt.reshape(M, D_MODEL), o_nsa.reshape(M, NSA_W), o_hg.reshape(M, HG_W), mk_p, mv_p, B, 256)

    kv5 = lambda a, n: a.reshape(1, -1, n, NSA_KV, HEAD_DIM)
    wbp = min(WINDOW, T)
    outs_p = (kv5(kcp, T), kv5(vcp, T), kv5(ksp, T), kv5(vsp, T),
              kv5(kwp, T)[:, :, -wbp:], kv5(vwp, T)[:, :, -wbp:], s_p[None],
              mk_p.reshape(1, B, ml, CA_HEADS, CA_HD), mv_p.reshape(1, B, ml, CA_HEADS, CA_HD))

    MS = DB * TS
    (q, qr, kcs, vcs, kss, vss, kws, vws, _, _, _, _, hm, gt) = _proj_in(
        x_sample.reshape(MS, D_MODEL), g_mix, w_in_p, tabs_s, oh_s, MS)
    pt_flat = page_table.reshape(-1).astype(I32)
    pool_seg = lambda a: a[0].reshape(n_pool, PAGE_SIZE // CMP_STRIDE, CMP_STRIDE * LANES)
    kc_s, vc_s = _compress(pt_flat, pool_seg(cache_cmp_k), pool_seg(cache_cmp_v), pek, pev, w1k, w1v, w2k, w2v,
                           DB, n_pages)
    tp, tn = 8, 16
    s3 = lambda a, n: _pad_rows(a.reshape(DB, TS, -1), n)
    key_blk = np.arange(past) // SEL_BLK
    oh_keys = jnp.asarray((key_blk[:, None] == np.arange(LANES)[None, :]).astype(np.float32)).astype(BF16)
    nseg = past // CMP_STRIDE
    gmat = jnp.asarray((np.arange(nseg)[:, None] // (SEL_BLK // CMP_STRIDE)
                        == np.arange(LANES)[None, :]).astype(np.float32))
    pool_pg = lambda a: a[0].reshape(n_pool, PAGE_SIZE, LANES)
    o_nsa_s, nwk, nwv = _nsa_sample(
        pt_flat, s3(q, tp), s3(qr, tp), s3(gt, tp), kc_s, vc_s, s3(kss, tn), s3(vss, tn), s3(kws, tn), s3(vws, tn),
        cache_win_k[0].reshape(DB, wb, LANES), cache_win_v[0].reshape(DB, wb, LANES), oh_keys, gmat,
        pool_pg(cache_sel_k), pool_pg(cache_sel_v), n_pages, past, TS)
    o_hg_s, s_s = _hgrn(s3(hm, 8), lb, g_hg, state_hgrn[0].astype(F32), 8, TS)
    y_s = tail(x_sample.reshape(MS, D_MODEL), o_nsa_s[:, :TS].reshape(MS, NSA_W), o_hg_s[:, :TS].reshape(MS, HG_W),
               cache_mem_k[0].reshape(DB, ml, D_MODEL), cache_mem_v[0].reshape(DB, ml, D_MODEL), DB, MS)

    outs_s = (kv5(kcs, TS), kv5(vcs, TS), kv5(kss, TS), kv5(vss, TS),
              nwk.reshape(1, DB, wb, NSA_KV, HEAD_DIM), nwv.reshape(1, DB, wb, NSA_KV, HEAD_DIM), s_s[None])
    return (y_p.reshape(B, T, D_MODEL), y_s.reshape(DB, TS, D_MODEL)) + outs_p + outs_s
```

```python
import functools

import numpy as np
import jax
import jax.numpy as jnp
from jax import lax
from jax.experimental import pallas as pl
from jax.experimental.pallas import tpu as pltpu

F32 = jnp.float32
BF16 = jnp.bfloat16
I32 = jnp.int32

D_MODEL = 1024
NSA_HEADS = 8
NSA_KV = 2
HEAD_DIM = 64
NSA_GROUP = NSA_HEADS // NSA_KV
NSA_W = NSA_HEADS * HEAD_DIM
CMP_STRIDE = 16
CMP_LEN = 32
CMP_HID = 128
SEL_BLK = 64
SEL_TOPK = 16
WINDOW = 512
ROPE_THETA = 500000.0
ROPE_DIM = HEAD_DIM // 4
HG_HEADS = 4
HG_DK = 128
HG_DV = 128
HG_W = HG_HEADS * HG_DV
CA_HEADS = 4
CA_HD = D_MODEL // CA_HEADS
PAGE_SIZE = 128
EPS = 1e-6
FORCE_SCORE = 1e4
NEG = -1e30
MASK_BIAS = -1e9
LANES = 128
VMEM_LIMIT = 56 * 1024 * 1024

C_Q = 0
C_KC, C_VC, C_KS, C_VS, C_KW, C_VW = 512, 640, 768, 896, 1024, 1152
C_AUG = 1280
C_HG = 1536
C_GT = 3584
N_PROJ = 3712


def _cparams(sem):
    return pltpu.CompilerParams(dimension_semantics=sem, vmem_limit_bytes=VMEM_LIMIT)


def _nt(a, b):
    return lax.dot_general(a, b, (((1,), (1,)), ((), ())), preferred_element_type=F32)


def _tn(a, b):
    return lax.dot_general(a, b, (((0,), (0,)), ((), ())), preferred_element_type=F32)


def _mm(a, b):
    return jnp.dot(a, b, preferred_element_type=F32)


def _rms(x, g):
    return x * lax.rsqrt(jnp.mean(x * x, axis=-1, keepdims=True) + EPS) * g


def _masked_softmax(s, mask, axis):
    s = jnp.where(mask, s, NEG)
    m = jnp.max(s, axis=axis, keepdims=True)
    e = jnp.where(mask, jnp.exp(s - m), 0.0)
    return e / jnp.maximum(jnp.sum(e, axis=axis, keepdims=True), 1e-20)


def _log2(n):
    l = int(n).bit_length() - 1
    assert (1 << l) == n, n
    return l


def _proj_in_kernel(x_ref, g_ref, w_ref, c_ref, sa_ref, sb_ref, oh_ref,
                    q_ref, qr_ref, kc_ref, vc_ref, ks_ref, vs_ref, kw_ref, vw_ref,
                    ksa_ref, vsb_ref, kwb_ref, vwb_ref, hm_ref, gt_ref):
    xn = _rms(x_ref[...], g_ref[...]).astype(BF16)
    c, sa, sb = c_ref[...], sa_ref[...], sb_ref[...]

    def mm(lo, hi):
        return _mm(xn, w_ref[:, lo:hi])

    def rope(y):
        n = y.shape[1] // LANES
        cc, aa, bb = (jnp.concatenate([t] * n, axis=1) if n > 1 else t for t in (c, sa, sb))
        w = y.shape[1]
        return y * cc + pltpu.roll(y, w - ROPE_DIM // 2, 1) * aa + pltpu.roll(y, ROPE_DIM // 2, 1) * bb

    yq = mm(C_Q, C_Q + NSA_W) * (HEAD_DIM ** -0.5)
    q_ref[...] = yq.astype(BF16)
    qr_ref[...] = rope(yq).astype(BF16)
    kc_ref[...] = mm(C_KC, C_KC + LANES)
    vc_ref[...] = mm(C_VC, C_VC + LANES)
    ks_ref[...] = rope(mm(C_KS, C_KS + LANES))
    vs = mm(C_VS, C_VS + LANES)
    vs_ref[...] = vs
    vsb_ref[...] = vs.astype(BF16)
    kw = rope(mm(C_KW, C_KW + LANES))
    kw_ref[...] = kw
    kwb_ref[...] = kw.astype(BF16)
    vw = mm(C_VW, C_VW + LANES)
    vw_ref[...] = vw
    vwb_ref[...] = vw.astype(BF16)
    ksa_ref[...] = (rope(mm(C_AUG, C_AUG + 2 * LANES)) + oh_ref[...]).astype(BF16)
    hm_ref[...] = mm(C_HG, C_HG + 4 * HG_W)
    gt_ref[...] = mm(C_GT, C_GT + LANES)


def _proj_in(x2d, g, w, tabs, oh, tm):
    m = x2d.shape[0]
    nt_tab = tabs[0].shape[0] // tm
    row = lambda i: (i, 0)
    tab = lambda i: (i % nt_tab, 0)
    const = lambda i: (0, 0)
    widths = [(NSA_W, BF16), (NSA_W, BF16)] + [(LANES, F32)] * 6 + [(2 * LANES, BF16)] + [(LANES, BF16)] * 3 \
        + [(4 * HG_W, F32), (LANES, F32)]
    return pl.pallas_call(
        _proj_in_kernel,
        grid=(m // tm,),
        in_specs=[pl.BlockSpec((tm, D_MODEL), row), pl.BlockSpec((1, D_MODEL), const),
                  pl.BlockSpec((D_MODEL, N_PROJ), const),
                  pl.BlockSpec((tm, LANES), tab), pl.BlockSpec((tm, LANES), tab), pl.BlockSpec((tm, LANES), tab),
                  pl.BlockSpec((tm, 2 * LANES), tab)],
        out_specs=[pl.BlockSpec((tm, wd), row) for wd, _ in widths],
        out_shape=[jax.ShapeDtypeStruct((m, wd), dt) for wd, dt in widths],
        compiler_params=_cparams(("parallel",)),
        name="proj_in",
    )(x2d, g, w, *tabs, oh)


def _page_copy(pool_hbm, page, buf, slot, p, sem):
    cols = pl.ds(pl.multiple_of(p * PAGE_SIZE, PAGE_SIZE), PAGE_SIZE)
    return pltpu.make_async_copy(pool_hbm.at[page], buf.at[slot, :, cols], sem)


def _gather_start(pt_ref, pools, bufs, sems, b, slot, n_pages):
    def body(p, carry):
        page = pt_ref[b * n_pages + p]
        for pool, buf, sem in zip(pools, bufs, sems):
            _page_copy(pool, page, buf, slot, p, sem.at[slot]).start()
        return carry
    lax.fori_loop(0, n_pages, body, 0)


def _gather_wait(pools, bufs, sems, slot, n_pages):
    def body(p, carry):
        for pool, buf, sem in zip(pools, bufs, sems):
            _page_copy(pool, 0, buf, slot, p, sem.at[slot]).wait()
        return carry
    lax.fori_loop(0, n_pages, body, 0)


def _gather_step(pt_ref, pools, bufs, sems, n_pages):
    b = pl.program_id(0)
    nb = pl.num_programs(0)
    slot = b % 2

    @pl.when(b == 0)
    def _():
        _gather_start(pt_ref, pools, bufs, sems, 0, 0, n_pages)

    @pl.when(b + 1 < nb)
    def _():
        _gather_start(pt_ref, pools, bufs, sems, b + 1, 1 - slot, n_pages)

    _gather_wait(pools, bufs, sems, slot, n_pages)
    return slot


def _compress_core(tok, pe_ref, w1_ref, w2_ref, bias_scr, first, out, nseg):
    npair = CMP_STRIDE // 2
    nh = NSA_KV * CMP_HID

    @pl.when(first)
    def _():
        acc = _mm(pe_ref[:, 0:2 * LANES].astype(BF16), w1_ref[0])
        for jj in range(1, npair):
            acc = acc + _mm(pe_ref[:, jj * 2 * LANES:(jj + 1) * 2 * LANES].astype(BF16), w1_ref[jj])
        bias_scr[...] = acc

    acc = None
    for jj in range(npair):
        xp = jnp.concatenate([tok[pl.ds(2 * jj, nseg, stride=CMP_STRIDE), :],
                              tok[pl.ds(2 * jj + 1, nseg, stride=CMP_STRIDE), :]], axis=1).astype(BF16)
        part = _mm(xp, w1_ref[jj])
        acc = part if acc is None else acc + part
    ha = acc[:, 0:nh] + bias_scr[0:1, 0:nh]
    hb = acc[:, nh:2 * nh] + bias_scr[1:2, nh:2 * nh]
    h = jax.nn.gelu(ha + pltpu.roll(hb, nseg - 1, 0))
    row = lax.broadcasted_iota(I32, (nseg, 1), 0)
    h = jnp.where(row < nseg - 1, h, 0.0)
    out[...] = _mm(h.astype(BF16), w2_ref[...]).astype(BF16)


def _compress_rows_kernel(k_ref, v_ref, pek_ref, pev_ref, w1k_ref, w1v_ref, w2k_ref, w2v_ref,
                          kc_ref, vc_ref, bias_scr, *, nseg):
    first = pl.program_id(0) == 0
    for i, (tok, pe, w1, w2, out) in enumerate(((k_ref, pek_ref, w1k_ref, w2k_ref, kc_ref),
                                                (v_ref, pev_ref, w1v_ref, w2v_ref, vc_ref))):
        _compress_core(tok.at[0], pe, w1, w2, bias_scr.at[i], first, out.at[0], nseg)


def _compress_paged_kernel(pt_ref, poolk_hbm, poolv_hbm, pek_ref, pev_ref, w1k_ref, w1v_ref, w2k_ref, w2v_ref,
                           kc_ref, vc_ref, xk_buf, xv_buf, tok_scr, bias_scr, semk, semv, *, n_pages):
    slot = _gather_step(pt_ref, (poolk_hbm, poolv_hbm), (xk_buf, xv_buf), (semk, semv), n_pages)
    first = pl.program_id(0) == 0
    nseg = n_pages * (PAGE_SIZE // CMP_STRIDE)
    for i, (xbuf, pe, w1, w2, out) in enumerate(((xk_buf, pek_ref, w1k_ref, w2k_ref, kc_ref),
                                                 (xv_buf, pev_ref, w1v_ref, w2v_ref, vc_ref))):
        tok_scr[...] = xbuf[slot].T
        _compress_core(tok_scr, pe, w1, w2, bias_scr.at[i], first, out.at[0], nseg)


def _compress_weight_specs(const2, const3):
    npair = CMP_STRIDE // 2
    return [pl.BlockSpec((8, CMP_STRIDE * LANES), const2), pl.BlockSpec((8, CMP_STRIDE * LANES), const2),
            pl.BlockSpec((npair, 2 * LANES, 4 * CMP_HID), const3), pl.BlockSpec((npair, 2 * LANES, 4 * CMP_HID), const3),
            pl.BlockSpec((NSA_KV * CMP_HID, LANES), const2), pl.BlockSpec((NSA_KV * CMP_HID, LANES), const2)]


def _compress_rows(k_rows, v_rows, weights):
    nb, ntok = k_rows.shape[:2]
    nseg = ntok // CMP_STRIDE
    per_b = lambda b: (b, 0, 0)
    return pl.pallas_call(
        functools.partial(_compress_rows_kernel, nseg=nseg),
        grid=(nb,),
        in_specs=[pl.BlockSpec((1, ntok, LANES), per_b), pl.BlockSpec((1, ntok, LANES), per_b)]
        + _compress_weight_specs(lambda b: (0, 0), lambda b: (0, 0, 0)),
        out_specs=[pl.BlockSpec((1, nseg, LANES), per_b)] * 2,
        out_shape=[jax.ShapeDtypeStruct((nb, nseg, LANES), BF16)] * 2,
        scratch_shapes=[pltpu.VMEM((2, 8, 4 * CMP_HID), F32)],
        compiler_params=_cparams(("arbitrary",)),
        name="compress_rows",
    )(k_rows, v_rows, *weights)


def _compress_paged(page_table_flat, pool_k, pool_v, weights, n_batch, n_pages):
    past = n_pages * PAGE_SIZE
    nseg = past // CMP_STRIDE
    gs = pltpu.PrefetchScalarGridSpec(
        num_scalar_prefetch=1,
        grid=(n_batch,),
        in_specs=[pl.BlockSpec(memory_space=pl.ANY), pl.BlockSpec(memory_space=pl.ANY)]
        + _compress_weight_specs(lambda b, pt: (0, 0), lambda b, pt: (0, 0, 0)),
        out_specs=[pl.BlockSpec((1, nseg, LANES), lambda b, pt: (b, 0, 0))] * 2,
        scratch_shapes=[pltpu.VMEM((2, LANES, past), F32), pltpu.VMEM((2, LANES, past), F32),
                        pltpu.VMEM((past, LANES), F32), pltpu.VMEM((2, 8, 4 * CMP_HID), F32),
                        pltpu.SemaphoreType.DMA((2,)), pltpu.SemaphoreType.DMA((2,))],
    )
    return pl.pallas_call(
        functools.partial(_compress_paged_kernel, n_pages=n_pages),
        grid_spec=gs,
        out_shape=[jax.ShapeDtypeStruct((n_batch, nseg, LANES), BF16)] * 2,
        compiler_params=_cparams(("arbitrary",)),
        name="compress_paged",
    )(page_table_flat, pool_k, pool_v, *weights)


def _nsa_prompt_kernel(q_ref, qr_ref, gt_ref, kc_ref, vc_ref, ksa_ref, vs_ref, kw_ref, vw_ref, o_ref,
                       psum_scr, sc_scr, *, tq, tk, ncp, nsel, topk):
    i = pl.program_id(1)
    s0 = i * tq
    rows = NSA_GROUP * tq
    lane = lax.broadcasted_iota(I32, (tq, LANES), 1)
    upper = lane >= HEAD_DIM
    upper_k = lax.broadcasted_iota(I32, (tk, LANES), 1) >= HEAD_DIM
    gsig = jax.nn.sigmoid(gt_ref[0])
    kc = kc_ref[0]
    vc = vc_ref[0]
    t_rows = s0 + (lax.broadcasted_iota(I32, (rows, 1), 0) & (tq - 1))
    out_slots = [None] * NSA_GROUP
    for g in range(NSA_KV):
        mine = upper if g == 1 else jnp.logical_not(upper)

        def stack(ref, fill):
            return jnp.concatenate(
                [jnp.where(mine, ref[0, :, r * LANES:(r + 1) * LANES], fill) for r in range(NSA_GROUP)], axis=0)

        cend = lax.broadcasted_iota(I32, (rows, ncp), 1) * CMP_STRIDE + (CMP_LEN - 1)
        p_c = _masked_softmax(_nt(stack(q_ref, 0), kc), cend <= t_rows, 1)
        o_c = _mm(p_c.astype(BF16), vc)
        psum = p_c[0:tq]
        for r in range(1, NSA_GROUP):
            psum = psum + p_c[r * tq:(r + 1) * tq]
        psum_scr[...] = psum.T
        cps = SEL_BLK // CMP_STRIDE
        imp = psum_scr[pl.ds(0, nsel, stride=cps), :]
        for u in range(1, cps):
            imp = imp + psum_scr[pl.ds(u, nsel, stride=cps), :]
        j = lax.broadcasted_iota(I32, (nsel, tq), 0)
        cur = (s0 + lax.broadcasted_iota(I32, (nsel, tq), 1)) >> _log2(SEL_BLK)
        avail = j <= cur
        forced = (j == 0) | (j == cur) | (j == cur - 1)
        score = jnp.where(avail, jnp.where(forced, FORCE_SCORE, imp), -FORCE_SCORE)
        sc_scr[...] = score

        def rank_body(jp, cnt):
            other = sc_scr[pl.ds(jp, 1), :]
            beats = (other > score) | ((other == score) & (jp < j))
            return cnt + beats.astype(I32)

        cnt = lax.fori_loop(0, nsel, rank_body, jnp.zeros((nsel, tq), I32))
        bias_t = jnp.where((cnt < topk) & avail, 0.0, MASK_BIAS)
        if nsel < HEAD_DIM:
            bias_t = jnp.concatenate([bias_t, jnp.full((HEAD_DIM - nsel, tq), MASK_BIAS, F32)], axis=0)
        bias = jnp.concatenate([bias_t, bias_t], axis=0).T.astype(BF16)
        qs = stack(qr_ref, bias)
        glanes = slice(g * LANES, (g + 1) * LANES)
        mine_k = upper_k if g == 1 else jnp.logical_not(upper_k)

        def scores(k0):
            return _nt(qs, ksa_ref[0, pl.ds(k0, tk), glanes])

        def update(s, k0, m, acc):
            v_one = jnp.where(mine_k, vs_ref[0, pl.ds(k0, tk), :], 1.0)
            m_new = jnp.maximum(m, jnp.max(s, axis=1, keepdims=True))
            p = jnp.exp(s - m_new)
            return m_new, jnp.exp(m - m_new) * acc + _mm(p.astype(BF16), v_one)

        def sel_body(kk, carry):
            s_cur, m, acc = carry
            k0 = pl.multiple_of(kk * tk, tk)
            s_next = scores(k0 + tk)
            m, acc = update(s_cur, k0, m, acc)
            return s_next, m, acc

        n_full = s0 // tk
        carry = (scores(0), jnp.full((rows, 1), NEG, F32), jnp.zeros((rows, LANES), F32))
        s_last, m_s, acc_s = lax.fori_loop(0, n_full, sel_body, carry)
        k_last = pl.multiple_of(n_full * tk, tk)
        causal = k_last + lax.broadcasted_iota(I32, (rows, tk), 1) <= t_rows
        _, acc_s = update(jnp.where(causal, s_last, NEG), k_last, m_s, acc_s)
        o_s = acc_s / jnp.maximum(pltpu.roll(acc_s, HEAD_DIM, 1), 1e-20)
        wl = WINDOW + tq
        ws = pl.multiple_of(jnp.maximum(s0 - WINDOW, 0), tq)
        kpos = ws + lax.broadcasted_iota(I32, (rows, wl), 1)
        wmask = (kpos <= t_rows) & (kpos > t_rows - WINDOW)
        p_w = _masked_softmax(_nt(stack(qr_ref, 0), kw_ref[0, pl.ds(ws, wl), :]), wmask, 1)
        o_w = _mm(p_w.astype(BF16), vw_ref[0, pl.ds(ws, wl), :])
        for r in range(NSA_GROUP):
            h = g * NSA_GROUP + r
            rs = slice(r * tq, (r + 1) * tq)
            res = (gsig[:, 3 * h:3 * h + 1] * o_c[rs] + gsig[:, 3 * h + 1:3 * h + 2] * o_s[rs]
                   + gsig[:, 3 * h + 2:3 * h + 3] * o_w[rs])
            out_slots[r] = res if g == 0 else jnp.where(upper, res, out_slots[r])
    o_ref[0] = jnp.concatenate(out_slots, axis=1)


def _nsa_prompt(q, qr, gt, kc, vc, ksa, vsb, kwb, vwb, tq, tk):
    b, t = q.shape[:2]
    ncp = t // CMP_STRIDE
    nsel = t // SEL_BLK
    assert t % tk == 0 and tk % tq == 0 and t >= WINDOW + tq and nsel <= HEAD_DIM and WINDOW % tq == 0
    blk = lambda bb, i: (bb, i, 0)
    whole = lambda bb, i: (bb, 0, 0)
    return pl.pallas_call(
        functools.partial(_nsa_prompt_kernel, tq=tq, tk=tk, ncp=ncp, nsel=nsel, topk=min(SEL_TOPK, nsel)),
        grid=(b, t // tq),
        in_specs=[pl.BlockSpec((1, tq, NSA_W), blk), pl.BlockSpec((1, tq, NSA_W), blk),
                  pl.BlockSpec((1, tq, LANES), blk),
                  pl.BlockSpec((1, ncp, LANES), whole), pl.BlockSpec((1, ncp, LANES), whole),
                  pl.BlockSpec((1, t, 2 * LANES), whole), pl.BlockSpec((1, t, LANES), whole),
                  pl.BlockSpec((1, t, LANES), whole), pl.BlockSpec((1, t, LANES), whole)],
        out_specs=pl.BlockSpec((1, tq, NSA_W), blk),
        out_shape=jax.ShapeDtypeStruct((b, t, NSA_W), F32),
        scratch_shapes=[pltpu.VMEM((ncp, tq), F32), pltpu.VMEM((nsel, tq), F32)],
        compiler_params=_cparams(("parallel", "parallel")),
        name="nsa_prompt",
    )(q, qr, gt, kc, vc, ksa, vsb, kwb, vwb)


def _nsa_sample_kernel(pt_ref, q_ref, qr_ref, gt_ref, kc_ref, vc_ref, ksn_ref, vsn_ref, kwn_ref, vwn_ref,
                       wink_ref, winv_ref, oht_ref, gmat_ref, poolk_hbm, poolv_hbm,
                       o_ref, nwk_ref, nwv_ref, kbuf, vbuf, semk, semv,
                       *, n_pages, past, ts, tp, tn, wb, topk_past):
    slot = _gather_step(pt_ref, (poolk_hbm, poolv_hbm), (kbuf, vbuf), (semk, semv), n_pages)
    nseg = n_pages * (PAGE_SIZE // CMP_STRIDE)
    npb = past // SEL_BLK
    rows = NSA_HEADS * tp
    lane = lax.broadcasted_iota(I32, (tp, LANES), 1)
    upper = lane >= HEAD_DIM

    def stack(ref):
        parts = []
        for g in range(NSA_KV):
            mine = upper if g == 1 else jnp.logical_not(upper)
            parts += [jnp.where(mine, ref[0, :, r * LANES:(r + 1) * LANES], 0) for r in range(NSA_GROUP)]
        return jnp.concatenate(parts, axis=0)

    t_rows = lax.broadcasted_iota(I32, (rows, 1), 0) & (tp - 1)
    qc = stack(q_ref)
    cend = lax.broadcasted_iota(I32, (rows, nseg), 1) * CMP_STRIDE + (CMP_LEN - 1)
    p_c = _masked_softmax(_nt(qc, kc_ref[0]), cend <= past + t_rows, 1)
    o_c = _mm(p_c.astype(BF16), vc_ref[0])
    impn = []
    for g in range(NSA_KV):
        acc = p_c[(g * NSA_GROUP) * tp:(g * NSA_GROUP + 1) * tp]
        for r in range(1, NSA_GROUP):
            acc = acc + p_c[(g * NSA_GROUP + r) * tp:(g * NSA_GROUP + r + 1) * tp]
        impn.append(acc)
    impn = jnp.concatenate(impn, axis=0)
    imp = jnp.dot(impn, gmat_ref[...], precision=lax.Precision.HIGHEST, preferred_element_type=F32)
    j = lax.broadcasted_iota(I32, (NSA_KV * tp, LANES), 1)
    avail = j < npb
    forced = (j == 0) | (j == npb - 1)
    score = jnp.where(avail, jnp.where(forced, FORCE_SCORE, imp), -FORCE_SCORE)
    cnt = jnp.zeros((NSA_KV * tp, LANES), I32)
    for jp in range(npb):
        other = score[:, jp:jp + 1]
        beats = (other > score) | ((other == score) & (jp < j))
        cnt = cnt + beats.astype(I32)
    bias = jnp.where((cnt < topk_past) & avail, 0.0, MASK_BIAS).astype(BF16)
    bias = jnp.concatenate([bias[0:tp]] * NSA_GROUP + [bias[tp:2 * tp]] * NSA_GROUP, axis=0)
    qs = stack(qr_ref)
    key_t = lax.broadcasted_iota(I32, (rows, tn), 1)
    new_mask = (key_t <= t_rows) & (key_t < ts)
    s_p = _mm(qs, kbuf[slot].astype(BF16)) + _mm(bias, oht_ref[...])
    s_n = jnp.where(new_mask, _nt(qs, ksn_ref[0].astype(BF16)), NEG)
    m_s = jnp.maximum(jnp.max(s_p, axis=1, keepdims=True), jnp.max(s_n, axis=1, keepdims=True))
    e_p = jnp.exp(s_p - m_s)
    e_n = jnp.where(new_mask, jnp.exp(s_n - m_s), 0.0)
    l_s = jnp.sum(e_p, axis=1, keepdims=True) + jnp.sum(e_n, axis=1, keepdims=True)
    o_s = (_nt(e_p.astype(BF16), vbuf[slot].astype(BF16)) + _mm(e_n.astype(BF16), vsn_ref[0].astype(BF16))) \
        / jnp.maximum(l_s, 1e-20)
    wk = wink_ref[0]
    wv = winv_ref[0]
    iw = lax.broadcasted_iota(I32, (rows, wb), 1)
    wmask = (iw > t_rows + (wb - WINDOW)) & (iw >= wb - past)
    s_w = jnp.where(wmask, _nt(qs, wk.astype(BF16)), NEG)
    s_wn = jnp.where(new_mask, _nt(qs, kwn_ref[0].astype(BF16)), NEG)
    m_w = jnp.maximum(jnp.max(s_w, axis=1, keepdims=True), jnp.max(s_wn, axis=1, keepdims=True))
    e_w = jnp.where(wmask, jnp.exp(s_w - m_w), 0.0)
    e_wn = jnp.where(new_mask, jnp.exp(s_wn - m_w), 0.0)
    inv = 1.0 / jnp.maximum(jnp.sum(e_w, axis=1, keepdims=True) + jnp.sum(e_wn, axis=1, keepdims=True), 1e-20)
    o_w = _mm((e_w * inv).astype(BF16), wv.astype(BF16)) + _mm((e_wn * inv).astype(BF16), vwn_ref[0].astype(BF16))
    gsig = jax.nn.sigmoid(gt_ref[0])
    slots = []
    for r in range(NSA_GROUP):
        per_g = []
        for g in range(NSA_KV):
            h = g * NSA_GROUP + r
            rs = slice(h * tp, (h + 1) * tp)
            per_g.append(gsig[:, 3 * h:3 * h + 1] * o_c[rs] + gsig[:, 3 * h + 1:3 * h + 2] * o_s[rs]
                         + gsig[:, 3 * h + 2:3 * h + 3] * o_w[rs])
        slots.append(jnp.where(upper, per_g[1], per_g[0]))
    o_ref[0] = jnp.concatenate(slots, axis=1)
    sub = lax.broadcasted_iota(I32, (8, LANES), 0)
    for win, new_ref, out in ((wk, kwn_ref, nwk_ref), (wv, vwn_ref, nwv_ref)):
        rolled = pltpu.roll(win, wb - ts, 0)
        shifted = pltpu.roll(new_ref[0, 0:8, :], 8 - ts, 0)
        out[0, 0:wb - 8, :] = rolled[0:wb - 8]
        out[0, wb - 8:wb, :] = jnp.where(sub >= 8 - ts, shifted, rolled[wb - 8:wb])


def _nsa_sample(page_table_flat, q, qr, gt, kc, vc, ksn, vsn, kwn, vwn, win_k, win_v, oh, gmat, pool_k, pool_v,
                n_pages, past, ts):
    db, tp = q.shape[:2]
    tn = ksn.shape[1]
    wb = win_k.shape[1]
    nseg = n_pages * (PAGE_SIZE // CMP_STRIDE)
    npb = past // SEL_BLK
    assert ts <= 8 and npb <= LANES and wb >= 8 and past % PAGE_SIZE == 0
    per_b = lambda b, pt: (b, 0, 0)
    const2 = lambda b, pt: (0, 0)
    gs = pltpu.PrefetchScalarGridSpec(
        num_scalar_prefetch=1,
        grid=(db,),
        in_specs=[pl.BlockSpec((1, tp, NSA_W), per_b), pl.BlockSpec((1, tp, NSA_W), per_b),
                  pl.BlockSpec((1, tp, LANES), per_b),
                  pl.BlockSpec((1, nseg, LANES), per_b), pl.BlockSpec((1, nseg, LANES), per_b),
                  pl.BlockSpec((1, tn, LANES), per_b), pl.BlockSpec((1, tn, LANES), per_b),
                  pl.BlockSpec((1, tn, LANES), per_b), pl.BlockSpec((1, tn, LANES), per_b),
                  pl.BlockSpec((1, wb, LANES), per_b), pl.BlockSpec((1, wb, LANES), per_b),
                  pl.BlockSpec((LANES, past), const2), pl.BlockSpec((nseg, LANES), const2),
                  pl.BlockSpec(memory_space=pl.ANY), pl.BlockSpec(memory_space=pl.ANY)],
        out_specs=[pl.BlockSpec((1, tp, NSA_W), per_b), pl.BlockSpec((1, wb, LANES), per_b),
                   pl.BlockSpec((1, wb, LANES), per_b)],
        scratch_shapes=[pltpu.VMEM((2, LANES, past), F32), pltpu.VMEM((2, LANES, past), F32),
                        pltpu.SemaphoreType.DMA((2,)), pltpu.SemaphoreType.DMA((2,))],
    )
    return pl.pallas_call(
        functools.partial(_nsa_sample_kernel, n_pages=n_pages, past=past, ts=ts, tp=tp, tn=tn, wb=wb,
                          topk_past=min(SEL_TOPK, npb + 1) - 1),
        grid_spec=gs,
        out_shape=[jax.ShapeDtypeStruct((db, tp, NSA_W), F32), jax.ShapeDtypeStruct((db, wb, LANES), F32),
                   jax.ShapeDtypeStruct((db, wb, LANES), F32)],
        compiler_params=_cparams(("arbitrary",)),
        name="nsa_sample",
    )(page_table_flat, q, qr, gt, kc, vc, ksn, vsn, kwn, vwn, win_k, win_v, oh, gmat, pool_k, pool_v)


def _hgrn_kernel(hm_ref, lb_ref, og_ref, s0_ref, o_ref, sout_ref, st_scr, *, chunk, t_real, levels):
    c = pl.program_id(1)
    nc = pl.num_programs(1)

    @pl.when(c == 0)
    def _():
        for hd in range(HG_HEADS):
            st_scr[hd] = s0_ref[0, hd].T

    row = lax.broadcasted_iota(I32, (chunk, 1), 0)
    valid = (c * chunk + row) < t_real
    row_c = lax.broadcasted_iota(I32, (chunk, chunk), 0)
    col_c = lax.broadcasted_iota(I32, (chunk, chunk), 1)
    outs = []
    for hd in range(HG_HEADS):
        sl = slice(hd * HG_DK, (hd + 1) * HG_DK)
        q = hm_ref[0, :, hd * HG_DK:(hd + 1) * HG_DK]
        z = hm_ref[0, :, HG_W + hd * HG_DK:HG_W + (hd + 1) * HG_DK]
        v = hm_ref[0, :, 2 * HG_W + hd * HG_DV:2 * HG_W + (hd + 1) * HG_DV]
        gate = hm_ref[0, :, 3 * HG_W + hd * HG_DV:3 * HG_W + (hd + 1) * HG_DV]
        lb = lb_ref[:, sl]
        logf = jnp.where(valid, jnp.log(lb + (1.0 - lb) * jax.nn.sigmoid(z)), 0.0)
        k = jnp.where(valid, (1.0 - lb) * jax.nn.sigmoid(-z), 0.0)
        b = logf
        step = 1
        while step < chunk:
            b = b + jnp.where(row >= step, pltpu.roll(b, step, 0), 0.0)
            step *= 2
        st = st_scr[hd]
        o = _nt((q * jnp.exp(b)).astype(BF16), st.astype(BF16))
        if levels:
            a_mat = jnp.zeros((chunk, chunk), F32)
            for h in levels:
                m = jnp.concatenate(
                    [jnp.broadcast_to(b[blk * 2 * h + h - 1:blk * 2 * h + h, :], (2 * h, HG_DK))
                     for blk in range(chunk // (2 * h))], axis=0)
                second = ((row >> _log2(h)) & 1) == 1
                qs = jnp.where(second, q * jnp.exp(jnp.minimum(b - m, 0.0)), 0.0)
                ks = jnp.where(second, 0.0, k * jnp.exp(jnp.minimum(m - b, 0.0)))
                same = (row_c >> _log2(2 * h)) == (col_c >> _log2(2 * h))
                a_mat = a_mat + jnp.where(same, _nt(qs.astype(BF16), ks.astype(BF16)), 0.0)
            o = o + _mm(a_mat.astype(BF16), v.astype(BF16))
        for d in range(min(8, chunk)):
            kd, bd, vd = (x if d == 0 else pltpu.roll(x, d, 0) for x in (k, b, v))
            p = jnp.where((row & 7) >= d, q * kd * jnp.exp(jnp.minimum(b - bd, 0.0)), 0.0)
            o = o + jnp.sum(p, axis=1, keepdims=True) * vd
        bl = b[chunk - 1:chunk, :]
        st_scr[hd] = jnp.exp(bl) * st + _tn(v.astype(BF16), (k * jnp.exp(bl - b)).astype(BF16))
        outs.append(_rms(o, og_ref[:, sl]) * (gate * jax.nn.sigmoid(gate)))
    o_ref[0] = jnp.concatenate(outs, axis=1)

    @pl.when(c == nc - 1)
    def _():
        for hd in range(HG_HEADS):
            sout_ref[0, hd] = st_scr[hd].T


def _hgrn(hm, lb, og, s0, chunk, t_real):
    b, tpad = hm.shape[:2]
    levels = tuple(h for h in (64, 32, 16, 8) if 2 * h <= chunk)
    return pl.pallas_call(
        functools.partial(_hgrn_kernel, chunk=chunk, t_real=t_real, levels=levels),
        grid=(b, tpad // chunk),
        in_specs=[pl.BlockSpec((1, chunk, 4 * HG_W), lambda bb, c: (bb, c, 0)),
                  pl.BlockSpec((1, HG_W), lambda bb, c: (0, 0)), pl.BlockSpec((1, HG_W), lambda bb, c: (0, 0)),
                  pl.BlockSpec((1, HG_HEADS, HG_DK, HG_DV), lambda bb, c: (bb, 0, 0, 0))],
        out_specs=[pl.BlockSpec((1, chunk, HG_W), lambda bb, c: (bb, c, 0)),
                   pl.BlockSpec((1, HG_HEADS, HG_DK, HG_DV), lambda bb, c: (bb, 0, 0, 0))],
        out_shape=[jax.ShapeDtypeStruct((b, tpad, HG_W), F32),
                   jax.ShapeDtypeStruct((b, HG_HEADS, HG_DK, HG_DV), F32)],
        scratch_shapes=[pltpu.VMEM((HG_HEADS, HG_DV, HG_DK), F32)],
        compiler_params=_cparams(("parallel", "arbitrary")),
        name="hgrn",
    )(hm, lb, og, s0)


def _mixout_kernel(x_ref, on_ref, oh_ref, gn_ref, woa_ref, wob_ref, gca_ref, wq_ref, x1_ref, q_ref):
    a = _rms(on_ref[...], gn_ref[...]).astype(BF16)
    x1 = x_ref[...] + _mm(a, woa_ref[...]) + _mm(oh_ref[...].astype(BF16), wob_ref[...])
    x1_ref[...] = x1
    hn = _rms(x1, gca_ref[...]).astype(BF16)
    q_ref[...] = (_mm(hn, wq_ref[...]) * (CA_HD ** -0.5)).astype(BF16)


def _mixout(x, o_nsa, o_hg, gn, woa, wob, gca, wq, tm):
    m = x.shape[0]
    row = lambda i: (i, 0)
    const = lambda i: (0, 0)
    return pl.pallas_call(
        _mixout_kernel,
        grid=(m // tm,),
        in_specs=[pl.BlockSpec((tm, D_MODEL), row), pl.BlockSpec((tm, NSA_W), row), pl.BlockSpec((tm, HG_W), row),
                  pl.BlockSpec((1, NSA_W), const), pl.BlockSpec((NSA_W, D_MODEL), const),
                  pl.BlockSpec((HG_W, D_MODEL), const), pl.BlockSpec((1, D_MODEL), const),
                  pl.BlockSpec((D_MODEL, D_MODEL), const)],
        out_specs=[pl.BlockSpec((tm, D_MODEL), row), pl.BlockSpec((tm, D_MODEL), row)],
        out_shape=[jax.ShapeDtypeStruct((m, D_MODEL), F32), jax.ShapeDtypeStruct((m, D_MODEL), BF16)],
        compiler_params=_cparams(("parallel",)),
        name="mixout",
    )(x, o_nsa, o_hg, gn, woa, wob, gca, wq)


def _norm_mm_kernel(x_ref, g_ref, w_ref, o_ref):
    o_ref[...] = _mm(_rms(x_ref[...], g_ref[...]).astype(BF16), w_ref[...])


def _norm_mm(x, g, w, tm):
    m, n = x.shape[0], w.shape[1]
    return pl.pallas_call(
        _norm_mm_kernel,
        grid=(m // tm,),
        in_specs=[pl.BlockSpec((tm, D_MODEL), lambda i: (i, 0)), pl.BlockSpec((1, D_MODEL), lambda i: (0, 0)),
                  pl.BlockSpec((D_MODEL, n), lambda i: (0, 0))],
        out_specs=pl.BlockSpec((tm, n), lambda i: (i, 0)),
        out_shape=jax.ShapeDtypeStruct((m, n), F32),
        compiler_params=_cparams(("parallel",)),
        name="norm_mm",
    )(x, g, w)


def _ca_kernel(q_ref, mk_ref, mv_ref, o_ref):
    outs = []
    for hh in range(CA_HEADS):
        sl = slice(hh * CA_HD, (hh + 1) * CA_HD)
        s = _nt(q_ref[0, :, sl], mk_ref[0, :, sl].astype(BF16))
        e = jnp.exp(s - jnp.max(s, axis=1, keepdims=True))
        p = e / jnp.sum(e, axis=1, keepdims=True)
        outs.append(_mm(p.astype(BF16), mv_ref[0, :, sl].astype(BF16)))
    o_ref[0] = jnp.concatenate(outs, axis=1).astype(BF16)


def _cross_attn(q, mk, mv, tm):
    b, t = q.shape[:2]
    ml = mk.shape[1]
    return pl.pallas_call(
        _ca_kernel,
        grid=(b, t // tm),
        in_specs=[pl.BlockSpec((1, tm, D_MODEL), lambda bb, i: (bb, i, 0)),
                  pl.BlockSpec((1, ml, D_MODEL), lambda bb, i: (bb, 0, 0)),
                  pl.BlockSpec((1, ml, D_MODEL), lambda bb, i: (bb, 0, 0))],
        out_specs=pl.BlockSpec((1, tm, D_MODEL), lambda bb, i: (bb, i, 0)),
        out_shape=jax.ShapeDtypeStruct((b, t, D_MODEL), BF16),
        compiler_params=_cparams(("parallel", "parallel")),
        name="cross_attn",
    )(q, mk, mv)


def _mm_res_kernel(a_ref, w_ref, r_ref, o_ref):
    o_ref[...] = r_ref[...] + _mm(a_ref[...], w_ref[...])


def _mm_res(a, w, res, tm):
    m, k = a.shape
    n = w.shape[1]
    return pl.pallas_call(
        _mm_res_kernel,
        grid=(m // tm,),
        in_specs=[pl.BlockSpec((tm, k), lambda i: (i, 0)), pl.BlockSpec((k, n), lambda i: (0, 0)),
                  pl.BlockSpec((tm, n), lambda i: (i, 0))],
        out_specs=pl.BlockSpec((tm, n), lambda i: (i, 0)),
        out_shape=jax.ShapeDtypeStruct((m, n), F32),
        compiler_params=_cparams(("parallel",)),
        name="mm_res",
    )(a, w, res)


def _ffn_kernel(x_ref, gn_ref, wg_ref, wu_ref, wd_ref, gf_ref, y_ref, h_scr, acc_scr):
    j = pl.program_id(1)

    @pl.when(j == 0)
    def _():
        h_scr[...] = _rms(x_ref[...], gn_ref[...]).astype(BF16)
        acc_scr[...] = jnp.zeros_like(acc_scr)

    h = h_scr[...]
    gate = _mm(h, wg_ref[...])
    act = (gate * jax.nn.sigmoid(gate)) * _mm(h, wu_ref[...])
    acc_scr[...] += _mm(act.astype(BF16), wd_ref[...])

    @pl.when(j == pl.num_programs(1) - 1)
    def _():
        y_ref[...] = _rms(x_ref[...] + acc_scr[...], gf_ref[...])


def _ffn(x, gn, wg, wu, wd, gf, tm, th):
    m = x.shape[0]
    hid = wg.shape[1]
    return pl.pallas_call(
        _ffn_kernel,
        grid=(m // tm, hid // th),
        in_specs=[pl.BlockSpec((tm, D_MODEL), lambda i, j: (i, 0)), pl.BlockSpec((1, D_MODEL), lambda i, j: (0, 0)),
                  pl.BlockSpec((D_MODEL, th), lambda i, j: (0, j)), pl.BlockSpec((D_MODEL, th), lambda i, j: (0, j)),
                  pl.BlockSpec((th, D_MODEL), lambda i, j: (j, 0)), pl.BlockSpec((1, D_MODEL), lambda i, j: (0, 0))],
        out_specs=pl.BlockSpec((tm, D_MODEL), lambda i, j: (i, 0)),
        out_shape=jax.ShapeDtypeStruct((m, D_MODEL), F32),
        scratch_shapes=[pltpu.VMEM((tm, D_MODEL), BF16), pltpu.VMEM((tm, D_MODEL), F32)],
        compiler_params=_cparams(("parallel", "arbitrary")),
        name="ffn",
    )(x, gn, wg, wu, wd, gf)


def _pair(a, axis):
    shp = a.shape
    a = a.reshape(shp[:axis] + (NSA_KV, NSA_GROUP, HEAD_DIM) + shp[axis + 1:])
    a = jnp.swapaxes(a, axis, axis + 1)
    return a.reshape(shp)


def _rope_tables(pos):
    half = ROPE_DIM // 2
    inv = ROPE_THETA ** (-jnp.arange(half, dtype=F32) / half)
    ang = pos.astype(F32)[:, None] * inv[None, :]
    cos, sin = jnp.cos(ang), jnp.sin(ang)
    l = np.arange(LANES) % HEAD_DIM
    idx = l % half
    c = jnp.where(l < ROPE_DIM, cos[:, idx], 1.0)
    sa = jnp.where(l < half, -sin[:, idx], 0.0)
    sb = jnp.where((l >= half) & (l < ROPE_DIM), sin[:, idx], 0.0)
    return c, sa, sb


def _prep_w_in(w):
    offs = np.cumsum([0, NSA_W, 3 * NSA_HEADS] + [NSA_KV * HEAD_DIM] * 6 + [HG_W] * 4)
    seg = lambda i: w[:, offs[i]:offs[i + 1]]
    wq = _pair(seg(0), 1)
    wks = seg(4)
    z = jnp.zeros((D_MODEL, HEAD_DIM), w.dtype)
    aug = jnp.concatenate([wks[:, :HEAD_DIM], z, z, wks[:, HEAD_DIM:]], axis=1)
    gates = jnp.pad(seg(1), ((0, 0), (0, LANES - 3 * NSA_HEADS)))
    out = jnp.concatenate([wq] + [seg(i) for i in range(2, 8)] + [aug] + [seg(i) for i in range(8, 12)] + [gates],
                          axis=1)
    assert out.shape[1] == N_PROJ
    return out.astype(BF16)


def _prep_compress(pe, w1, w2):
    npair = CMP_STRIDE // 2
    pe_big = jnp.broadcast_to(pe.reshape(2, CMP_STRIDE, 1, HEAD_DIM), (2, CMP_STRIDE, NSA_KV, HEAD_DIM))
    pe_big = jnp.pad(pe_big.reshape(2, CMP_STRIDE * LANES), ((0, 6), (0, 0)))
    w1r = w1.reshape(2, npair, 2, HEAD_DIM, CMP_HID)
    eye = jnp.eye(NSA_KV, dtype=w1.dtype)
    w1_big = jnp.einsum('cjldh,ge->jlgdceh', w1r, eye).reshape(npair, 2 * LANES, 4 * CMP_HID)
    w2_big = jnp.einsum('hd,ge->ghed', w2, eye).reshape(NSA_KV * CMP_HID, LANES)
    return pe_big, w1_big.astype(BF16), w2_big.astype(BF16)


def _pad_rows(a, n):
    return jnp.pad(a, ((0, 0), (0, n - a.shape[1]), (0, 0)))


def kernel(x_prompt, x_sample, cache_cmp_k, cache_cmp_v, cache_sel_k, cache_sel_v, cache_win_k, cache_win_v,
           state_hgrn, cache_mem_k, cache_mem_v, page_table, mem_prompt, norm_mix, w_in, cmp_pe_k, cmp_w1_k,
           cmp_w2_k, cmp_pe_v, cmp_w1_v, cmp_w2_v, nsa_out_norm, hg_lb_logits, hg_out_norm, w_out, norm_ca,
           norm_mem, ca_wq, ca_wk, ca_wv, ca_wo, norm_ffn, ffn_w_gate, ffn_w_up, ffn_w_down, final_norm):
    B, T = x_prompt.shape[:2]
    DB, TS = x_sample.shape[:2]
    n_pages = page_table.shape[1]
    past = n_pages * PAGE_SIZE
    n_pool = cache_cmp_k.shape[1]
    wb = cache_win_k.shape[2]
    ml = mem_prompt.shape[1]
    assert w_in.shape[0] == 1, "single layer"
    row2 = lambda a: a.reshape(1, -1)

    w_in_p = _prep_w_in(w_in[0])
    pek, w1k, w2k = _prep_compress(cmp_pe_k[0], cmp_w1_k[0], cmp_w2_k[0])
    pev, w1v, w2v = _prep_compress(cmp_pe_v[0], cmp_w1_v[0], cmp_w2_v[0])
    lb = jnp.cumsum(jax.nn.softmax(hg_lb_logits.astype(F32), axis=0), axis=0)[0].reshape(1, HG_W)
    gn_nsa = row2(_pair(nsa_out_norm[0], 0))
    wo_a = _pair(w_out[0][:NSA_W], 0).astype(BF16)
    wo_b = w_out[0][NSA_W:].astype(BF16)
    wq_ca = ca_wq[0].astype(BF16)
    wo_ca = ca_wo[0].astype(BF16)
    w_mem = jnp.concatenate([ca_wk[0], ca_wv[0]], axis=1).astype(BF16)
    wg, wu, wd = ffn_w_gate[0].astype(BF16), ffn_w_up[0].astype(BF16), ffn_w_down[0].astype(BF16)
    g_mix, g_ca, g_ffn, g_fin = row2(norm_mix[0]), row2(norm_ca[0]), row2(norm_ffn[0]), row2(final_norm)
    g_hg = row2(hg_out_norm[0])

    tabs_p = _rope_tables(jnp.arange(T))
    tabs_s = _rope_tables(past + (jnp.arange(DB * TS) % TS))
    blk = np.arange(T) // SEL_BLK
    lanes2 = np.arange(2 * LANES)
    oh_np = ((lanes2[None, :] >= HEAD_DIM) & (lanes2[None, :] < 3 * HEAD_DIM)
             & ((lanes2[None, :] - HEAD_DIM) % HEAD_DIM == blk[:, None]))
    oh_p = jnp.asarray(oh_np.astype(np.float32))
    oh_s = jnp.zeros((DB * TS, 2 * LANES), F32)

    def tail(x, o_nsa, o_hg, mk, mv, nb, tm):
        rows = x.shape[0]
        x1, qca = _mixout(x, o_nsa, o_hg, gn_nsa, wo_a, wo_b, g_ca, wq_ca, tm)
        per = rows // nb
        tq = min(per, 256)
        if per % 16:
            tq = -(-per // 16) * 16
            qca3 = _pad_rows(qca.reshape(nb, per, D_MODEL), tq)
        else:
            qca3 = qca.reshape(nb, per, D_MODEL)
        oca = _cross_attn(qca3, mk, mv, tq)[:, :per].reshape(rows, D_MODEL)
        x2 = _mm_res(oca, wo_ca, x1, tm)
        return _ffn(x2, g_ffn, wg, wu, wd, g_fin, min(rows, 512), wg.shape[1] // 2)

    M = B * T
    (q, qr, kcp, vcp, ksp, vsp, kwp, vwp, ksa, vsb, kwb, vwb, hm, gt) = _proj_in(
        x_prompt.reshape(M, D_MODEL), g_mix, w_in_p, tabs_p, oh_p, 256)
    r3 = lambda a: a.reshape(B, T, -1)
    cmp_w = (pek, pev, w1k, w1v, w2k, w2v)
    kc_p, vc_p = _compress_rows(r3(kcp), r3(vcp), cmp_w)
    o_nsa = _nsa_prompt(r3(q), r3(qr), r3(gt), kc_p, vc_p, r3(ksa), r3(vsb), r3(kwb), r3(vwb), 128, 256)
    o_hg, s_p = _hgrn(r3(hm), lb, g_hg, jnp.zeros((B, HG_HEADS, HG_DK, HG_DV), F32), 128, T)
    mkv = _norm_mm(mem_prompt.reshape(B * ml, D_MODEL), row2(norm_mem[0]), w_mem, 256)
    mk_p = mkv[:, :D_MODEL].reshape(B, ml, D_MODEL)
    mv_p = mkv[:, D_MODEL:].reshape(B, ml, D_MODEL)
    y_p = tail(x_prompt.reshape(M, D_MODEL), o_nsa.reshape(M, NSA_W), o_hg.reshape(M, HG_W), mk_p, mv_p, B, 256)

    kv5 = lambda a, n: a.reshape(1, -1, n, NSA_KV, HEAD_DIM)
    wbp = min(WINDOW, T)
    outs_p = (kv5(kcp, T), kv5(vcp, T), kv5(ksp, T), kv5(vsp, T),
              kv5(kwp, T)[:, :, -wbp:], kv5(vwp, T)[:, :, -wbp:], s_p[None],
              mk_p.reshape(1, B, ml, CA_HEADS, CA_HD), mv_p.reshape(1, B, ml, CA_HEADS, CA_HD))

    MS = DB * TS
    (q, qr, kcs, vcs, kss, vss, kws, vws, _, _, _, _, hm, gt) = _proj_in(
        x_sample.reshape(MS, D_MODEL), g_mix, w_in_p, tabs_s, oh_s, MS)
    pt_flat = page_table.reshape(-1).astype(I32)
    pool_t = lambda a: jnp.transpose(a[0], (0, 2, 3, 1)).reshape(n_pool, LANES, PAGE_SIZE)
    kc_s, vc_s = _compress_paged(pt_flat, pool_t(cache_cmp_k), pool_t(cache_cmp_v), cmp_w, DB, n_pages)
    tp, tn = 8, 16
    s3 = lambda a, n: _pad_rows(a.reshape(DB, TS, -1), n)
    key_blk = np.arange(past) // SEL_BLK
    oh_keys = jnp.asarray((np.arange(LANES)[:, None] == key_blk[None, :]).astype(np.float32)).astype(BF16)
    nseg = past // CMP_STRIDE
    gmat = jnp.asarray((np.arange(nseg)[:, None] // (SEL_BLK // CMP_STRIDE)
                        == np.arange(LANES)[None, :]).astype(np.float32))
    o_nsa_s, nwk, nwv = _nsa_sample(
        pt_flat, s3(q, tp), s3(qr, tp), s3(gt, tp), kc_s, vc_s, s3(kss, tn), s3(vss, tn), s3(kws, tn), s3(vws, tn),
        cache_win_k[0].reshape(DB, wb, LANES), cache_win_v[0].reshape(DB, wb, LANES), oh_keys, gmat,
        pool_t(cache_sel_k), pool_t(cache_sel_v), n_pages, past, TS)
    o_hg_s, s_s = _hgrn(s3(hm, 8), lb, g_hg, state_hgrn[0].astype(F32), 8, TS)
    y_s = tail(x_sample.reshape(MS, D_MODEL), o_nsa_s[:, :TS].reshape(MS, NSA_W), o_hg_s[:, :TS].reshape(MS, HG_W),
               cache_mem_k[0].reshape(DB, ml, D_MODEL), cache_mem_v[0].reshape(DB, ml, D_MODEL), DB, MS)

    outs_s = (kv5(kcs, TS), kv5(vcs, TS), kv5(kss, TS), kv5(vss, TS),
              nwk.reshape(1, DB, wb, NSA_KV, HEAD_DIM), nwv.reshape(1, DB, wb, NSA_KV, HEAD_DIM), s_s[None])
    return (y_p.reshape(B, T, D_MODEL), y_s.reshape(DB, TS, D_MODEL)) + outs_p + outs_s
```

```python
import functools

import numpy as np
import jax
import jax.numpy as jnp
from jax import lax
from jax.experimental import pallas as pl
from jax.experimental.pallas import tpu as pltpu

F32 = jnp.float32
BF16 = jnp.bfloat16
I32 = jnp.int32

D_MODEL = 1024
NSA_HEADS = 8
NSA_KV = 2
HEAD_DIM = 64
NSA_GROUP = NSA_HEADS // NSA_KV
NSA_W = NSA_HEADS * HEAD_DIM
CMP_STRIDE = 16
CMP_LEN = 32
CMP_HID = 128
SEL_BLK = 64
SEL_TOPK = 16
WINDOW = 512
ROPE_THETA = 500000.0
ROPE_DIM = HEAD_DIM // 4
HG_HEADS = 4
HG_DK = 128
HG_DV = 128
HG_W = HG_HEADS * HG_DV
CA_HEADS = 4
CA_HD = D_MODEL // CA_HEADS
PAGE_SIZE = 128
EPS = 1e-6
FORCE_SCORE = 1e4
NEG = -1e30
MASK_BIAS = -1e9
LANES = 128
VMEM_LIMIT = 56 * 1024 * 1024

C_Q = 0
C_KC, C_VC, C_KS, C_VS, C_KW, C_VW = 512, 640, 768, 896, 1024, 1152
C_AUG = 1280
C_HG = 1536
C_GT = 3584
N_PROJ = 3712


def _cparams(sem):
    return pltpu.CompilerParams(dimension_semantics=sem, vmem_limit_bytes=VMEM_LIMIT)


def _nt(a, b):
    return lax.dot_general(a, b, (((1,), (1,)), ((), ())), preferred_element_type=F32)


def _tn(a, b):
    return lax.dot_general(a, b, (((0,), (0,)), ((), ())), preferred_element_type=F32)


def _mm(a, b):
    return jnp.dot(a, b, preferred_element_type=F32)


def _rms(x, g):
    return x * lax.rsqrt(jnp.mean(x * x, axis=-1, keepdims=True) + EPS) * g


def _masked_softmax(s, mask, axis):
    s = jnp.where(mask, s, NEG)
    m = jnp.max(s, axis=axis, keepdims=True)
    e = jnp.where(mask, jnp.exp(s - m), 0.0)
    return e / jnp.maximum(jnp.sum(e, axis=axis, keepdims=True), 1e-20)


def _log2(n):
    l = int(n).bit_length() - 1
    assert (1 << l) == n, n
    return l


def _proj_in_kernel(x_ref, g_ref, w_ref, c_ref, sa_ref, sb_ref, oh_ref,
                    q_ref, qr_ref, kc_ref, vc_ref, ks_ref, vs_ref, kw_ref, vw_ref,
                    ksa_ref, kwb_ref, vwb_ref, hm_ref, gt_ref,
                    kct_ref, vct_ref, kst_ref, vst_ref, kwt_ref, vwt_ref):
    xn = _rms(x_ref[...], g_ref[...]).astype(BF16)
    c, sa, sb = c_ref[...], sa_ref[...], sb_ref[...]

    def mm(lo, hi):
        return _mm(xn, w_ref[:, lo:hi])

    def rope(y):
        n = y.shape[1] // LANES
        cc, aa, bb = (jnp.concatenate([t] * n, axis=1) if n > 1 else t for t in (c, sa, sb))
        w = y.shape[1]
        return y * cc + pltpu.roll(y, w - ROPE_DIM // 2, 1) * aa + pltpu.roll(y, ROPE_DIM // 2, 1) * bb

    yq = mm(C_Q, C_Q + NSA_W) * (HEAD_DIM ** -0.5)
    q_ref[...] = yq.astype(BF16)
    qr_ref[...] = rope(yq).astype(BF16)
    kw = vw = None
    for col, roped, ref, tref in ((C_KC, False, kc_ref, kct_ref), (C_VC, False, vc_ref, vct_ref),
                                  (C_KS, True, ks_ref, kst_ref), (C_VS, False, vs_ref, vst_ref),
                                  (C_KW, True, kw_ref, kwt_ref), (C_VW, False, vw_ref, vwt_ref)):
        y = mm(col, col + LANES)
        y = rope(y) if roped else y
        ref[...] = y
        tref[0] = y.T
        kw = y if col == C_KW else kw
        vw = y if col == C_VW else vw
    kwb_ref[...] = kw.astype(BF16)
    vwb_ref[...] = vw.astype(BF16)
    ksa_ref[...] = (rope(mm(C_AUG, C_AUG + 2 * LANES)) + oh_ref[...]).astype(BF16)
    hm_ref[...] = mm(C_HG, C_HG + 4 * HG_W)
    gt_ref[...] = mm(C_GT, C_GT + LANES)


def _proj_in(x2d, g, w, tabs, oh, tm, nb):
    m = x2d.shape[0]
    per = m // nb
    nt_tab = tabs[0].shape[0] // tm
    nt_seq = per // tm
    row = lambda i: (i, 0)
    tab = lambda i: (i % nt_tab, 0)
    const = lambda i: (0, 0)
    widths = [(NSA_W, BF16), (NSA_W, BF16)] + [(LANES, F32)] * 6 + [(2 * LANES, BF16)] + [(LANES, BF16)] * 2 \
        + [(4 * HG_W, F32), (LANES, F32)]
    return pl.pallas_call(
        _proj_in_kernel,
        grid=(m // tm,),
        in_specs=[pl.BlockSpec((tm, D_MODEL), row), pl.BlockSpec((1, D_MODEL), const),
                  pl.BlockSpec((D_MODEL, N_PROJ), const),
                  pl.BlockSpec((tm, LANES), tab), pl.BlockSpec((tm, LANES), tab), pl.BlockSpec((tm, LANES), tab),
                  pl.BlockSpec((tm, 2 * LANES), tab)],
        out_specs=[pl.BlockSpec((tm, wd), row) for wd, _ in widths]
        + [pl.BlockSpec((1, LANES, tm), lambda i: (i // nt_seq, 0, i % nt_seq))] * 6,
        out_shape=[jax.ShapeDtypeStruct((m, wd), dt) for wd, dt in widths]
        + [jax.ShapeDtypeStruct((nb, LANES, per), F32)] * 6,
        compiler_params=_cparams(("parallel",)),
        name="proj_in",
    )(x2d, g, w, *tabs, oh)


def _page_copy(pool_hbm, page, buf, slot, p, sem):
    cols = pl.ds(pl.multiple_of(p * PAGE_SIZE, PAGE_SIZE), PAGE_SIZE)
    return pltpu.make_async_copy(pool_hbm.at[page], buf.at[slot, :, cols], sem)


def _gather_start(pt_ref, pools, bufs, sems, b, slot, n_pages):
    def body(p, carry):
        page = pt_ref[b * n_pages + p]
        for pool, buf, sem in zip(pools, bufs, sems):
            _page_copy(pool, page, buf, slot, p, sem.at[slot]).start()
        return carry
    lax.fori_loop(0, n_pages, body, 0)


def _gather_wait(pools, bufs, sems, slot, n_pages):
    def body(p, carry):
        for pool, buf, sem in zip(pools, bufs, sems):
            _page_copy(pool, 0, buf, slot, p, sem.at[slot]).wait()
        return carry
    lax.fori_loop(0, n_pages, body, 0)


def _gather_step(pt_ref, pools, bufs, sems, n_pages):
    b = pl.program_id(0)
    nb = pl.num_programs(0)
    slot = b % 2

    @pl.when(b == 0)
    def _():
        _gather_start(pt_ref, pools, bufs, sems, 0, 0, n_pages)

    @pl.when(b + 1 < nb)
    def _():
        _gather_start(pt_ref, pools, bufs, sems, b + 1, 1 - slot, n_pages)

    _gather_wait(pools, bufs, sems, slot, n_pages)
    return slot


def _compress_core(tok, pe_ref, w1_ref, w2_ref, bias_scr, first, out, nseg):
    npair = CMP_STRIDE // 2
    nh = NSA_KV * CMP_HID

    @pl.when(first)
    def _():
        acc = _mm(pe_ref[:, 0:2 * LANES].astype(BF16), w1_ref[0])
        for jj in range(1, npair):
            acc = acc + _mm(pe_ref[:, jj * 2 * LANES:(jj + 1) * 2 * LANES].astype(BF16), w1_ref[jj])
        bias_scr[...] = acc

    acc = None
    for jj in range(npair):
        xp = jnp.concatenate([tok[pl.ds(2 * jj, nseg, stride=CMP_STRIDE), :],
                              tok[pl.ds(2 * jj + 1, nseg, stride=CMP_STRIDE), :]], axis=1).astype(BF16)
        part = _mm(xp, w1_ref[jj])
        acc = part if acc is None else acc + part
    ha = acc[:, 0:nh] + bias_scr[0:1, 0:nh]
    hb = acc[:, nh:2 * nh] + bias_scr[1:2, nh:2 * nh]
    h = jax.nn.gelu(ha + pltpu.roll(hb, nseg - 1, 0))
    row = lax.broadcasted_iota(I32, (nseg, 1), 0)
    h = jnp.where(row < nseg - 1, h, 0.0)
    out[...] = _mm(h.astype(BF16), w2_ref[...]).astype(BF16)


def _compress_rows_kernel(k_ref, v_ref, pek_ref, pev_ref, w1k_ref, w1v_ref, w2k_ref, w2v_ref,
                          kc_ref, vc_ref, bias_scr, *, nseg):
    first = pl.program_id(0) == 0
    for i, (tok, pe, w1, w2, out) in enumerate(((k_ref, pek_ref, w1k_ref, w2k_ref, kc_ref),
                                                (v_ref, pev_ref, w1v_ref, w2v_ref, vc_ref))):
        _compress_core(tok.at[0], pe, w1, w2, bias_scr.at[i], first, out.at[0], nseg)


def _compress_paged_kernel(pt_ref, poolk_hbm, poolv_hbm, pek_ref, pev_ref, w1k_ref, w1v_ref, w2k_ref, w2v_ref,
                           kc_ref, vc_ref, xk_buf, xv_buf, tok_scr, bias_scr, semk, semv, *, n_pages):
    slot = _gather_step(pt_ref, (poolk_hbm, poolv_hbm), (xk_buf, xv_buf), (semk, semv), n_pages)
    first = pl.program_id(0) == 0
    nseg = n_pages * (PAGE_SIZE // CMP_STRIDE)
    for i, (xbuf, pe, w1, w2, out) in enumerate(((xk_buf, pek_ref, w1k_ref, w2k_ref, kc_ref),
                                                 (xv_buf, pev_ref, w1v_ref, w2v_ref, vc_ref))):
        tok_scr[...] = xbuf[slot].T
        _compress_core(tok_scr, pe, w1, w2, bias_scr.at[i], first, out.at[0], nseg)


def _compress_weight_specs(const2, const3):
    npair = CMP_STRIDE // 2
    return [pl.BlockSpec((8, CMP_STRIDE * LANES), const2), pl.BlockSpec((8, CMP_STRIDE * LANES), const2),
            pl.BlockSpec((npair, 2 * LANES, 4 * CMP_HID), const3), pl.BlockSpec((npair, 2 * LANES, 4 * CMP_HID), const3),
            pl.BlockSpec((NSA_KV * CMP_HID, LANES), const2), pl.BlockSpec((NSA_KV * CMP_HID, LANES), const2)]


def _compress_rows(k_rows, v_rows, weights):
    nb, ntok = k_rows.shape[:2]
    nseg = ntok // CMP_STRIDE
    per_b = lambda b: (b, 0, 0)
    return pl.pallas_call(
        functools.partial(_compress_rows_kernel, nseg=nseg),
        grid=(nb,),
        in_specs=[pl.BlockSpec((1, ntok, LANES), per_b), pl.BlockSpec((1, ntok, LANES), per_b)]
        + _compress_weight_specs(lambda b: (0, 0), lambda b: (0, 0, 0)),
        out_specs=[pl.BlockSpec((1, nseg, LANES), per_b)] * 2,
        out_shape=[jax.ShapeDtypeStruct((nb, nseg, LANES), BF16)] * 2,
        scratch_shapes=[pltpu.VMEM((2, 8, 4 * CMP_HID), F32)],
        compiler_params=_cparams(("arbitrary",)),
        name="compress_rows",
    )(k_rows, v_rows, *weights)


def _compress_paged(page_table_flat, pool_k, pool_v, weights, n_batch, n_pages):
    past = n_pages * PAGE_SIZE
    nseg = past // CMP_STRIDE
    gs = pltpu.PrefetchScalarGridSpec(
        num_scalar_prefetch=1,
        grid=(n_batch,),
        in_specs=[pl.BlockSpec(memory_space=pl.ANY), pl.BlockSpec(memory_space=pl.ANY)]
        + _compress_weight_specs(lambda b, pt: (0, 0), lambda b, pt: (0, 0, 0)),
        out_specs=[pl.BlockSpec((1, nseg, LANES), lambda b, pt: (b, 0, 0))] * 2,
        scratch_shapes=[pltpu.VMEM((2, LANES, past), F32), pltpu.VMEM((2, LANES, past), F32),
                        pltpu.VMEM((past, LANES), F32), pltpu.VMEM((2, 8, 4 * CMP_HID), F32),
                        pltpu.SemaphoreType.DMA((2,)), pltpu.SemaphoreType.DMA((2,))],
    )
    return pl.pallas_call(
        functools.partial(_compress_paged_kernel, n_pages=n_pages),
        grid_spec=gs,
        out_shape=[jax.ShapeDtypeStruct((n_batch, nseg, LANES), BF16)] * 2,
        compiler_params=_cparams(("arbitrary",)),
        name="compress_paged",
    )(page_table_flat, pool_k, pool_v, *weights)


def _nsa_prompt_kernel(q_ref, qr_ref, gt_ref, kc_ref, vc_ref, ksa_ref, vst_ref, kw_ref, vw_ref, o_ref,
                       psum_scr, sc_scr, *, tq, tk, ncp, nsel, topk):
    i = pl.program_id(1)
    s0 = i * tq
    rows = NSA_GROUP * tq
    hk = tk // 2
    lane = lax.broadcasted_iota(I32, (tq, LANES), 1)
    upper = lane >= HEAD_DIM
    upper_v = lax.broadcasted_iota(I32, (LANES, hk), 0) >= HEAD_DIM
    gsig = jax.nn.sigmoid(gt_ref[0])
    kc = kc_ref[0]
    vc = vc_ref[0]
    t_rows = s0 + (lax.broadcasted_iota(I32, (rows, 1), 0) & (tq - 1))
    t_cols = s0 + (lax.broadcasted_iota(I32, (1, rows), 1) & (tq - 1))
    o_c, o_w, o_st = [], [], []
    for g in range(NSA_KV):
        mine = upper if g == 1 else jnp.logical_not(upper)

        def stack(ref, fill):
            return jnp.concatenate(
                [jnp.where(mine, ref[0, :, r * LANES:(r + 1) * LANES], fill) for r in range(NSA_GROUP)], axis=0)

        cend = lax.broadcasted_iota(I32, (rows, ncp), 1) * CMP_STRIDE + (CMP_LEN - 1)
        p_c = _masked_softmax(_nt(stack(q_ref, 0), kc), cend <= t_rows, 1)
        o_c.append(_mm(p_c.astype(BF16), vc))
        psum = p_c[0:tq]
        for r in range(1, NSA_GROUP):
            psum = psum + p_c[r * tq:(r + 1) * tq]
        psum_scr[...] = psum.T
        cps = SEL_BLK // CMP_STRIDE
        imp = psum_scr[pl.ds(0, nsel, stride=cps), :]
        for u in range(1, cps):
            imp = imp + psum_scr[pl.ds(u, nsel, stride=cps), :]
        j = lax.broadcasted_iota(I32, (nsel, tq), 0)
        cur = (s0 + lax.broadcasted_iota(I32, (nsel, tq), 1)) >> _log2(SEL_BLK)
        avail = j <= cur
        forced = (j == 0) | (j == cur) | (j == cur - 1)
        score = jnp.where(avail, jnp.where(forced, FORCE_SCORE, imp), -FORCE_SCORE)
        sc_scr[...] = score

        def rank_body(jp, cnt):
            other = sc_scr[pl.ds(jp, 1), :]
            beats = (other > score) | ((other == score) & (jp < j))
            return cnt + beats.astype(I32)

        cnt = lax.fori_loop(0, nsel, rank_body, jnp.zeros((nsel, tq), I32), unroll=8)
        bias_t = jnp.where((cnt < topk) & avail, 0.0, MASK_BIAS)
        if nsel < HEAD_DIM:
            bias_t = jnp.concatenate([bias_t, jnp.full((HEAD_DIM - nsel, tq), MASK_BIAS, F32)], axis=0)
        bias = jnp.concatenate([bias_t, bias_t], axis=0).T.astype(BF16)
        qs = stack(qr_ref, bias)
        glanes = slice(g * LANES, (g + 1) * LANES)
        mine_v = upper_v if g == 1 else jnp.logical_not(upper_v)

        def scores(k0):
            return _nt(ksa_ref[0, pl.ds(k0, tk), glanes], qs)

        def update(s, k0, stats):
            new = []
            for c, (m, acc) in enumerate(stats):
                sc = s[c * hk:(c + 1) * hk]
                v_one = jnp.where(mine_v, vst_ref[0, :, pl.ds(k0 + c * hk, hk)], 1.0).astype(BF16)
                m_new = jnp.maximum(m, jnp.max(sc, axis=0, keepdims=True))
                p = jnp.exp((sc - m_new).astype(BF16))
                new.append((m_new, jnp.exp(m - m_new) * acc + _mm(v_one, p)))
            return tuple(new)

        def sel_body(kk, carry):
            s_cur, stats = carry
            k0 = pl.multiple_of(kk * tk, tk)
            return scores(k0 + tk), update(s_cur, k0, stats)

        n_full = s0 // tk
        stats0 = tuple((jnp.full((1, rows), NEG, F32), jnp.zeros((LANES, rows), F32)) for _ in range(2))
        s_last, stats = lax.fori_loop(0, n_full, sel_body, (scores(0), stats0))
        k_last = pl.multiple_of(n_full * tk, tk)
        causal = k_last + lax.broadcasted_iota(I32, (tk, rows), 0) <= t_cols
        (m_a, acc_a), (m_b, acc_b) = update(jnp.where(causal, s_last, NEG), k_last, stats)
        m_s = jnp.maximum(m_a, m_b)
        acc_s = jnp.exp(m_a - m_s) * acc_a + jnp.exp(m_b - m_s) * acc_b
        num = acc_s[g * HEAD_DIM:(g + 1) * HEAD_DIM]
        den = acc_s[(1 - g) * HEAD_DIM:(2 - g) * HEAD_DIM]
        o_st.append(num / jnp.maximum(den, 1e-20))
        wl = WINDOW + tq
        ws = pl.multiple_of(jnp.maximum(s0 - WINDOW, 0), tq)
        kpos = ws + lax.broadcasted_iota(I32, (rows, wl), 1)
        wmask = (kpos <= t_rows) & (kpos > t_rows - WINDOW)
        p_w = _masked_softmax(_nt(stack(qr_ref, 0), kw_ref[0, pl.ds(ws, wl), :]), wmask, 1)
        o_w.append(_mm(p_w.astype(BF16), vw_ref[0, pl.ds(ws, wl), :]))
    out_slots = []
    for r in range(NSA_GROUP):
        rs = slice(r * tq, (r + 1) * tq)
        o_s = jnp.concatenate([o_st[0][:, rs], o_st[1][:, rs]], axis=0).T
        res = []
        for g in range(NSA_KV):
            h = g * NSA_GROUP + r
            res.append(gsig[:, 3 * h:3 * h + 1] * o_c[g][rs] + gsig[:, 3 * h + 1:3 * h + 2] * o_s
                       + gsig[:, 3 * h + 2:3 * h + 3] * o_w[g][rs])
        out_slots.append(jnp.where(upper, res[1], res[0]))
    o_ref[0] = jnp.concatenate(out_slots, axis=1)


def _nsa_prompt(q, qr, gt, kc, vc, ksa, vst, kwb, vwb, tq, tk):
    b, t = q.shape[:2]
    ncp = t // CMP_STRIDE
    nsel = t // SEL_BLK
    assert t % tk == 0 and tk % tq == 0 and t >= WINDOW + tq and nsel <= HEAD_DIM and WINDOW % tq == 0
    blk = lambda bb, i: (bb, i, 0)
    whole = lambda bb, i: (bb, 0, 0)
    return pl.pallas_call(
        functools.partial(_nsa_prompt_kernel, tq=tq, tk=tk, ncp=ncp, nsel=nsel, topk=min(SEL_TOPK, nsel)),
        grid=(b, t // tq),
        in_specs=[pl.BlockSpec((1, tq, NSA_W), blk), pl.BlockSpec((1, tq, NSA_W), blk),
                  pl.BlockSpec((1, tq, LANES), blk),
                  pl.BlockSpec((1, ncp, LANES), whole), pl.BlockSpec((1, ncp, LANES), whole),
                  pl.BlockSpec((1, t, 2 * LANES), whole), pl.BlockSpec((1, LANES, t), whole),
                  pl.BlockSpec((1, t, LANES), whole), pl.BlockSpec((1, t, LANES), whole)],
        out_specs=pl.BlockSpec((1, tq, NSA_W), blk),
        out_shape=jax.ShapeDtypeStruct((b, t, NSA_W), F32),
        scratch_shapes=[pltpu.VMEM((ncp, tq), F32), pltpu.VMEM((nsel, tq), F32)],
        compiler_params=_cparams(("parallel", "parallel")),
        name="nsa_prompt",
    )(q, qr, gt, kc, vc, ksa, vst, kwb, vwb)


def _nsa_sample_kernel(pt_ref, q_ref, qr_ref, gt_ref, kc_ref, vc_ref, ksn_ref, vsn_ref, kwn_ref, vwn_ref,
                       wink_ref, winv_ref, oht_ref, gmat_ref, poolk_hbm, poolv_hbm,
                       o_ref, nwk_ref, nwv_ref, kbuf, vbuf, semk, semv,
                       *, n_pages, past, ts, tp, tn, wb, topk_past):
    slot = _gather_step(pt_ref, (poolk_hbm, poolv_hbm), (kbuf, vbuf), (semk, semv), n_pages)
    nseg = n_pages * (PAGE_SIZE // CMP_STRIDE)
    npb = past // SEL_BLK
    rows = NSA_HEADS * tp
    lane = lax.broadcasted_iota(I32, (tp, LANES), 1)
    upper = lane >= HEAD_DIM

    def stack(ref):
        parts = []
        for g in range(NSA_KV):
            mine = upper if g == 1 else jnp.logical_not(upper)
            parts += [jnp.where(mine, ref[0, :, r * LANES:(r + 1) * LANES], 0) for r in range(NSA_GROUP)]
        return jnp.concatenate(parts, axis=0)

    t_rows = lax.broadcasted_iota(I32, (rows, 1), 0) & (tp - 1)
    qc = stack(q_ref)
    cend = lax.broadcasted_iota(I32, (rows, nseg), 1) * CMP_STRIDE + (CMP_LEN - 1)
    p_c = _masked_softmax(_nt(qc, kc_ref[0]), cend <= past + t_rows, 1)
    o_c = _mm(p_c.astype(BF16), vc_ref[0])
    impn = []
    for g in range(NSA_KV):
        acc = p_c[(g * NSA_GROUP) * tp:(g * NSA_GROUP + 1) * tp]
        for r in range(1, NSA_GROUP):
            acc = acc + p_c[(g * NSA_GROUP + r) * tp:(g * NSA_GROUP + r + 1) * tp]
        impn.append(acc)
    impn = jnp.concatenate(impn, axis=0)
    imp = jnp.dot(impn, gmat_ref[...], precision=lax.Precision.HIGHEST, preferred_element_type=F32)
    j = lax.broadcasted_iota(I32, (NSA_KV * tp, LANES), 1)
    avail = j < npb
    forced = (j == 0) | (j == npb - 1)
    score = jnp.where(avail, jnp.where(forced, FORCE_SCORE, imp), -FORCE_SCORE)
    cnt = jnp.zeros((NSA_KV * tp, LANES), I32)
    for jp in range(npb):
        other = score[:, jp:jp + 1]
        beats = (other > score) | ((other == score) & (jp < j))
        cnt = cnt + beats.astype(I32)
    bias = jnp.where((cnt < topk_past) & avail, 0.0, MASK_BIAS).astype(BF16)
    bias = jnp.concatenate([bias[0:tp]] * NSA_GROUP + [bias[tp:2 * tp]] * NSA_GROUP, axis=0)
    qs = stack(qr_ref)
    key_t = lax.broadcasted_iota(I32, (rows, tn), 1)
    new_mask = (key_t <= t_rows) & (key_t < ts)
    s_p = _mm(qs, kbuf[slot].astype(BF16)) + _mm(bias, oht_ref[...])
    s_n = jnp.where(new_mask, _nt(qs, ksn_ref[0].astype(BF16)), NEG)
    m_s = jnp.maximum(jnp.max(s_p, axis=1, keepdims=True), jnp.max(s_n, axis=1, keepdims=True))
    e_p = jnp.exp(s_p - m_s)
    e_n = jnp.where(new_mask, jnp.exp(s_n - m_s), 0.0)
    l_s = jnp.sum(e_p, axis=1, keepdims=True) + jnp.sum(e_n, axis=1, keepdims=True)
    o_s = (_nt(e_p.astype(BF16), vbuf[slot].astype(BF16)) + _mm(e_n.astype(BF16), vsn_ref[0].astype(BF16))) \
        / jnp.maximum(l_s, 1e-20)
    wk = wink_ref[0]
    wv = winv_ref[0]
    iw = lax.broadcasted_iota(I32, (rows, wb), 1)
    wmask = (iw > t_rows + (wb - WINDOW)) & (iw >= wb - past)
    s_w = jnp.where(wmask, _nt(qs, wk.astype(BF16)), NEG)
    s_wn = jnp.where(new_mask, _nt(qs, kwn_ref[0].astype(BF16)), NEG)
    m_w = jnp.maximum(jnp.max(s_w, axis=1, keepdims=True), jnp.max(s_wn, axis=1, keepdims=True))
    e_w = jnp.where(wmask, jnp.exp(s_w - m_w), 0.0)
    e_wn = jnp.where(new_mask, jnp.exp(s_wn - m_w), 0.0)
    inv = 1.0 / jnp.maximum(jnp.sum(e_w, axis=1, keepdims=True) + jnp.sum(e_wn, axis=1, keepdims=True), 1e-20)
    o_w = _mm((e_w * inv).astype(BF16), wv.astype(BF16)) + _mm((e_wn * inv).astype(BF16), vwn_ref[0].astype(BF16))
    gsig = jax.nn.sigmoid(gt_ref[0])
    slots = []
    for r in range(NSA_GROUP):
        per_g = []
        for g in range(NSA_KV):
            h = g * NSA_GROUP + r
            rs = slice(h * tp, (h + 1) * tp)
            per_g.append(gsig[:, 3 * h:3 * h + 1] * o_c[rs] + gsig[:, 3 * h + 1:3 * h + 2] * o_s[rs]
                         + gsig[:, 3 * h + 2:3 * h + 3] * o_w[rs])
        slots.append(jnp.where(upper, per_g[1], per_g[0]))
    o_ref[0] = jnp.concatenate(slots, axis=1)
    sub = lax.broadcasted_iota(I32, (8, LANES), 0)
    for win, new_ref, out in ((wk, kwn_ref, nwk_ref), (wv, vwn_ref, nwv_ref)):
        rolled = pltpu.roll(win, wb - ts, 0)
        shifted = pltpu.roll(new_ref[0, 0:8, :], 8 - ts, 0)
        out[0, 0:wb - 8, :] = rolled[0:wb - 8]
        out[0, wb - 8:wb, :] = jnp.where(sub >= 8 - ts, shifted, rolled[wb - 8:wb])


def _nsa_sample(page_table_flat, q, qr, gt, kc, vc, ksn, vsn, kwn, vwn, win_k, win_v, oh, gmat, pool_k, pool_v,
                n_pages, past, ts):
    db, tp = q.shape[:2]
    tn = ksn.shape[1]
    wb = win_k.shape[1]
    nseg = n_pages * (PAGE_SIZE // CMP_STRIDE)
    npb = past // SEL_BLK
    assert ts <= 8 and npb <= LANES and wb >= 8 and past % PAGE_SIZE == 0
    per_b = lambda b, pt: (b, 0, 0)
    const2 = lambda b, pt: (0, 0)
    gs = pltpu.PrefetchScalarGridSpec(
        num_scalar_prefetch=1,
        grid=(db,),
        in_specs=[pl.BlockSpec((1, tp, NSA_W), per_b), pl.BlockSpec((1, tp, NSA_W), per_b),
                  pl.BlockSpec((1, tp, LANES), per_b),
                  pl.BlockSpec((1, nseg, LANES), per_b), pl.BlockSpec((1, nseg, LANES), per_b),
                  pl.BlockSpec((1, tn, LANES), per_b), pl.BlockSpec((1, tn, LANES), per_b),
                  pl.BlockSpec((1, tn, LANES), per_b), pl.BlockSpec((1, tn, LANES), per_b),
                  pl.BlockSpec((1, wb, LANES), per_b), pl.BlockSpec((1, wb, LANES), per_b),
                  pl.BlockSpec((LANES, past), const2), pl.BlockSpec((nseg, LANES), const2),
                  pl.BlockSpec(memory_space=pl.ANY), pl.BlockSpec(memory_space=pl.ANY)],
        out_specs=[pl.BlockSpec((1, tp, NSA_W), per_b), pl.BlockSpec((1, wb, LANES), per_b),
                   pl.BlockSpec((1, wb, LANES), per_b)],
        scratch_shapes=[pltpu.VMEM((2, LANES, past), F32), pltpu.VMEM((2, LANES, past), F32),
                        pltpu.SemaphoreType.DMA((2,)), pltpu.SemaphoreType.DMA((2,))],
    )
    return pl.pallas_call(
        functools.partial(_nsa_sample_kernel, n_pages=n_pages, past=past, ts=ts, tp=tp, tn=tn, wb=wb,
                          topk_past=min(SEL_TOPK, npb + 1) - 1),
        grid_spec=gs,
        out_shape=[jax.ShapeDtypeStruct((db, tp, NSA_W), F32), jax.ShapeDtypeStruct((db, wb, LANES), F32),
                   jax.ShapeDtypeStruct((db, wb, LANES), F32)],
        compiler_params=_cparams(("arbitrary",)),
        name="nsa_sample",
    )(page_table_flat, q, qr, gt, kc, vc, ksn, vsn, kwn, vwn, win_k, win_v, oh, gmat, pool_k, pool_v)


def _hgrn_kernel(hm_ref, lb_ref, og_ref, s0_ref, o_ref, sout_ref, st_scr, *, chunk, t_real, levels):
    c = pl.program_id(1)
    nc = pl.num_programs(1)

    @pl.when(c == 0)
    def _():
        for hd in range(HG_HEADS):
            st_scr[hd] = s0_ref[0, hd].T

    row = lax.broadcasted_iota(I32, (chunk, 1), 0)
    valid = (c * chunk + row) < t_real
    row_c = lax.broadcasted_iota(I32, (chunk, chunk), 0)
    col_c = lax.broadcasted_iota(I32, (chunk, chunk), 1)
    outs = []
    for hd in range(HG_HEADS):
        sl = slice(hd * HG_DK, (hd + 1) * HG_DK)
        q = hm_ref[0, :, hd * HG_DK:(hd + 1) * HG_DK]
        z = hm_ref[0, :, HG_W + hd * HG_DK:HG_W + (hd + 1) * HG_DK]
        v = hm_ref[0, :, 2 * HG_W + hd * HG_DV:2 * HG_W + (hd + 1) * HG_DV]
        gate = hm_ref[0, :, 3 * HG_W + hd * HG_DV:3 * HG_W + (hd + 1) * HG_DV]
        lb = lb_ref[:, sl]
        logf = jnp.where(valid, jnp.log(lb + (1.0 - lb) * jax.nn.sigmoid(z)), 0.0)
        k = jnp.where(valid, (1.0 - lb) * jax.nn.sigmoid(-z), 0.0)
        b = logf
        step = 1
        while step < chunk:
            b = b + jnp.where(row >= step, pltpu.roll(b, step, 0), 0.0)
            step *= 2
        st = st_scr[hd]
        o = _nt((q * jnp.exp(b)).astype(BF16), st.astype(BF16))
        if levels:
            a_mat = jnp.zeros((chunk, chunk), F32)
            for h in levels:
                m = jnp.concatenate(
                    [jnp.broadcast_to(b[blk * 2 * h + h - 1:blk * 2 * h + h, :], (2 * h, HG_DK))
                     for blk in range(chunk // (2 * h))], axis=0)
                second = ((row >> _log2(h)) & 1) == 1
                qs = jnp.where(second, q * jnp.exp(jnp.minimum(b - m, 0.0)), 0.0)
                ks = jnp.where(second, 0.0, k * jnp.exp(jnp.minimum(m - b, 0.0)))
                same = (row_c >> _log2(2 * h)) == (col_c >> _log2(2 * h))
                a_mat = a_mat + jnp.where(same, _nt(qs.astype(BF16), ks.astype(BF16)), 0.0)
            o = o + _mm(a_mat.astype(BF16), v.astype(BF16))
        for d in range(min(8, chunk)):
            kd, bd, vd = (x if d == 0 else pltpu.roll(x, d, 0) for x in (k, b, v))
            p = jnp.where((row & 7) >= d, q * kd * jnp.exp(jnp.minimum(b - bd, 0.0)), 0.0)
            o = o + jnp.sum(p, axis=1, keepdims=True) * vd
        bl = b[chunk - 1:chunk, :]
        st_scr[hd] = jnp.exp(bl) * st + _tn(v.astype(BF16), (k * jnp.exp(bl - b)).astype(BF16))
        outs.append(_rms(o, og_ref[:, sl]) * (gate * jax.nn.sigmoid(gate)))
    o_ref[0] = jnp.concatenate(outs, axis=1)

    @pl.when(c == nc - 1)
    def _():
        for hd in range(HG_HEADS):
            sout_ref[0, hd] = st_scr[hd].T


def _hgrn(hm, lb, og, s0, chunk, t_real):
    b, tpad = hm.shape[:2]
    levels = tuple(h for h in (64, 32, 16, 8) if 2 * h <= chunk)
    return pl.pallas_call(
        functools.partial(_hgrn_kernel, chunk=chunk, t_real=t_real, levels=levels),
        grid=(b, tpad // chunk),
        in_specs=[pl.BlockSpec((1, chunk, 4 * HG_W), lambda bb, c: (bb, c, 0)),
                  pl.BlockSpec((1, HG_W), lambda bb, c: (0, 0)), pl.BlockSpec((1, HG_W), lambda bb, c: (0, 0)),
                  pl.BlockSpec((1, HG_HEADS, HG_DK, HG_DV), lambda bb, c: (bb, 0, 0, 0))],
        out_specs=[pl.BlockSpec((1, chunk, HG_W), lambda bb, c: (bb, c, 0)),
                   pl.BlockSpec((1, HG_HEADS, HG_DK, HG_DV), lambda bb, c: (bb, 0, 0, 0))],
        out_shape=[jax.ShapeDtypeStruct((b, tpad, HG_W), F32),
                   jax.ShapeDtypeStruct((b, HG_HEADS, HG_DK, HG_DV), F32)],
        scratch_shapes=[pltpu.VMEM((HG_HEADS, HG_DV, HG_DK), F32)],
        compiler_params=_cparams(("parallel", "arbitrary")),
        name="hgrn",
    )(hm, lb, og, s0)


def _mixout_kernel(x_ref, on_ref, oh_ref, gn_ref, woa_ref, wob_ref, gca_ref, wq_ref, x1_ref, q_ref):
    a = _rms(on_ref[...], gn_ref[...]).astype(BF16)
    x1 = x_ref[...] + _mm(a, woa_ref[...]) + _mm(oh_ref[...].astype(BF16), wob_ref[...])
    x1_ref[...] = x1
    hn = _rms(x1, gca_ref[...]).astype(BF16)
    q_ref[...] = (_mm(hn, wq_ref[...]) * (CA_HD ** -0.5)).astype(BF16)


def _mixout(x, o_nsa, o_hg, gn, woa, wob, gca, wq, tm):
    m = x.shape[0]
    row = lambda i: (i, 0)
    const = lambda i: (0, 0)
    return pl.pallas_call(
        _mixout_kernel,
        grid=(m // tm,),
        in_specs=[pl.BlockSpec((tm, D_MODEL), row), pl.BlockSpec((tm, NSA_W), row), pl.BlockSpec((tm, HG_W), row),
                  pl.BlockSpec((1, NSA_W), const), pl.BlockSpec((NSA_W, D_MODEL), const),
                  pl.BlockSpec((HG_W, D_MODEL), const), pl.BlockSpec((1, D_MODEL), const),
                  pl.BlockSpec((D_MODEL, D_MODEL), const)],
        out_specs=[pl.BlockSpec((tm, D_MODEL), row), pl.BlockSpec((tm, D_MODEL), row)],
        out_shape=[jax.ShapeDtypeStruct((m, D_MODEL), F32), jax.ShapeDtypeStruct((m, D_MODEL), BF16)],
        compiler_params=_cparams(("parallel",)),
        name="mixout",
    )(x, o_nsa, o_hg, gn, woa, wob, gca, wq)


def _norm_mm_kernel(x_ref, g_ref, w_ref, o_ref):
    o_ref[...] = _mm(_rms(x_ref[...], g_ref[...]).astype(BF16), w_ref[...])


def _norm_mm(x, g, w, tm):
    m, n = x.shape[0], w.shape[1]
    return pl.pallas_call(
        _norm_mm_kernel,
        grid=(m // tm,),
        in_specs=[pl.BlockSpec((tm, D_MODEL), lambda i: (i, 0)), pl.BlockSpec((1, D_MODEL), lambda i: (0, 0)),
                  pl.BlockSpec((D_MODEL, n), lambda i: (0, 0))],
        out_specs=pl.BlockSpec((tm, n), lambda i: (i, 0)),
        out_shape=jax.ShapeDtypeStruct((m, n), F32),
        compiler_params=_cparams(("parallel",)),
        name="norm_mm",
    )(x, g, w)


def _ca_kernel(q_ref, mk_ref, mv_ref, o_ref, *, ml, tiled):
    nchunk = CA_HD // LANES

    def head(ref, hh):
        if tiled:
            return jnp.concatenate([ref[0, pl.ds(c * CA_HEADS + hh, ml, stride=nchunk * CA_HEADS), :]
                                    for c in range(nchunk)], axis=1)
        return ref[0, :, hh * CA_HD:(hh + 1) * CA_HD]

    outs = []
    for hh in range(CA_HEADS):
        s = _nt(q_ref[0, :, hh * CA_HD:(hh + 1) * CA_HD], head(mk_ref, hh).astype(BF16))
        e = jnp.exp(s - jnp.max(s, axis=1, keepdims=True))
        p = e / jnp.sum(e, axis=1, keepdims=True)
        outs.append(_mm(p.astype(BF16), head(mv_ref, hh).astype(BF16)))
    o_ref[0] = jnp.concatenate(outs, axis=1).astype(BF16)


def _cross_attn(q, mk, mv, tm, ml, tiled):
    b, t = q.shape[:2]
    mem_block = mk.shape[1:]
    return pl.pallas_call(
        functools.partial(_ca_kernel, ml=ml, tiled=tiled),
        grid=(b, t // tm),
        in_specs=[pl.BlockSpec((1, tm, D_MODEL), lambda bb, i: (bb, i, 0)),
                  pl.BlockSpec((1,) + mem_block, lambda bb, i: (bb, 0, 0)),
                  pl.BlockSpec((1,) + mem_block, lambda bb, i: (bb, 0, 0))],
        out_specs=pl.BlockSpec((1, tm, D_MODEL), lambda bb, i: (bb, i, 0)),
        out_shape=jax.ShapeDtypeStruct((b, t, D_MODEL), BF16),
        compiler_params=_cparams(("parallel", "parallel")),
        name="cross_attn",
    )(q, mk, mv)


def _mm_res_kernel(a_ref, w_ref, r_ref, o_ref):
    o_ref[...] = r_ref[...] + _mm(a_ref[...], w_ref[...])


def _mm_res(a, w, res, tm):
    m, k = a.shape
    n = w.shape[1]
    return pl.pallas_call(
        _mm_res_kernel,
        grid=(m // tm,),
        in_specs=[pl.BlockSpec((tm, k), lambda i: (i, 0)), pl.BlockSpec((k, n), lambda i: (0, 0)),
                  pl.BlockSpec((tm, n), lambda i: (i, 0))],
        out_specs=pl.BlockSpec((tm, n), lambda i: (i, 0)),
        out_shape=jax.ShapeDtypeStruct((m, n), F32),
        compiler_params=_cparams(("parallel",)),
        name="mm_res",
    )(a, w, res)


def _ffn_kernel(x_ref, gn_ref, wg_ref, wu_ref, wd_ref, gf_ref, y_ref, h_scr, acc_scr):
    j = pl.program_id(1)

    @pl.when(j == 0)
    def _():
        h_scr[...] = _rms(x_ref[...], gn_ref[...]).astype(BF16)
        acc_scr[...] = jnp.zeros_like(acc_scr)

    h = h_scr[...]
    gate = _mm(h, wg_ref[...])
    act = (gate * jax.nn.sigmoid(gate)) * _mm(h, wu_ref[...])
    acc_scr[...] += _mm(act.astype(BF16), wd_ref[...])

    @pl.when(j == pl.num_programs(1) - 1)
    def _():
        y_ref[...] = _rms(x_ref[...] + acc_scr[...], gf_ref[...])


def _ffn(x, gn, wg, wu, wd, gf, tm, th):
    m = x.shape[0]
    hid = wg.shape[1]
    return pl.pallas_call(
        _ffn_kernel,
        grid=(m // tm, hid // th),
        in_specs=[pl.BlockSpec((tm, D_MODEL), lambda i, j: (i, 0)), pl.BlockSpec((1, D_MODEL), lambda i, j: (0, 0)),
                  pl.BlockSpec((D_MODEL, th), lambda i, j: (0, j)), pl.BlockSpec((D_MODEL, th), lambda i, j: (0, j)),
                  pl.BlockSpec((th, D_MODEL), lambda i, j: (j, 0)), pl.BlockSpec((1, D_MODEL), lambda i, j: (0, 0))],
        out_specs=pl.BlockSpec((tm, D_MODEL), lambda i, j: (i, 0)),
        out_shape=jax.ShapeDtypeStruct((m, D_MODEL), F32),
        scratch_shapes=[pltpu.VMEM((tm, D_MODEL), BF16), pltpu.VMEM((tm, D_MODEL), F32)],
        compiler_params=_cparams(("parallel", "arbitrary")),
        name="ffn",
    )(x, gn, wg, wu, wd, gf)


def _pair(a, axis):
    shp = a.shape
    a = a.reshape(shp[:axis] + (NSA_KV, NSA_GROUP, HEAD_DIM) + shp[axis + 1:])
    a = jnp.swapaxes(a, axis, axis + 1)
    return a.reshape(shp)


def _rope_tables(pos):
    half = ROPE_DIM // 2
    inv = ROPE_THETA ** (-jnp.arange(half, dtype=F32) / half)
    ang = pos.astype(F32)[:, None] * inv[None, :]
    cos, sin = jnp.cos(ang), jnp.sin(ang)
    l = np.arange(LANES) % HEAD_DIM
    idx = l % half
    c = jnp.where(l < ROPE_DIM, cos[:, idx], 1.0)
    sa = jnp.where(l < half, -sin[:, idx], 0.0)
    sb = jnp.where((l >= half) & (l < ROPE_DIM), sin[:, idx], 0.0)
    return c, sa, sb


def _prep_w_in(w):
    offs = np.cumsum([0, NSA_W, 3 * NSA_HEADS] + [NSA_KV * HEAD_DIM] * 6 + [HG_W] * 4)
    seg = lambda i: w[:, offs[i]:offs[i + 1]]
    wq = _pair(seg(0), 1)
    wks = seg(4)
    z = jnp.zeros((D_MODEL, HEAD_DIM), w.dtype)
    aug = jnp.concatenate([wks[:, :HEAD_DIM], z, z, wks[:, HEAD_DIM:]], axis=1)
    gates = jnp.pad(seg(1), ((0, 0), (0, LANES - 3 * NSA_HEADS)))
    out = jnp.concatenate([wq] + [seg(i) for i in range(2, 8)] + [aug] + [seg(i) for i in range(8, 12)] + [gates],
                          axis=1)
    assert out.shape[1] == N_PROJ
    return out.astype(BF16)


def _prep_compress(pe, w1, w2):
    npair = CMP_STRIDE // 2
    pe_big = jnp.broadcast_to(pe.reshape(2, CMP_STRIDE, 1, HEAD_DIM), (2, CMP_STRIDE, NSA_KV, HEAD_DIM))
    pe_big = jnp.pad(pe_big.reshape(2, CMP_STRIDE * LANES), ((0, 6), (0, 0)))
    w1r = w1.reshape(2, npair, 2, HEAD_DIM, CMP_HID)
    eye = jnp.eye(NSA_KV, dtype=w1.dtype)
    w1_big = jnp.einsum('cjldh,ge->jlgdceh', w1r, eye).reshape(npair, 2 * LANES, 4 * CMP_HID)
    w2_big = jnp.einsum('hd,ge->ghed', w2, eye).reshape(NSA_KV * CMP_HID, LANES)
    return pe_big, w1_big.astype(BF16), w2_big.astype(BF16)


def _pad_rows(a, n):
    return jnp.pad(a, ((0, 0), (0, n - a.shape[1]), (0, 0)))


def kernel(x_prompt, x_sample, cache_cmp_k, cache_cmp_v, cache_sel_k, cache_sel_v, cache_win_k, cache_win_v,
           state_hgrn, cache_mem_k, cache_mem_v, page_table, mem_prompt, norm_mix, w_in, cmp_pe_k, cmp_w1_k,
           cmp_w2_k, cmp_pe_v, cmp_w1_v, cmp_w2_v, nsa_out_norm, hg_lb_logits, hg_out_norm, w_out, norm_ca,
           norm_mem, ca_wq, ca_wk, ca_wv, ca_wo, norm_ffn, ffn_w_gate, ffn_w_up, ffn_w_down, final_norm):
    B, T = x_prompt.shape[:2]
    DB, TS = x_sample.shape[:2]
    n_pages = page_table.shape[1]
    past = n_pages * PAGE_SIZE
    n_pool = cache_cmp_k.shape[1]
    wb = cache_win_k.shape[2]
    ml = mem_prompt.shape[1]
    assert w_in.shape[0] == 1, "single layer"
    row2 = lambda a: a.reshape(1, -1)

    w_in_p = _prep_w_in(w_in[0])
    pek, w1k, w2k = _prep_compress(cmp_pe_k[0], cmp_w1_k[0], cmp_w2_k[0])
    pev, w1v, w2v = _prep_compress(cmp_pe_v[0], cmp_w1_v[0], cmp_w2_v[0])
    lb = jnp.cumsum(jax.nn.softmax(hg_lb_logits.astype(F32), axis=0), axis=0)[0].reshape(1, HG_W)
    gn_nsa = row2(_pair(nsa_out_norm[0], 0))
    wo_a = _pair(w_out[0][:NSA_W], 0).astype(BF16)
    wo_b = w_out[0][NSA_W:].astype(BF16)
    wq_ca = ca_wq[0].astype(BF16)
    wo_ca = ca_wo[0].astype(BF16)
    w_mem = jnp.concatenate([ca_wk[0], ca_wv[0]], axis=1).astype(BF16)
    wg, wu, wd = ffn_w_gate[0].astype(BF16), ffn_w_up[0].astype(BF16), ffn_w_down[0].astype(BF16)
    g_mix, g_ca, g_ffn, g_fin = row2(norm_mix[0]), row2(norm_ca[0]), row2(norm_ffn[0]), row2(final_norm)
    g_hg = row2(hg_out_norm[0])

    tabs_p = _rope_tables(jnp.arange(T))
    tabs_s = _rope_tables(past + (jnp.arange(DB * TS) % TS))
    blk = np.arange(T) // SEL_BLK
    lanes2 = np.arange(2 * LANES)
    oh_np = ((lanes2[None, :] >= HEAD_DIM) & (lanes2[None, :] < 3 * HEAD_DIM)
             & ((lanes2[None, :] - HEAD_DIM) % HEAD_DIM == blk[:, None]))
    oh_p = jnp.asarray(oh_np.astype(np.float32))
    oh_s = jnp.zeros((DB * TS, 2 * LANES), F32)

    def tail(x, o_nsa, o_hg, mk, mv, nb, tm, tiled):
        rows = x.shape[0]
        x1, qca = _mixout(x, o_nsa, o_hg, gn_nsa, wo_a, wo_b, g_ca, wq_ca, tm)
        per = rows // nb
        tq = min(per, 256)
        if per % 16:
            tq = -(-per // 16) * 16
            qca3 = _pad_rows(qca.reshape(nb, per, D_MODEL), tq)
        else:
            qca3 = qca.reshape(nb, per, D_MODEL)
        oca = _cross_attn(qca3, mk, mv, tq, ml, tiled)[:, :per].reshape(rows, D_MODEL)
        x2 = _mm_res(oca, wo_ca, x1, tm)
        return _ffn(x2, g_ffn, wg, wu, wd, g_fin, min(rows, 512), wg.shape[1] // 2)

    M = B * T
    (q, qr, kcp, vcp, _, _, _, _, ksa, kwb, vwb, hm, gt, kct, vct, kst, vst, kwt, vwt) = _proj_in(
        x_prompt.reshape(M, D_MODEL), g_mix, w_in_p, tabs_p, oh_p, 256, B)
    r3 = lambda a: a.reshape(B, T, -1)
    cmp_w = (pek, pev, w1k, w1v, w2k, w2v)
    kc_p, vc_p = _compress_rows(r3(kcp), r3(vcp), cmp_w)
    o_nsa = _nsa_prompt(r3(q), r3(qr), r3(gt), kc_p, vc_p, r3(ksa), vst, r3(kwb), r3(vwb), 128, 512)
    o_hg, s_p = _hgrn(r3(hm), lb, g_hg, jnp.zeros((B, HG_HEADS, HG_DK, HG_DV), F32), 128, T)
    mkv = _norm_mm(mem_prompt.reshape(B * ml, D_MODEL), row2(norm_mem[0]), w_mem, 256)
    mk_p = mkv[:, :D_MODEL].reshape(B, ml, D_MODEL)
    mv_p = mkv[:, D_MODEL:].reshape(B, ml, D_MODEL)
    y_p = tail(x_prompt.reshape(M, D_MODEL), o_nsa.reshape(M, NSA_W), o_hg.reshape(M, HG_W), mk_p, mv_p, B, 256, False)

    kv5 = lambda a, n: a.reshape(1, -1, n, NSA_KV, HEAD_DIM)
    wbp = min(WINDOW, T)
    nat = lambda a: jnp.transpose(a.reshape(B, NSA_KV, HEAD_DIM, -1), (0, 3, 1, 2))[None]
    outs_p = (nat(kct), nat(vct), nat(kst), nat(vst),
              nat(kwt[:, :, T - wbp:]), nat(vwt[:, :, T - wbp:]), s_p[None],
              mk_p.reshape(1, B, ml, CA_HEADS, CA_HD), mv_p.reshape(1, B, ml, CA_HEADS, CA_HD))

    MS = DB * TS
    (q, qr, kcs, vcs, kss, vss, kws, vws, _, _, _, hm, gt) = _proj_in(
        x_sample.reshape(MS, D_MODEL), g_mix, w_in_p, tabs_s, oh_s, MS, 1)[:13]
    pt_flat = page_table.reshape(-1).astype(I32)
    pool_t = lambda a: jnp.transpose(a[0], (0, 2, 3, 1)).reshape(n_pool, LANES, PAGE_SIZE)
    kc_s, vc_s = _compress_paged(pt_flat, pool_t(cache_cmp_k), pool_t(cache_cmp_v), cmp_w, DB, n_pages)
    tp, tn = 8, 16
    s3 = lambda a, n: _pad_rows(a.reshape(DB, TS, -1), n)
    key_blk = np.arange(past) // SEL_BLK
    oh_keys = jnp.asarray((np.arange(LANES)[:, None] == key_blk[None, :]).astype(np.float32)).astype(BF16)
    nseg = past // CMP_STRIDE
    gmat = jnp.asarray((np.arange(nseg)[:, None] // (SEL_BLK // CMP_STRIDE)
                        == np.arange(LANES)[None, :]).astype(np.float32))
    o_nsa_s, nwk, nwv = _nsa_sample(
        pt_flat, s3(q, tp), s3(qr, tp), s3(gt, tp), kc_s, vc_s, s3(kss, tn), s3(vss, tn), s3(kws, tn), s3(vws, tn),
        cache_win_k[0].reshape(DB, wb, LANES), cache_win_v[0].reshape(DB, wb, LANES), oh_keys, gmat,
        pool_t(cache_sel_k), pool_t(cache_sel_v), n_pages, past, TS)
    o_hg_s, s_s = _hgrn(s3(hm, 8), lb, g_hg, state_hgrn[0].astype(F32), 8, TS)
    nchunk = CA_HD // LANES
    mem_t = lambda a: jnp.transpose(a[0].reshape(DB, ml, CA_HEADS, nchunk, LANES), (0, 1, 3, 2, 4)).reshape(
        DB, ml * nchunk * CA_HEADS, LANES)
    y_s = tail(x_sample.reshape(MS, D_MODEL), o_nsa_s[:, :TS].reshape(MS, NSA_W), o_hg_s[:, :TS].reshape(MS, HG_W),
               mem_t(cache_mem_k), mem_t(cache_mem_v), DB, MS, True)

    outs_s = (kv5(kcs, TS), kv5(vcs, TS), kv5(kss, TS), kv5(vss, TS),
              nwk.reshape(1, DB, wb, NSA_KV, HEAD_DIM), nwv.reshape(1, DB, wb, NSA_KV, HEAD_DIM), s_s[None])
    return (y_p.reshape(B, T, D_MODEL), y_s.reshape(DB, TS, D_MODEL)) + outs_p + outs_s
```

```python
import functools

import numpy as np
import jax
import jax.numpy as jnp
from jax import lax
from jax.experimental import pallas as pl
from jax.experimental.pallas import tpu as pltpu

F32 = jnp.float32
BF16 = jnp.bfloat16
I32 = jnp.int32

D_MODEL = 1024
NSA_HEADS = 8
NSA_KV = 2
HEAD_DIM = 64
NSA_GROUP = NSA_HEADS // NSA_KV
NSA_W = NSA_HEADS * HEAD_DIM
CMP_STRIDE = 16
CMP_LEN = 32
CMP_HID = 128
SEL_BLK = 64
SEL_TOPK = 16
WINDOW = 512
ROPE_THETA = 500000.0
ROPE_DIM = HEAD_DIM // 4
HG_HEADS = 4
HG_DK = 128
HG_DV = 128
HG_W = HG_HEADS * HG_DV
CA_HEADS = 4
CA_HD = D_MODEL // CA_HEADS
PAGE_SIZE = 128
EPS = 1e-6
FORCE_SCORE = 1e4
NEG = -1e30
MASK_BIAS = -1e9
LANES = 128
VMEM_LIMIT = 56 * 1024 * 1024
ROW_TILE = 256
FFN_ROW_TILE = 512
NSA_Q_TILE = 128
NSA_KEY_TILE = 512
HG_CHUNK = 128
SAMPLE_Q_PAD = 8
SAMPLE_KEY_PAD = 16

C_Q = 0
C_KC, C_VC, C_KS, C_VS, C_KW, C_VW = 512, 640, 768, 896, 1024, 1152
C_AUG = 1280
C_HG = 1536
C_GT = 3584
N_PROJ = 3712


def _cparams(sem):
    return pltpu.CompilerParams(dimension_semantics=sem, vmem_limit_bytes=VMEM_LIMIT)


def _nt(a, b):
    return lax.dot_general(a, b, (((1,), (1,)), ((), ())), preferred_element_type=F32)


def _tn(a, b):
    return lax.dot_general(a, b, (((0,), (0,)), ((), ())), preferred_element_type=F32)


def _mm(a, b):
    return jnp.dot(a, b, preferred_element_type=F32)


def _rms(x, g):
    return x * lax.rsqrt(jnp.mean(x * x, axis=-1, keepdims=True) + EPS) * g


def _masked_softmax(s, mask, axis):
    s = jnp.where(mask, s, NEG)
    m = jnp.max(s, axis=axis, keepdims=True)
    e = jnp.where(mask, jnp.exp(s - m), 0.0)
    return e / jnp.maximum(jnp.sum(e, axis=axis, keepdims=True), 1e-20)


def _log2(n):
    l = int(n).bit_length() - 1
    assert (1 << l) == n, n
    return l


def _proj_in_kernel(x_ref, g_ref, w_ref, c_ref, sa_ref, sb_ref, oh_ref,
                    q_ref, qr_ref, kc_ref, vc_ref, ks_ref, vs_ref, kw_ref, vw_ref,
                    ksa_ref, kwb_ref, vwb_ref, hm_ref, gt_ref,
                    kct_ref, vct_ref, kst_ref, vst_ref, kwt_ref, vwt_ref):
    xn = _rms(x_ref[...], g_ref[...]).astype(BF16)
    c, sa, sb = c_ref[...], sa_ref[...], sb_ref[...]

    def mm(lo, hi):
        return _mm(xn, w_ref[:, lo:hi])

    def rope(y):
        n = y.shape[1] // LANES
        cc, aa, bb = (jnp.concatenate([t] * n, axis=1) if n > 1 else t for t in (c, sa, sb))
        w = y.shape[1]
        return y * cc + pltpu.roll(y, w - ROPE_DIM // 2, 1) * aa + pltpu.roll(y, ROPE_DIM // 2, 1) * bb

    yq = mm(C_Q, C_Q + NSA_W) * (HEAD_DIM ** -0.5)
    q_ref[...] = yq.astype(BF16)
    qr_ref[...] = rope(yq).astype(BF16)
    kw = vw = None
    for col, roped, ref, tref in ((C_KC, False, kc_ref, kct_ref), (C_VC, False, vc_ref, vct_ref),
                                  (C_KS, True, ks_ref, kst_ref), (C_VS, False, vs_ref, vst_ref),
                                  (C_KW, True, kw_ref, kwt_ref), (C_VW, False, vw_ref, vwt_ref)):
        y = mm(col, col + LANES)
        y = rope(y) if roped else y
        ref[...] = y
        tref[0] = y.T
        kw = y if col == C_KW else kw
        vw = y if col == C_VW else vw
    kwb_ref[...] = kw.astype(BF16)
    vwb_ref[...] = vw.astype(BF16)
    ksa_ref[...] = (rope(mm(C_AUG, C_AUG + 2 * LANES)) + oh_ref[...]).astype(BF16)
    hm_ref[...] = mm(C_HG, C_HG + 4 * HG_W)
    gt_ref[...] = mm(C_GT, C_GT + LANES)


def _proj_in(x2d, g, w, tabs, oh, tm, nb):
    m = x2d.shape[0]
    per = m // nb
    nt_tab = tabs[0].shape[0] // tm
    nt_seq = per // tm
    row = lambda i: (i, 0)
    tab = lambda i: (i % nt_tab, 0)
    const = lambda i: (0, 0)
    widths = [(NSA_W, BF16), (NSA_W, BF16)] + [(LANES, F32)] * 6 + [(2 * LANES, BF16)] + [(LANES, BF16)] * 2 \
        + [(4 * HG_W, F32), (LANES, F32)]
    return pl.pallas_call(
        _proj_in_kernel,
        grid=(m // tm,),
        in_specs=[pl.BlockSpec((tm, D_MODEL), row), pl.BlockSpec((1, D_MODEL), const),
                  pl.BlockSpec((D_MODEL, N_PROJ), const),
                  pl.BlockSpec((tm, LANES), tab), pl.BlockSpec((tm, LANES), tab), pl.BlockSpec((tm, LANES), tab),
                  pl.BlockSpec((tm, 2 * LANES), tab)],
        out_specs=[pl.BlockSpec((tm, wd), row) for wd, _ in widths]
        + [pl.BlockSpec((1, LANES, tm), lambda i: (i // nt_seq, 0, i % nt_seq))] * 6,
        out_shape=[jax.ShapeDtypeStruct((m, wd), dt) for wd, dt in widths]
        + [jax.ShapeDtypeStruct((nb, LANES, per), F32)] * 6,
        compiler_params=_cparams(("parallel",)),
        name="proj_in",
    )(x2d, g, w, *tabs, oh)


def _page_copy(pool_hbm, page, buf, slot, p, sem):
    cols = pl.ds(pl.multiple_of(p * PAGE_SIZE, PAGE_SIZE), PAGE_SIZE)
    return pltpu.make_async_copy(pool_hbm.at[page], buf.at[slot, :, cols], sem)


def _gather_start(pt_ref, pools, bufs, sems, b, slot, n_pages):
    def body(p, carry):
        page = pt_ref[b * n_pages + p]
        for pool, buf, sem in zip(pools, bufs, sems):
            _page_copy(pool, page, buf, slot, p, sem.at[slot]).start()
        return carry
    lax.fori_loop(0, n_pages, body, 0)


def _gather_wait(pools, bufs, sems, slot, n_pages):
    def body(p, carry):
        for pool, buf, sem in zip(pools, bufs, sems):
            _page_copy(pool, 0, buf, slot, p, sem.at[slot]).wait()
        return carry
    lax.fori_loop(0, n_pages, body, 0)


def _gather_step(pt_ref, pools, bufs, sems, n_pages):
    b = pl.program_id(0)
    nb = pl.num_programs(0)
    slot = b % 2

    @pl.when(b == 0)
    def _():
        _gather_start(pt_ref, pools, bufs, sems, 0, 0, n_pages)

    @pl.when(b + 1 < nb)
    def _():
        _gather_start(pt_ref, pools, bufs, sems, b + 1, 1 - slot, n_pages)

    _gather_wait(pools, bufs, sems, slot, n_pages)
    return slot


def _compress_core(tok, pe_ref, w1_ref, w2_ref, bias_scr, first, out, nseg):
    npair = CMP_STRIDE // 2
    nh = NSA_KV * CMP_HID

    @pl.when(first)
    def _():
        acc = _mm(pe_ref[:, 0:2 * LANES].astype(BF16), w1_ref[0])
        for jj in range(1, npair):
            acc = acc + _mm(pe_ref[:, jj * 2 * LANES:(jj + 1) * 2 * LANES].astype(BF16), w1_ref[jj])
        bias_scr[...] = acc

    acc = None
    for jj in range(npair):
        xp = jnp.concatenate([tok[pl.ds(2 * jj, nseg, stride=CMP_STRIDE), :],
                              tok[pl.ds(2 * jj + 1, nseg, stride=CMP_STRIDE), :]], axis=1).astype(BF16)
        part = _mm(xp, w1_ref[jj])
        acc = part if acc is None else acc + part
    ha = acc[:, 0:nh] + bias_scr[0:1, 0:nh]
    hb = acc[:, nh:2 * nh] + bias_scr[1:2, nh:2 * nh]
    h = jax.nn.gelu(ha + pltpu.roll(hb, nseg - 1, 0))
    row = lax.broadcasted_iota(I32, (nseg, 1), 0)
    h = jnp.where(row < nseg - 1, h, 0.0)
    out[...] = _mm(h.astype(BF16), w2_ref[...]).astype(BF16)


def _compress_rows_kernel(k_ref, v_ref, pek_ref, pev_ref, w1k_ref, w1v_ref, w2k_ref, w2v_ref,
                          kc_ref, vc_ref, bias_scr, *, nseg):
    first = pl.program_id(0) == 0
    for i, (tok, pe, w1, w2, out) in enumerate(((k_ref, pek_ref, w1k_ref, w2k_ref, kc_ref),
                                                (v_ref, pev_ref, w1v_ref, w2v_ref, vc_ref))):
        _compress_core(tok.at[0], pe, w1, w2, bias_scr.at[i], first, out.at[0], nseg)


def _compress_paged_kernel(pt_ref, poolk_hbm, poolv_hbm, pek_ref, pev_ref, w1k_ref, w1v_ref, w2k_ref, w2v_ref,
                           kc_ref, vc_ref, xk_buf, xv_buf, tokk_scr, tokv_scr, bias_scr, semk, semv, *, n_pages):
    slot = _gather_step(pt_ref, (poolk_hbm, poolv_hbm), (xk_buf, xv_buf), (semk, semv), n_pages)
    first = pl.program_id(0) == 0
    nseg = n_pages * (PAGE_SIZE // CMP_STRIDE)
    for i, (xbuf, tok, pe, w1, w2, out) in enumerate(((xk_buf, tokk_scr, pek_ref, w1k_ref, w2k_ref, kc_ref),
                                                      (xv_buf, tokv_scr, pev_ref, w1v_ref, w2v_ref, vc_ref))):
        tok[...] = xbuf[slot].T
        _compress_core(tok, pe, w1, w2, bias_scr.at[i], first, out.at[0], nseg)


def _compress_weight_specs(const2, const3):
    npair = CMP_STRIDE // 2
    return [pl.BlockSpec((8, CMP_STRIDE * LANES), const2), pl.BlockSpec((8, CMP_STRIDE * LANES), const2),
            pl.BlockSpec((npair, 2 * LANES, 4 * CMP_HID), const3), pl.BlockSpec((npair, 2 * LANES, 4 * CMP_HID), const3),
            pl.BlockSpec((NSA_KV * CMP_HID, LANES), const2), pl.BlockSpec((NSA_KV * CMP_HID, LANES), const2)]


def _compress_rows(k_rows, v_rows, weights):
    nb, ntok = k_rows.shape[:2]
    nseg = ntok // CMP_STRIDE
    per_b = lambda b: (b, 0, 0)
    return pl.pallas_call(
        functools.partial(_compress_rows_kernel, nseg=nseg),
        grid=(nb,),
        in_specs=[pl.BlockSpec((1, ntok, LANES), per_b), pl.BlockSpec((1, ntok, LANES), per_b)]
        + _compress_weight_specs(lambda b: (0, 0), lambda b: (0, 0, 0)),
        out_specs=[pl.BlockSpec((1, nseg, LANES), per_b)] * 2,
        out_shape=[jax.ShapeDtypeStruct((nb, nseg, LANES), BF16)] * 2,
        scratch_shapes=[pltpu.VMEM((2, 8, 4 * CMP_HID), F32)],
        compiler_params=_cparams(("arbitrary",)),
        name="compress_rows",
    )(k_rows, v_rows, *weights)


def _compress_paged(page_table_flat, pool_k, pool_v, weights, n_batch, n_pages):
    past = n_pages * PAGE_SIZE
    nseg = past // CMP_STRIDE
    gs = pltpu.PrefetchScalarGridSpec(
        num_scalar_prefetch=1,
        grid=(n_batch,),
        in_specs=[pl.BlockSpec(memory_space=pl.ANY), pl.BlockSpec(memory_space=pl.ANY)]
        + _compress_weight_specs(lambda b, pt: (0, 0), lambda b, pt: (0, 0, 0)),
        out_specs=[pl.BlockSpec((1, nseg, LANES), lambda b, pt: (b, 0, 0))] * 2,
        scratch_shapes=[pltpu.VMEM((2, LANES, past), F32), pltpu.VMEM((2, LANES, past), F32),
                        pltpu.VMEM((past, LANES), F32), pltpu.VMEM((past, LANES), F32),
                        pltpu.VMEM((2, 8, 4 * CMP_HID), F32),
                        pltpu.SemaphoreType.DMA((2,)), pltpu.SemaphoreType.DMA((2,))],
    )
    return pl.pallas_call(
        functools.partial(_compress_paged_kernel, n_pages=n_pages),
        grid_spec=gs,
        out_shape=[jax.ShapeDtypeStruct((n_batch, nseg, LANES), BF16)] * 2,
        compiler_params=_cparams(("arbitrary",)),
        name="compress_paged",
    )(page_table_flat, pool_k, pool_v, *weights)


def _nsa_prompt_kernel(q_ref, qr_ref, gt_ref, kc_ref, vc_ref, ksa_ref, vst_ref, kw_ref, vw_ref, o_ref,
                       psum_scr, sc_scr, *, tq, tk, ncp, nsel, topk):
    i = pl.program_id(1)
    s0 = i * tq
    rows = NSA_GROUP * tq
    hk = tk // 2
    lane = lax.broadcasted_iota(I32, (tq, LANES), 1)
    upper = lane >= HEAD_DIM
    upper_v = lax.broadcasted_iota(I32, (LANES, hk), 0) >= HEAD_DIM
    gsig = jax.nn.sigmoid(gt_ref[0])
    kc = kc_ref[0]
    vc = vc_ref[0]
    t_rows = s0 + (lax.broadcasted_iota(I32, (rows, 1), 0) & (tq - 1))
    t_cols = s0 + (lax.broadcasted_iota(I32, (1, rows), 1) & (tq - 1))
    o_c, o_w, o_st = [], [], []
    for g in range(NSA_KV):
        mine = upper if g == 1 else jnp.logical_not(upper)

        def stack(ref, fill):
            return jnp.concatenate(
                [jnp.where(mine, ref[0, :, r * LANES:(r + 1) * LANES], fill) for r in range(NSA_GROUP)], axis=0)

        cend = lax.broadcasted_iota(I32, (rows, ncp), 1) * CMP_STRIDE + (CMP_LEN - 1)
        p_c = _masked_softmax(_nt(stack(q_ref, 0), kc), cend <= t_rows, 1)
        o_c.append(_mm(p_c.astype(BF16), vc))
        psum = p_c[0:tq]
        for r in range(1, NSA_GROUP):
            psum = psum + p_c[r * tq:(r + 1) * tq]
        psum_scr[...] = psum.T
        cps = SEL_BLK // CMP_STRIDE
        imp = psum_scr[pl.ds(0, nsel, stride=cps), :]
        for u in range(1, cps):
            imp = imp + psum_scr[pl.ds(u, nsel, stride=cps), :]
        j = lax.broadcasted_iota(I32, (nsel, tq), 0)
        cur = (s0 + lax.broadcasted_iota(I32, (nsel, tq), 1)) >> _log2(SEL_BLK)
        avail = j <= cur
        forced = (j == 0) | (j == cur) | (j == cur - 1)
        score = jnp.where(avail, jnp.where(forced, FORCE_SCORE, imp), -FORCE_SCORE)
        sc_scr[...] = score

        def rank_body(jp, cnt):
            other = sc_scr[pl.ds(jp, 1), :]
            beats = (other > score) | ((other == score) & (jp < j))
            return cnt + beats.astype(I32)

        cnt = lax.fori_loop(0, nsel, rank_body, jnp.zeros((nsel, tq), I32), unroll=8)
        bias_t = jnp.where((cnt < topk) & avail, 0.0, MASK_BIAS)
        if nsel < HEAD_DIM:
            bias_t = jnp.concatenate([bias_t, jnp.full((HEAD_DIM - nsel, tq), MASK_BIAS, F32)], axis=0)
        bias = jnp.concatenate([bias_t, bias_t], axis=0).T.astype(BF16)
        qs = stack(qr_ref, bias)
        glanes = slice(g * LANES, (g + 1) * LANES)
        mine_v = upper_v if g == 1 else jnp.logical_not(upper_v)

        def scores(k0):
            return _nt(ksa_ref[0, pl.ds(k0, tk), glanes], qs)

        def col_max(s):
            return tuple(jnp.max(s[c * hk:(c + 1) * hk], axis=0, keepdims=True) for c in range(2))

        def update(s, mx, k0, stats):
            new = []
            for c, (m, acc) in enumerate(stats):
                v_one = jnp.where(mine_v, vst_ref[0, :, pl.ds(k0 + c * hk, hk)], 1.0).astype(BF16)
                m_new = jnp.maximum(m, mx[c])
                p = jnp.exp((s[c * hk:(c + 1) * hk] - m_new).astype(BF16))
                new.append((m_new, jnp.exp(m - m_new) * acc + _mm(v_one, p)))
            return tuple(new)

        def sel_body(kk, carry):
            s_cur, mx_cur, stats = carry
            k0 = pl.multiple_of(kk * tk, tk)
            s_next = scores(k0 + tk)
            stats = update(s_cur, mx_cur, k0, stats)
            return s_next, col_max(s_next), stats

        n_full = s0 // tk
        stats0 = tuple((jnp.full((1, rows), NEG, F32), jnp.zeros((LANES, rows), F32)) for _ in range(2))
        s_first = scores(0)
        s_last, _, stats = lax.fori_loop(0, n_full, sel_body, (s_first, col_max(s_first), stats0))
        k_last = pl.multiple_of(n_full * tk, tk)
        causal = k_last + lax.broadcasted_iota(I32, (tk, rows), 0) <= t_cols
        s_last = jnp.where(causal, s_last, NEG)
        (m_a, acc_a), (m_b, acc_b) = update(s_last, col_max(s_last), k_last, stats)
        m_s = jnp.maximum(m_a, m_b)
        acc_s = jnp.exp(m_a - m_s) * acc_a + jnp.exp(m_b - m_s) * acc_b
        num = acc_s[g * HEAD_DIM:(g + 1) * HEAD_DIM]
        den = acc_s[(1 - g) * HEAD_DIM:(2 - g) * HEAD_DIM]
        o_st.append(num / jnp.maximum(den, 1e-20))
        wl = WINDOW + tq
        ws = pl.multiple_of(jnp.maximum(s0 - WINDOW, 0), tq)
        kpos = ws + lax.broadcasted_iota(I32, (rows, wl), 1)
        wmask = (kpos <= t_rows) & (kpos > t_rows - WINDOW)
        p_w = _masked_softmax(_nt(stack(qr_ref, 0), kw_ref[0, pl.ds(ws, wl), :]), wmask, 1)
        o_w.append(_mm(p_w.astype(BF16), vw_ref[0, pl.ds(ws, wl), :]))
    out_slots = []
    for r in range(NSA_GROUP):
        rs = slice(r * tq, (r + 1) * tq)
        o_s = jnp.concatenate([o_st[0][:, rs], o_st[1][:, rs]], axis=0).T
        res = []
        for g in range(NSA_KV):
            h = g * NSA_GROUP + r
            res.append(gsig[:, 3 * h:3 * h + 1] * o_c[g][rs] + gsig[:, 3 * h + 1:3 * h + 2] * o_s
                       + gsig[:, 3 * h + 2:3 * h + 3] * o_w[g][rs])
        out_slots.append(jnp.where(upper, res[1], res[0]))
    o_ref[0] = jnp.concatenate(out_slots, axis=1)


def _nsa_prompt(q, qr, gt, kc, vc, ksa, vst, kwb, vwb, tq, tk):
    b, t = q.shape[:2]
    ncp = t // CMP_STRIDE
    nsel = t // SEL_BLK
    assert t % tk == 0 and tk % tq == 0 and t >= WINDOW + tq and nsel <= HEAD_DIM and WINDOW % tq == 0
    blk = lambda bb, i: (bb, i, 0)
    whole = lambda bb, i: (bb, 0, 0)
    return pl.pallas_call(
        functools.partial(_nsa_prompt_kernel, tq=tq, tk=tk, ncp=ncp, nsel=nsel, topk=min(SEL_TOPK, nsel)),
        grid=(b, t // tq),
        in_specs=[pl.BlockSpec((1, tq, NSA_W), blk), pl.BlockSpec((1, tq, NSA_W), blk),
                  pl.BlockSpec((1, tq, LANES), blk),
                  pl.BlockSpec((1, ncp, LANES), whole), pl.BlockSpec((1, ncp, LANES), whole),
                  pl.BlockSpec((1, t, 2 * LANES), whole), pl.BlockSpec((1, LANES, t), whole),
                  pl.BlockSpec((1, t, LANES), whole), pl.BlockSpec((1, t, LANES), whole)],
        out_specs=pl.BlockSpec((1, tq, NSA_W), blk),
        out_shape=jax.ShapeDtypeStruct((b, t, NSA_W), F32),
        scratch_shapes=[pltpu.VMEM((ncp, tq), F32), pltpu.VMEM((nsel, tq), F32)],
        compiler_params=_cparams(("parallel", "parallel")),
        name="nsa_prompt",
    )(q, qr, gt, kc, vc, ksa, vst, kwb, vwb)


def _nsa_sample_kernel(pt_ref, q_ref, qr_ref, gt_ref, kc_ref, vc_ref, ksn_ref, vsn_ref, kwn_ref, vwn_ref,
                       wink_ref, winv_ref, oht_ref, gmat_ref, poolk_hbm, poolv_hbm,
                       o_ref, nwk_ref, nwv_ref, kbuf, vbuf, semk, semv,
                       *, n_pages, past, ts, tp, tn, wb, topk_past):
    slot = _gather_step(pt_ref, (poolk_hbm, poolv_hbm), (kbuf, vbuf), (semk, semv), n_pages)
    nseg = n_pages * (PAGE_SIZE // CMP_STRIDE)
    npb = past // SEL_BLK
    rows = NSA_HEADS * tp
    lane = lax.broadcasted_iota(I32, (tp, LANES), 1)
    upper = lane >= HEAD_DIM

    def stack(ref):
        parts = []
        for g in range(NSA_KV):
            mine = upper if g == 1 else jnp.logical_not(upper)
            parts += [jnp.where(mine, ref[0, :, r * LANES:(r + 1) * LANES], 0) for r in range(NSA_GROUP)]
        return jnp.concatenate(parts, axis=0)

    t_rows = lax.broadcasted_iota(I32, (rows, 1), 0) & (tp - 1)
    qc = stack(q_ref)
    cend = lax.broadcasted_iota(I32, (rows, nseg), 1) * CMP_STRIDE + (CMP_LEN - 1)
    p_c = _masked_softmax(_nt(qc, kc_ref[0]), cend <= past + t_rows, 1)
    o_c = _mm(p_c.astype(BF16), vc_ref[0])
    impn = []
    for g in range(NSA_KV):
        acc = p_c[(g * NSA_GROUP) * tp:(g * NSA_GROUP + 1) * tp]
        for r in range(1, NSA_GROUP):
            acc = acc + p_c[(g * NSA_GROUP + r) * tp:(g * NSA_GROUP + r + 1) * tp]
        impn.append(acc)
    impn = jnp.concatenate(impn, axis=0)
    imp = jnp.dot(impn, gmat_ref[...], precision=lax.Precision.HIGHEST, preferred_element_type=F32)
    j = lax.broadcasted_iota(I32, (NSA_KV * tp, LANES), 1)
    avail = j < npb
    forced = (j == 0) | (j == npb - 1)
    score = jnp.where(avail, jnp.where(forced, FORCE_SCORE, imp), -FORCE_SCORE)
    cnt = jnp.zeros((NSA_KV * tp, LANES), I32)
    for jp in range(npb):
        other = score[:, jp:jp + 1]
        beats = (other > score) | ((other == score) & (jp < j))
        cnt = cnt + beats.astype(I32)
    bias = jnp.where((cnt < topk_past) & avail, 0.0, MASK_BIAS).astype(BF16)
    bias = jnp.concatenate([bias[0:tp]] * NSA_GROUP + [bias[tp:2 * tp]] * NSA_GROUP, axis=0)
    qs = stack(qr_ref)
    key_t = lax.broadcasted_iota(I32, (rows, tn), 1)
    new_mask = (key_t <= t_rows) & (key_t < ts)
    s_p = _mm(qs, kbuf[slot].astype(BF16)) + _mm(bias, oht_ref[...])
    s_n = jnp.where(new_mask, _nt(qs, ksn_ref[0].astype(BF16)), NEG)
    m_s = jnp.maximum(jnp.max(s_p, axis=1, keepdims=True), jnp.max(s_n, axis=1, keepdims=True))
    e_p = jnp.exp(s_p - m_s)
    e_n = jnp.where(new_mask, jnp.exp(s_n - m_s), 0.0)
    l_s = jnp.sum(e_p, axis=1, keepdims=True) + jnp.sum(e_n, axis=1, keepdims=True)
    o_s = (_nt(e_p.astype(BF16), vbuf[slot].astype(BF16)) + _mm(e_n.astype(BF16), vsn_ref[0].astype(BF16))) \
        / jnp.maximum(l_s, 1e-20)
    wk = wink_ref[0]
    wv = winv_ref[0]
    iw = lax.broadcasted_iota(I32, (rows, wb), 1)
    wmask = (iw > t_rows + (wb - WINDOW)) & (iw >= wb - past)
    s_w = jnp.where(wmask, _nt(qs, wk.astype(BF16)), NEG)
    s_wn = jnp.where(new_mask, _nt(qs, kwn_ref[0].astype(BF16)), NEG)
    m_w = jnp.maximum(jnp.max(s_w, axis=1, keepdims=True), jnp.max(s_wn, axis=1, keepdims=True))
    e_w = jnp.where(wmask, jnp.exp(s_w - m_w), 0.0)
    e_wn = jnp.where(new_mask, jnp.exp(s_wn - m_w), 0.0)
    inv = 1.0 / jnp.maximum(jnp.sum(e_w, axis=1, keepdims=True) + jnp.sum(e_wn, axis=1, keepdims=True), 1e-20)
    o_w = _mm((e_w * inv).astype(BF16), wv.astype(BF16)) + _mm((e_wn * inv).astype(BF16), vwn_ref[0].astype(BF16))
    gsig = jax.nn.sigmoid(gt_ref[0])
    slots = []
    for r in range(NSA_GROUP):
        per_g = []
        for g in range(NSA_KV):
            h = g * NSA_GROUP + r
            rs = slice(h * tp, (h + 1) * tp)
            per_g.append(gsig[:, 3 * h:3 * h + 1] * o_c[rs] + gsig[:, 3 * h + 1:3 * h + 2] * o_s[rs]
                         + gsig[:, 3 * h + 2:3 * h + 3] * o_w[rs])
        slots.append(jnp.where(upper, per_g[1], per_g[0]))
    o_ref[0] = jnp.concatenate(slots, axis=1)
    sub = lax.broadcasted_iota(I32, (8, LANES), 0)
    for win, new_ref, out in ((wk, kwn_ref, nwk_ref), (wv, vwn_ref, nwv_ref)):
        rolled = pltpu.roll(win, wb - ts, 0)
        shifted = pltpu.roll(new_ref[0, 0:8, :], 8 - ts, 0)
        out[0, 0:wb - 8, :] = rolled[0:wb - 8]
        out[0, wb - 8:wb, :] = jnp.where(sub >= 8 - ts, shifted, rolled[wb - 8:wb])


def _nsa_sample(page_table_flat, q, qr, gt, kc, vc, ksn, vsn, kwn, vwn, win_k, win_v, oh, gmat, pool_k, pool_v,
                n_pages, past, ts):
    db, tp = q.shape[:2]
    tn = ksn.shape[1]
    wb = win_k.shape[1]
    nseg = n_pages * (PAGE_SIZE // CMP_STRIDE)
    npb = past // SEL_BLK
    assert ts <= 8 and npb <= LANES and wb >= 8 and past % PAGE_SIZE == 0
    per_b = lambda b, pt: (b, 0, 0)
    const2 = lambda b, pt: (0, 0)
    gs = pltpu.PrefetchScalarGridSpec(
        num_scalar_prefetch=1,
        grid=(db,),
        in_specs=[pl.BlockSpec((1, tp, NSA_W), per_b), pl.BlockSpec((1, tp, NSA_W), per_b),
                  pl.BlockSpec((1, tp, LANES), per_b),
                  pl.BlockSpec((1, nseg, LANES), per_b), pl.BlockSpec((1, nseg, LANES), per_b),
                  pl.BlockSpec((1, tn, LANES), per_b), pl.BlockSpec((1, tn, LANES), per_b),
                  pl.BlockSpec((1, tn, LANES), per_b), pl.BlockSpec((1, tn, LANES), per_b),
                  pl.BlockSpec((1, wb, LANES), per_b), pl.BlockSpec((1, wb, LANES), per_b),
                  pl.BlockSpec((LANES, past), const2), pl.BlockSpec((nseg, LANES), const2),
                  pl.BlockSpec(memory_space=pl.ANY), pl.BlockSpec(memory_space=pl.ANY)],
        out_specs=[pl.BlockSpec((1, tp, NSA_W), per_b), pl.BlockSpec((1, wb, LANES), per_b),
                   pl.BlockSpec((1, wb, LANES), per_b)],
        scratch_shapes=[pltpu.VMEM((2, LANES, past), F32), pltpu.VMEM((2, LANES, past), F32),
                        pltpu.SemaphoreType.DMA((2,)), pltpu.SemaphoreType.DMA((2,))],
    )
    return pl.pallas_call(
        functools.partial(_nsa_sample_kernel, n_pages=n_pages, past=past, ts=ts, tp=tp, tn=tn, wb=wb,
                          topk_past=min(SEL_TOPK, npb + 1) - 1),
        grid_spec=gs,
        out_shape=[jax.ShapeDtypeStruct((db, tp, NSA_W), F32), jax.ShapeDtypeStruct((db, wb, LANES), F32),
                   jax.ShapeDtypeStruct((db, wb, LANES), F32)],
        compiler_params=_cparams(("arbitrary",)),
        name="nsa_sample",
    )(page_table_flat, q, qr, gt, kc, vc, ksn, vsn, kwn, vwn, win_k, win_v, oh, gmat, pool_k, pool_v)


def _hgrn_kernel(hm_ref, lb_ref, og_ref, s0_ref, o_ref, sout_ref, st_scr, *, chunk, t_real, levels):
    c = pl.program_id(1)
    nc = pl.num_programs(1)

    @pl.when(c == 0)
    def _():
        for hd in range(HG_HEADS):
            st_scr[hd] = s0_ref[0, hd].T

    row = lax.broadcasted_iota(I32, (chunk, 1), 0)
    valid = (c * chunk + row) < t_real
    row_c = lax.broadcasted_iota(I32, (chunk, chunk), 0)
    col_c = lax.broadcasted_iota(I32, (chunk, chunk), 1)
    outs = []
    for hd in range(HG_HEADS):
        sl = slice(hd * HG_DK, (hd + 1) * HG_DK)
        q = hm_ref[0, :, hd * HG_DK:(hd + 1) * HG_DK]
        z = hm_ref[0, :, HG_W + hd * HG_DK:HG_W + (hd + 1) * HG_DK]
        v = hm_ref[0, :, 2 * HG_W + hd * HG_DV:2 * HG_W + (hd + 1) * HG_DV]
        gate = hm_ref[0, :, 3 * HG_W + hd * HG_DV:3 * HG_W + (hd + 1) * HG_DV]
        lb = lb_ref[:, sl]
        logf = jnp.where(valid, jnp.log(lb + (1.0 - lb) * jax.nn.sigmoid(z)), 0.0)
        k = jnp.where(valid, (1.0 - lb) * jax.nn.sigmoid(-z), 0.0)
        b = logf
        step = 1
        while step < chunk:
            b = b + jnp.where(row >= step, pltpu.roll(b, step, 0), 0.0)
            step *= 2
        st = st_scr[hd]
        o = _nt((q * jnp.exp(b)).astype(BF16), st.astype(BF16))
        if levels:
            a_mat = jnp.zeros((chunk, chunk), F32)
            for h in levels:
                if 2 * h <= 8:
                    b3 = b.reshape(chunk // 8, 8, HG_DK)
                    m = None
                    for blk in range(8 // (2 * h)):
                        mid_row = blk * 2 * h + h - 1
                        cand = jnp.broadcast_to(b3[:, mid_row:mid_row + 1, :], b3.shape).reshape(chunk, HG_DK)
                        m = cand if m is None else jnp.where((row & 7) >= blk * 2 * h, cand, m)
                else:
                    m = jnp.concatenate(
                        [jnp.broadcast_to(b[blk * 2 * h + h - 1:blk * 2 * h + h, :], (2 * h, HG_DK))
                         for blk in range(chunk // (2 * h))], axis=0)
                second = ((row >> _log2(h)) & 1) == 1
                qs = jnp.where(second, q * jnp.exp(jnp.minimum(b - m, 0.0)), 0.0)
                ks = jnp.where(second, 0.0, k * jnp.exp(jnp.minimum(m - b, 0.0)))
                same = (row_c >> _log2(2 * h)) == (col_c >> _log2(2 * h))
                a_mat = a_mat + jnp.where(same, _nt(qs.astype(BF16), ks.astype(BF16)), 0.0)
            o = o + _mm(a_mat.astype(BF16), v.astype(BF16))
        dblk = min(levels) if levels else chunk
        for d in range(dblk):
            if d == 0:
                o = o + jnp.sum(q * k, axis=1, keepdims=True) * v
                continue
            kd, bd, vd = (pltpu.roll(x, d, 0) for x in (k, b, v))
            p = jnp.where((row & (dblk - 1)) >= d, q * kd * jnp.exp(jnp.minimum(b - bd, 0.0)), 0.0)
            o = o + jnp.sum(p, axis=1, keepdims=True) * vd
        bl = b[chunk - 1:chunk, :]
        st_scr[hd] = jnp.exp(bl) * st + _tn(v.astype(BF16), (k * jnp.exp(bl - b)).astype(BF16))
        outs.append(_rms(o, og_ref[:, sl]) * (gate * jax.nn.sigmoid(gate)))
    o_ref[0] = jnp.concatenate(outs, axis=1)

    @pl.when(c == nc - 1)
    def _():
        for hd in range(HG_HEADS):
            sout_ref[0, hd] = st_scr[hd].T


def _hgrn(hm, lb, og, s0, chunk, t_real):
    b, tpad = hm.shape[:2]
    assert chunk <= 8 or chunk % 16 == 0
    levels = tuple(h for h in (64, 32, 16, 8, 4, 2, 1) if 2 * h <= chunk and chunk >= 16)
    return pl.pallas_call(
        functools.partial(_hgrn_kernel, chunk=chunk, t_real=t_real, levels=levels),
        grid=(b, tpad // chunk),
        in_specs=[pl.BlockSpec((1, chunk, 4 * HG_W), lambda bb, c: (bb, c, 0)),
                  pl.BlockSpec((1, HG_W), lambda bb, c: (0, 0)), pl.BlockSpec((1, HG_W), lambda bb, c: (0, 0)),
                  pl.BlockSpec((1, HG_HEADS, HG_DK, HG_DV), lambda bb, c: (bb, 0, 0, 0))],
        out_specs=[pl.BlockSpec((1, chunk, HG_W), lambda bb, c: (bb, c, 0)),
                   pl.BlockSpec((1, HG_HEADS, HG_DK, HG_DV), lambda bb, c: (bb, 0, 0, 0))],
        out_shape=[jax.ShapeDtypeStruct((b, tpad, HG_W), F32),
                   jax.ShapeDtypeStruct((b, HG_HEADS, HG_DK, HG_DV), F32)],
        scratch_shapes=[pltpu.VMEM((HG_HEADS, HG_DV, HG_DK), F32)],
        compiler_params=_cparams(("parallel", "arbitrary")),
        name="hgrn",
    )(hm, lb, og, s0)


def _mixout_kernel(x_ref, on_ref, oh_ref, gn_ref, woa_ref, wob_ref, gca_ref, wq_ref, x1_ref, q_ref):
    a = _rms(on_ref[...], gn_ref[...]).astype(BF16)
    x1 = x_ref[...] + _mm(a, woa_ref[...]) + _mm(oh_ref[...].astype(BF16), wob_ref[...])
    x1_ref[...] = x1
    hn = _rms(x1, gca_ref[...]).astype(BF16)
    q_ref[...] = (_mm(hn, wq_ref[...]) * (CA_HD ** -0.5)).astype(BF16)


def _mixout(x, o_nsa, o_hg, gn, woa, wob, gca, wq, tm):
    m = x.shape[0]
    row = lambda i: (i, 0)
    const = lambda i: (0, 0)
    return pl.pallas_call(
        _mixout_kernel,
        grid=(m // tm,),
        in_specs=[pl.BlockSpec((tm, D_MODEL), row), pl.BlockSpec((tm, NSA_W), row), pl.BlockSpec((tm, HG_W), row),
                  pl.BlockSpec((1, NSA_W), const), pl.BlockSpec((NSA_W, D_MODEL), const),
                  pl.BlockSpec((HG_W, D_MODEL), const), pl.BlockSpec((1, D_MODEL), const),
                  pl.BlockSpec((D_MODEL, D_MODEL), const)],
        out_specs=[pl.BlockSpec((tm, D_MODEL), row), pl.BlockSpec((tm, D_MODEL), row)],
        out_shape=[jax.ShapeDtypeStruct((m, D_MODEL), F32), jax.ShapeDtypeStruct((m, D_MODEL), BF16)],
        compiler_params=_cparams(("parallel",)),
        name="mixout",
    )(x, o_nsa, o_hg, gn, woa, wob, gca, wq)


def _norm_mm_kernel(x_ref, g_ref, w_ref, o_ref):
    o_ref[...] = _mm(_rms(x_ref[...], g_ref[...]).astype(BF16), w_ref[...])


def _norm_mm(x, g, w, tm):
    m, n = x.shape[0], w.shape[1]
    return pl.pallas_call(
        _norm_mm_kernel,
        grid=(m // tm,),
        in_specs=[pl.BlockSpec((tm, D_MODEL), lambda i: (i, 0)), pl.BlockSpec((1, D_MODEL), lambda i: (0, 0)),
                  pl.BlockSpec((D_MODEL, n), lambda i: (0, 0))],
        out_specs=pl.BlockSpec((tm, n), lambda i: (i, 0)),
        out_shape=jax.ShapeDtypeStruct((m, n), F32),
        compiler_params=_cparams(("parallel",)),
        name="norm_mm",
    )(x, g, w)


def _ca_kernel(q_ref, mk_ref, mv_ref, *rest, ml, tiled):
    o_ref = rest[-1]
    nchunk = CA_HD // LANES

    def head(ref, hh):
        if tiled:
            return jnp.concatenate([ref[0, pl.ds(c * CA_HEADS + hh, ml, stride=nchunk * CA_HEADS), :]
                                    for c in range(nchunk)], axis=1)
        return ref[0, :, hh * CA_HD:(hh + 1) * CA_HD]

    outs = []
    for hh in range(CA_HEADS):
        s = _nt(q_ref[0, :, hh * CA_HD:(hh + 1) * CA_HD], head(mk_ref, hh).astype(BF16))
        e = jnp.exp(s - jnp.max(s, axis=1, keepdims=True))
        p = e / jnp.sum(e, axis=1, keepdims=True)
        outs.append(_mm(p.astype(BF16), head(mv_ref, hh).astype(BF16)))
    o = jnp.concatenate(outs, axis=1).astype(BF16)
    if len(rest) == 3:
        wo_ref, res_ref, _ = rest
        o_ref[0] = res_ref[0] + _mm(o, wo_ref[...])
    else:
        o_ref[0] = o


def _cross_attn(q, mk, mv, tm, ml, tiled, wo=None, res=None):
    b, t = q.shape[:2]
    mem_block = mk.shape[1:]
    rows = lambda bb, i: (bb, i, 0)
    in_specs = [pl.BlockSpec((1, tm, D_MODEL), rows),
                pl.BlockSpec((1,) + mem_block, lambda bb, i: (bb, 0, 0)),
                pl.BlockSpec((1,) + mem_block, lambda bb, i: (bb, 0, 0))]
    args = (q, mk, mv)
    if wo is not None:
        in_specs += [pl.BlockSpec((D_MODEL, D_MODEL), lambda bb, i: (0, 0)), pl.BlockSpec((1, tm, D_MODEL), rows)]
        args += (wo, res)
    return pl.pallas_call(
        functools.partial(_ca_kernel, ml=ml, tiled=tiled),
        grid=(b, t // tm),
        in_specs=in_specs,
        out_specs=pl.BlockSpec((1, tm, D_MODEL), rows),
        out_shape=jax.ShapeDtypeStruct((b, t, D_MODEL), BF16 if wo is None else F32),
        compiler_params=_cparams(("parallel", "parallel")),
        name="cross_attn",
    )(*args)


def _mm_res_kernel(a_ref, w_ref, r_ref, o_ref):
    o_ref[...] = r_ref[...] + _mm(a_ref[...], w_ref[...])


def _mm_res(a, w, res, tm):
    m, k = a.shape
    n = w.shape[1]
    return pl.pallas_call(
        _mm_res_kernel,
        grid=(m // tm,),
        in_specs=[pl.BlockSpec((tm, k), lambda i: (i, 0)), pl.BlockSpec((k, n), lambda i: (0, 0)),
                  pl.BlockSpec((tm, n), lambda i: (i, 0))],
        out_specs=pl.BlockSpec((tm, n), lambda i: (i, 0)),
        out_shape=jax.ShapeDtypeStruct((m, n), F32),
        compiler_params=_cparams(("parallel",)),
        name="mm_res",
    )(a, w, res)


def _ffn_kernel(x_ref, gn_ref, wg_ref, wu_ref, wd_ref, gf_ref, y_ref, h_scr, acc_scr):
    j = pl.program_id(1)

    @pl.when(j == 0)
    def _():
        h_scr[...] = _rms(x_ref[...], gn_ref[...]).astype(BF16)
        acc_scr[...] = jnp.zeros_like(acc_scr)

    h = h_scr[...]
    gate = _mm(h, wg_ref[...])
    act = (gate * jax.nn.sigmoid(gate)) * _mm(h, wu_ref[...])
    acc_scr[...] += _mm(act.astype(BF16), wd_ref[...])

    @pl.when(j == pl.num_programs(1) - 1)
    def _():
        y_ref[...] = _rms(x_ref[...] + acc_scr[...], gf_ref[...])


def _ffn(x, gn, wg, wu, wd, gf, tm, th):
    m = x.shape[0]
    hid = wg.shape[1]
    return pl.pallas_call(
        _ffn_kernel,
        grid=(m // tm, hid // th),
        in_specs=[pl.BlockSpec((tm, D_MODEL), lambda i, j: (i, 0)), pl.BlockSpec((1, D_MODEL), lambda i, j: (0, 0)),
                  pl.BlockSpec((D_MODEL, th), lambda i, j: (0, j)), pl.BlockSpec((D_MODEL, th), lambda i, j: (0, j)),
                  pl.BlockSpec((th, D_MODEL), lambda i, j: (j, 0)), pl.BlockSpec((1, D_MODEL), lambda i, j: (0, 0))],
        out_specs=pl.BlockSpec((tm, D_MODEL), lambda i, j: (i, 0)),
        out_shape=jax.ShapeDtypeStruct((m, D_MODEL), F32),
        scratch_shapes=[pltpu.VMEM((tm, D_MODEL), BF16), pltpu.VMEM((tm, D_MODEL), F32)],
        compiler_params=_cparams(("parallel", "arbitrary")),
        name="ffn",
    )(x, gn, wg, wu, wd, gf)


def _pair(a, axis):
    shp = a.shape
    a = a.reshape(shp[:axis] + (NSA_KV, NSA_GROUP, HEAD_DIM) + shp[axis + 1:])
    a = jnp.swapaxes(a, axis, axis + 1)
    return a.reshape(shp)


def _rope_tables(pos):
    half = ROPE_DIM // 2
    inv = ROPE_THETA ** (-jnp.arange(half, dtype=F32) / half)
    ang = pos.astype(F32)[:, None] * inv[None, :]
    cos, sin = jnp.cos(ang), jnp.sin(ang)
    l = np.arange(LANES) % HEAD_DIM
    idx = l % half
    c = jnp.where(l < ROPE_DIM, cos[:, idx], 1.0)
    sa = jnp.where(l < half, -sin[:, idx], 0.0)
    sb = jnp.where((l >= half) & (l < ROPE_DIM), sin[:, idx], 0.0)
    return c, sa, sb


def _prep_w_in(w):
    offs = np.cumsum([0, NSA_W, 3 * NSA_HEADS] + [NSA_KV * HEAD_DIM] * 6 + [HG_W] * 4)
    seg = lambda i: w[:, offs[i]:offs[i + 1]]
    wq = _pair(seg(0), 1)
    wks = seg(4)
    z = jnp.zeros((D_MODEL, HEAD_DIM), w.dtype)
    aug = jnp.concatenate([wks[:, :HEAD_DIM], z, z, wks[:, HEAD_DIM:]], axis=1)
    gates = jnp.pad(seg(1), ((0, 0), (0, LANES - 3 * NSA_HEADS)))
    out = jnp.concatenate([wq] + [seg(i) for i in range(2, 8)] + [aug] + [seg(i) for i in range(8, 12)] + [gates],
                          axis=1)
    assert out.shape[1] == N_PROJ
    return out.astype(BF16)


def _prep_compress(pe, w1, w2):
    npair = CMP_STRIDE // 2
    pe_big = jnp.broadcast_to(pe.reshape(2, CMP_STRIDE, 1, HEAD_DIM), (2, CMP_STRIDE, NSA_KV, HEAD_DIM))
    pe_big = jnp.pad(pe_big.reshape(2, CMP_STRIDE * LANES), ((0, 6), (0, 0)))
    w1r = w1.reshape(2, npair, 2, HEAD_DIM, CMP_HID)
    eye = jnp.eye(NSA_KV, dtype=w1.dtype)
    w1_big = jnp.einsum('cjldh,ge->jlgdceh', w1r, eye).reshape(npair, 2 * LANES, 4 * CMP_HID)
    w2_big = jnp.einsum('hd,ge->ghed', w2, eye).reshape(NSA_KV * CMP_HID, LANES)
    return pe_big, w1_big.astype(BF16), w2_big.astype(BF16)


def _pad_rows(a, n):
    return jnp.pad(a, ((0, 0), (0, n - a.shape[1]), (0, 0)))


def kernel(x_prompt, x_sample, cache_cmp_k, cache_cmp_v, cache_sel_k, cache_sel_v, cache_win_k, cache_win_v,
           state_hgrn, cache_mem_k, cache_mem_v, page_table, mem_prompt, norm_mix, w_in, cmp_pe_k, cmp_w1_k,
           cmp_w2_k, cmp_pe_v, cmp_w1_v, cmp_w2_v, nsa_out_norm, hg_lb_logits, hg_out_norm, w_out, norm_ca,
           norm_mem, ca_wq, ca_wk, ca_wv, ca_wo, norm_ffn, ffn_w_gate, ffn_w_up, ffn_w_down, final_norm):
    B, T = x_prompt.shape[:2]
    DB, TS = x_sample.shape[:2]
    n_pages = page_table.shape[1]
    past = n_pages * PAGE_SIZE
    n_pool = cache_cmp_k.shape[1]
    wb = cache_win_k.shape[2]
    ml = mem_prompt.shape[1]
    assert w_in.shape[0] == 1, "single layer"
    row2 = lambda a: a.reshape(1, -1)

    w_in_p = _prep_w_in(w_in[0])
    pek, w1k, w2k = _prep_compress(cmp_pe_k[0], cmp_w1_k[0], cmp_w2_k[0])
    pev, w1v, w2v = _prep_compress(cmp_pe_v[0], cmp_w1_v[0], cmp_w2_v[0])
    lb = jnp.cumsum(jax.nn.softmax(hg_lb_logits.astype(F32), axis=0), axis=0)[0].reshape(1, HG_W)
    gn_nsa = row2(_pair(nsa_out_norm[0], 0))
    wo_a = _pair(w_out[0][:NSA_W], 0).astype(BF16)
    wo_b = w_out[0][NSA_W:].astype(BF16)
    wq_ca = ca_wq[0].astype(BF16)
    wo_ca = ca_wo[0].astype(BF16)
    w_mem = jnp.concatenate([ca_wk[0], ca_wv[0]], axis=1).astype(BF16)
    wg, wu, wd = ffn_w_gate[0].astype(BF16), ffn_w_up[0].astype(BF16), ffn_w_down[0].astype(BF16)
    g_mix, g_ca, g_ffn, g_fin = row2(norm_mix[0]), row2(norm_ca[0]), row2(norm_ffn[0]), row2(final_norm)
    g_hg = row2(hg_out_norm[0])

    tabs_p = _rope_tables(jnp.arange(T))
    tabs_s = _rope_tables(past + (jnp.arange(DB * TS) % TS))
    blk = np.arange(T) // SEL_BLK
    lanes2 = np.arange(2 * LANES)
    oh_np = ((lanes2[None, :] >= HEAD_DIM) & (lanes2[None, :] < 3 * HEAD_DIM)
             & ((lanes2[None, :] - HEAD_DIM) % HEAD_DIM == blk[:, None]))
    oh_p = jnp.asarray(oh_np.astype(np.float32))
    oh_s = jnp.zeros((DB * TS, 2 * LANES), F32)

    def tail(x, o_nsa, o_hg, mk, mv, nb, tm, tiled):
        rows = x.shape[0]
        x1, qca = _mixout(x, o_nsa, o_hg, gn_nsa, wo_a, wo_b, g_ca, wq_ca, tm)
        per = rows // nb
        tq = min(per, ROW_TILE)
        if per % 16:
            tq = -(-per // 16) * 16
            qca3 = _pad_rows(qca.reshape(nb, per, D_MODEL), tq)
            oca = _cross_attn(qca3, mk, mv, tq, ml, tiled)[:, :per].reshape(rows, D_MODEL)
            x2 = _mm_res(oca, wo_ca, x1, tm)
        else:
            x2 = _cross_attn(qca.reshape(nb, per, D_MODEL), mk, mv, tq, ml, tiled, wo_ca,
                             x1.reshape(nb, per, D_MODEL)).reshape(rows, D_MODEL)
        return _ffn(x2, g_ffn, wg, wu, wd, g_fin, min(rows, FFN_ROW_TILE), wg.shape[1] // 2)

    M = B * T
    (q, qr, kcp, vcp, _, _, _, _, ksa, kwb, vwb, hm, gt, kct, vct, kst, vst, kwt, vwt) = _proj_in(
        x_prompt.reshape(M, D_MODEL), g_mix, w_in_p, tabs_p, oh_p, ROW_TILE, B)
    r3 = lambda a: a.reshape(B, T, -1)
    cmp_w = (pek, pev, w1k, w1v, w2k, w2v)
    kc_p, vc_p = _compress_rows(r3(kcp), r3(vcp), cmp_w)
    o_nsa = _nsa_prompt(r3(q), r3(qr), r3(gt), kc_p, vc_p, r3(ksa), vst, r3(kwb), r3(vwb), NSA_Q_TILE, NSA_KEY_TILE)
    o_hg, s_p = _hgrn(r3(hm), lb, g_hg, jnp.zeros((B, HG_HEADS, HG_DK, HG_DV), F32), HG_CHUNK, T)
    mkv = _norm_mm(mem_prompt.reshape(B * ml, D_MODEL), row2(norm_mem[0]), w_mem, ROW_TILE)
    mk_p = mkv[:, :D_MODEL].reshape(B, ml, D_MODEL)
    mv_p = mkv[:, D_MODEL:].reshape(B, ml, D_MODEL)
    y_p = tail(x_prompt.reshape(M, D_MODEL), o_nsa.reshape(M, NSA_W), o_hg.reshape(M, HG_W), mk_p, mv_p, B, ROW_TILE, False)

    kv5 = lambda a, n: a.reshape(1, -1, n, NSA_KV, HEAD_DIM)
    wbp = min(WINDOW, T)
    nat = lambda a: jnp.transpose(a.reshape(B, NSA_KV, HEAD_DIM, -1), (0, 3, 1, 2))[None]
    outs_p = (nat(kct), nat(vct), nat(kst), nat(vst),
              nat(kwt[:, :, T - wbp:]), nat(vwt[:, :, T - wbp:]), s_p[None],
              mk_p.reshape(1, B, ml, CA_HEADS, CA_HD), mv_p.reshape(1, B, ml, CA_HEADS, CA_HD))

    MS = DB * TS
    (q, qr, kcs, vcs, kss, vss, kws, vws, _, _, _, hm, gt) = _proj_in(
        x_sample.reshape(MS, D_MODEL), g_mix, w_in_p, tabs_s, oh_s, MS, 1)[:13]
    pt_flat = page_table.reshape(-1).astype(I32)
    pool_t = lambda a: jnp.transpose(a[0], (0, 2, 3, 1)).reshape(n_pool, LANES, PAGE_SIZE)
    kc_s, vc_s = _compress_paged(pt_flat, pool_t(cache_cmp_k), pool_t(cache_cmp_v), cmp_w, DB, n_pages)
    tp, tn = SAMPLE_Q_PAD, SAMPLE_KEY_PAD
    s3 = lambda a, n: _pad_rows(a.reshape(DB, TS, -1), n)
    key_blk = np.arange(past) // SEL_BLK
    oh_keys = jnp.asarray((np.arange(LANES)[:, None] == key_blk[None, :]).astype(np.float32)).astype(BF16)
    nseg = past // CMP_STRIDE
    gmat = jnp.asarray((np.arange(nseg)[:, None] // (SEL_BLK // CMP_STRIDE)
                        == np.arange(LANES)[None, :]).astype(np.float32))
    o_nsa_s, nwk, nwv = _nsa_sample(
        pt_flat, s3(q, tp), s3(qr, tp), s3(gt, tp), kc_s, vc_s, s3(kss, tn), s3(vss, tn), s3(kws, tn), s3(vws, tn),
        cache_win_k[0].reshape(DB, wb, LANES), cache_win_v[0].reshape(DB, wb, LANES), oh_keys, gmat,
        pool_t(cache_sel_k), pool_t(cache_sel_v), n_pages, past, TS)
    o_hg_s, s_s = _hgrn(s3(hm, SAMPLE_Q_PAD), lb, g_hg, state_hgrn[0].astype(F32), SAMPLE_Q_PAD, TS)
    nchunk = CA_HD // LANES
    mem_t = lambda a: jnp.transpose(a[0].reshape(DB, ml, CA_HEADS, nchunk, LANES), (0, 1, 3, 2, 4)).reshape(
        DB, ml * nchunk * CA_HEADS, LANES)
    y_s = tail(x_sample.reshape(MS, D_MODEL), o_nsa_s[:, :TS].reshape(MS, NSA_W), o_hg_s[:, :TS].reshape(MS, HG_W),
               mem_t(cache_mem_k), mem_t(cache_mem_v), DB, MS, True)

    outs_s = (kv5(kcs, TS), kv5(vcs, TS), kv5(kss, TS), kv5(vss, TS),
              nwk.reshape(1, DB, wb, NSA_KV, HEAD_DIM), nwv.reshape(1, DB, wb, NSA_KV, HEAD_DIM), s_s[None])
    return (y_p.reshape(B, T, D_MODEL), y_s.reshape(DB, TS, D_MODEL)) + outs_p + outs_s
```

```python
import functools

import numpy as np
import jax
import jax.numpy as jnp
from jax import lax
from jax.experimental import pallas as pl
from jax.experimental.pallas import tpu as pltpu

F32 = jnp.float32
BF16 = jnp.bfloat16
I32 = jnp.int32

D_MODEL = 1024
NSA_HEADS = 8
NSA_KV = 2
HEAD_DIM = 64
NSA_GROUP = NSA_HEADS // NSA_KV
NSA_W = NSA_HEADS * HEAD_DIM
CMP_STRIDE = 16
CMP_LEN = 32
CMP_HID = 128
SEL_BLK = 64
SEL_TOPK = 16
WINDOW = 512
ROPE_THETA = 500000.0
ROPE_DIM = HEAD_DIM // 4
HG_HEADS = 4
HG_DK = 128
HG_DV = 128
HG_W = HG_HEADS * HG_DV
CA_HEADS = 4
CA_HD = D_MODEL // CA_HEADS
PAGE_SIZE = 128
EPS = 1e-6
FORCE_SCORE = 1e4
NEG = -1e30
MASK_BIAS = -1e9
LANES = 128
VMEM_LIMIT = 56 * 1024 * 1024
ROW_TILE = 256
FFN_ROW_TILE = 512
NSA_Q_TILE = 128
NSA_KEY_TILE = 512
RANK_UNROLL = 8
PAGE_UNROLL = 8
HG_CHUNK = 128
SAMPLE_Q_PAD = 8
SAMPLE_KEY_PAD = 16

C_Q = 0
C_KC, C_VC, C_KS, C_VS, C_KW, C_VW = 512, 640, 768, 896, 1024, 1152
C_AUG = 1280
C_HG = 1536
C_GT = 3584
N_PROJ = 3712


def _cparams(sem):
    return pltpu.CompilerParams(dimension_semantics=sem, vmem_limit_bytes=VMEM_LIMIT)


def _nt(a, b):
    return lax.dot_general(a, b, (((1,), (1,)), ((), ())), preferred_element_type=F32)


def _tn(a, b):
    return lax.dot_general(a, b, (((0,), (0,)), ((), ())), preferred_element_type=F32)


def _mm(a, b):
    return jnp.dot(a, b, preferred_element_type=F32)


def _rms(x, g):
    return x * lax.rsqrt(jnp.mean(x * x, axis=-1, keepdims=True) + EPS) * g


def _masked_softmax(s, mask, axis):
    s = jnp.where(mask, s, NEG)
    m = jnp.max(s, axis=axis, keepdims=True)
    e = jnp.where(mask, jnp.exp(s - m), 0.0)
    return e / jnp.maximum(jnp.sum(e, axis=axis, keepdims=True), 1e-20)


def _log2(n):
    l = int(n).bit_length() - 1
    assert (1 << l) == n, n
    return l


def _proj_in_kernel(x_ref, g_ref, w_ref, c_ref, sa_ref, sb_ref, oh_ref,
                    q_ref, qr_ref, kc_ref, vc_ref, ks_ref, vs_ref, kw_ref, vw_ref,
                    ksa_ref, kwb_ref, vwb_ref, hm_ref, gt_ref,
                    kct_ref, vct_ref, kst_ref, vst_ref, kwt_ref, vwt_ref):
    xn = _rms(x_ref[...], g_ref[...]).astype(BF16)
    c, sa, sb = c_ref[...], sa_ref[...], sb_ref[...]

    def mm(lo, hi):
        return _mm(xn, w_ref[:, lo:hi])

    def rope(y):
        n = y.shape[1] // LANES
        cc, aa, bb = (jnp.concatenate([t] * n, axis=1) if n > 1 else t for t in (c, sa, sb))
        w = y.shape[1]
        return y * cc + pltpu.roll(y, w - ROPE_DIM // 2, 1) * aa + pltpu.roll(y, ROPE_DIM // 2, 1) * bb

    yq = mm(C_Q, C_Q + NSA_W) * (HEAD_DIM ** -0.5)
    q_ref[...] = yq.astype(BF16)
    qr_ref[...] = rope(yq).astype(BF16)
    kw = vw = None
    for col, roped, ref, tref in ((C_KC, False, kc_ref, kct_ref), (C_VC, False, vc_ref, vct_ref),
                                  (C_KS, True, ks_ref, kst_ref), (C_VS, False, vs_ref, vst_ref),
                                  (C_KW, True, kw_ref, kwt_ref), (C_VW, False, vw_ref, vwt_ref)):
        y = mm(col, col + LANES)
        y = rope(y) if roped else y
        ref[...] = y
        tref[0] = y.T
        kw = y if col == C_KW else kw
        vw = y if col == C_VW else vw
    kwb_ref[...] = kw.astype(BF16)
    vwb_ref[...] = vw.astype(BF16)
    ksa_ref[...] = (rope(mm(C_AUG, C_AUG + 2 * LANES)) + oh_ref[...]).astype(BF16)
    hm_ref[...] = mm(C_HG, C_HG + 4 * HG_W)
    gt_ref[...] = mm(C_GT, C_GT + LANES)


def _proj_in(x2d, g, w, tabs, oh, tm, nb):
    m = x2d.shape[0]
    per = m // nb
    nt_tab = tabs[0].shape[0] // tm
    nt_seq = per // tm
    row = lambda i: (i, 0)
    tab = lambda i: (i % nt_tab, 0)
    const = lambda i: (0, 0)
    widths = [(NSA_W, BF16), (NSA_W, BF16)] + [(LANES, F32)] * 6 + [(2 * LANES, BF16)] + [(LANES, BF16)] * 2 \
        + [(4 * HG_W, F32), (LANES, F32)]
    return pl.pallas_call(
        _proj_in_kernel,
        grid=(m // tm,),
        in_specs=[pl.BlockSpec((tm, D_MODEL), row), pl.BlockSpec((1, D_MODEL), const),
                  pl.BlockSpec((D_MODEL, N_PROJ), const),
                  pl.BlockSpec((tm, LANES), tab), pl.BlockSpec((tm, LANES), tab), pl.BlockSpec((tm, LANES), tab),
                  pl.BlockSpec((tm, 2 * LANES), tab)],
        out_specs=[pl.BlockSpec((tm, wd), row) for wd, _ in widths]
        + [pl.BlockSpec((1, LANES, tm), lambda i: (i // nt_seq, 0, i % nt_seq))] * 6,
        out_shape=[jax.ShapeDtypeStruct((m, wd), dt) for wd, dt in widths]
        + [jax.ShapeDtypeStruct((nb, LANES, per), F32)] * 6,
        compiler_params=_cparams(("parallel",)),
        name="proj_in",
    )(x2d, g, w, *tabs, oh)


def _page_copy(pool_hbm, page, buf, slot, p, sem):
    cols = pl.ds(pl.multiple_of(p * PAGE_SIZE, PAGE_SIZE), PAGE_SIZE)
    return pltpu.make_async_copy(pool_hbm.at[page], buf.at[slot, :, cols], sem)


def _gather_start(pt_ref, pools, bufs, sems, b, slot, n_pages):
    def body(p, carry):
        page = pt_ref[b * n_pages + p]
        for pool, buf, sem in zip(pools, bufs, sems):
            _page_copy(pool, page, buf, slot, p, sem.at[slot]).start()
        return carry
    lax.fori_loop(0, n_pages, body, 0, unroll=PAGE_UNROLL)


def _gather_wait(pools, bufs, sems, slot, n_pages):
    def body(p, carry):
        for pool, buf, sem in zip(pools, bufs, sems):
            _page_copy(pool, 0, buf, slot, p, sem.at[slot]).wait()
        return carry
    lax.fori_loop(0, n_pages, body, 0, unroll=PAGE_UNROLL)


def _gather_step(pt_ref, pools, bufs, sems, n_pages):
    b = pl.program_id(0)
    nb = pl.num_programs(0)
    slot = b % 2

    @pl.when(b == 0)
    def _():
        _gather_start(pt_ref, pools, bufs, sems, 0, 0, n_pages)

    @pl.when(b + 1 < nb)
    def _():
        _gather_start(pt_ref, pools, bufs, sems, b + 1, 1 - slot, n_pages)

    _gather_wait(pools, bufs, sems, slot, n_pages)
    return slot


def _compress_core(tok, pe_ref, w1_ref, w2_ref, bias_scr, first, out, nseg):
    npair = CMP_STRIDE // 2
    nh = NSA_KV * CMP_HID

    @pl.when(first)
    def _():
        acc = _mm(pe_ref[:, 0:2 * LANES].astype(BF16), w1_ref[0])
        for jj in range(1, npair):
            acc = acc + _mm(pe_ref[:, jj * 2 * LANES:(jj + 1) * 2 * LANES].astype(BF16), w1_ref[jj])
        bias_scr[...] = acc

    acc = None
    for jj in range(npair):
        xp = jnp.concatenate([tok[pl.ds(2 * jj, nseg, stride=CMP_STRIDE), :],
                              tok[pl.ds(2 * jj + 1, nseg, stride=CMP_STRIDE), :]], axis=1).astype(BF16)
        part = _mm(xp, w1_ref[jj])
        acc = part if acc is None else acc + part
    ha = acc[:, 0:nh] + bias_scr[0:1, 0:nh]
    hb = acc[:, nh:2 * nh] + bias_scr[1:2, nh:2 * nh]
    h = jax.nn.gelu(ha + pltpu.roll(hb, nseg - 1, 0))
    row = lax.broadcasted_iota(I32, (nseg, 1), 0)
    h = jnp.where(row < nseg - 1, h, 0.0)
    out[...] = _mm(h.astype(BF16), w2_ref[...]).astype(BF16)


def _compress_rows_kernel(k_ref, v_ref, pek_ref, pev_ref, w1k_ref, w1v_ref, w2k_ref, w2v_ref,
                          kc_ref, vc_ref, bias_scr, *, nseg):
    first = pl.program_id(0) == 0
    for i, (tok, pe, w1, w2, out) in enumerate(((k_ref, pek_ref, w1k_ref, w2k_ref, kc_ref),
                                                (v_ref, pev_ref, w1v_ref, w2v_ref, vc_ref))):
        _compress_core(tok.at[0], pe, w1, w2, bias_scr.at[i], first, out.at[0], nseg)


def _compress_paged_kernel(pt_ref, poolk_hbm, poolv_hbm, pek_ref, pev_ref, w1k_ref, w1v_ref, w2k_ref, w2v_ref,
                           kc_ref, vc_ref, xk_buf, xv_buf, tokk_scr, tokv_scr, bias_scr, semk, semv, *, n_pages):
    slot = _gather_step(pt_ref, (poolk_hbm, poolv_hbm), (xk_buf, xv_buf), (semk, semv), n_pages)
    first = pl.program_id(0) == 0
    nseg = n_pages * (PAGE_SIZE // CMP_STRIDE)
    for i, (xbuf, tok, pe, w1, w2, out) in enumerate(((xk_buf, tokk_scr, pek_ref, w1k_ref, w2k_ref, kc_ref),
                                                      (xv_buf, tokv_scr, pev_ref, w1v_ref, w2v_ref, vc_ref))):
        tok[...] = xbuf[slot].T
        _compress_core(tok, pe, w1, w2, bias_scr.at[i], first, out.at[0], nseg)


def _compress_weight_specs(const2, const3):
    npair = CMP_STRIDE // 2
    return [pl.BlockSpec((8, CMP_STRIDE * LANES), const2), pl.BlockSpec((8, CMP_STRIDE * LANES), const2),
            pl.BlockSpec((npair, 2 * LANES, 4 * CMP_HID), const3), pl.BlockSpec((npair, 2 * LANES, 4 * CMP_HID), const3),
            pl.BlockSpec((NSA_KV * CMP_HID, LANES), const2), pl.BlockSpec((NSA_KV * CMP_HID, LANES), const2)]


def _compress_rows(k_rows, v_rows, weights):
    nb, ntok = k_rows.shape[:2]
    nseg = ntok // CMP_STRIDE
    per_b = lambda b: (b, 0, 0)
    return pl.pallas_call(
        functools.partial(_compress_rows_kernel, nseg=nseg),
        grid=(nb,),
        in_specs=[pl.BlockSpec((1, ntok, LANES), per_b), pl.BlockSpec((1, ntok, LANES), per_b)]
        + _compress_weight_specs(lambda b: (0, 0), lambda b: (0, 0, 0)),
        out_specs=[pl.BlockSpec((1, nseg, LANES), per_b)] * 2,
        out_shape=[jax.ShapeDtypeStruct((nb, nseg, LANES), BF16)] * 2,
        scratch_shapes=[pltpu.VMEM((2, 8, 4 * CMP_HID), F32)],
        compiler_params=_cparams(("arbitrary",)),
        name="compress_rows",
    )(k_rows, v_rows, *weights)


def _compress_paged(page_table_flat, pool_k, pool_v, weights, n_batch, n_pages):
    past = n_pages * PAGE_SIZE
    nseg = past // CMP_STRIDE
    gs = pltpu.PrefetchScalarGridSpec(
        num_scalar_prefetch=1,
        grid=(n_batch,),
        in_specs=[pl.BlockSpec(memory_space=pl.ANY), pl.BlockSpec(memory_space=pl.ANY)]
        + _compress_weight_specs(lambda b, pt: (0, 0), lambda b, pt: (0, 0, 0)),
        out_specs=[pl.BlockSpec((1, nseg, LANES), lambda b, pt: (b, 0, 0))] * 2,
        scratch_shapes=[pltpu.VMEM((2, LANES, past), F32), pltpu.VMEM((2, LANES, past), F32),
                        pltpu.VMEM((past, LANES), F32), pltpu.VMEM((past, LANES), F32),
                        pltpu.VMEM((2, 8, 4 * CMP_HID), F32),
                        pltpu.SemaphoreType.DMA((2,)), pltpu.SemaphoreType.DMA((2,))],
    )
    return pl.pallas_call(
        functools.partial(_compress_paged_kernel, n_pages=n_pages),
        grid_spec=gs,
        out_shape=[jax.ShapeDtypeStruct((n_batch, nseg, LANES), BF16)] * 2,
        compiler_params=_cparams(("arbitrary",)),
        name="compress_paged",
    )(page_table_flat, pool_k, pool_v, *weights)


def _nsa_prompt_kernel(q_ref, qr_ref, gt_ref, kc_ref, vc_ref, ksa_ref, vst_ref, kw_ref, vw_ref, o_ref,
                       psum_scr, sc_scr, *, tq, tk, ncp, nsel, topk):
    i = pl.program_id(1)
    s0 = i * tq
    rows = NSA_GROUP * tq
    hk = tk // 2
    lane = lax.broadcasted_iota(I32, (tq, LANES), 1)
    upper = lane >= HEAD_DIM
    upper_v = lax.broadcasted_iota(I32, (LANES, hk), 0) >= HEAD_DIM
    gsig = jax.nn.sigmoid(gt_ref[0])
    kc = kc_ref[0]
    vc = vc_ref[0]
    t_rows = s0 + (lax.broadcasted_iota(I32, (rows, 1), 0) & (tq - 1))
    t_cols = s0 + (lax.broadcasted_iota(I32, (1, rows), 1) & (tq - 1))
    o_c, o_w, o_st = [], [], []
    for g in range(NSA_KV):
        mine = upper if g == 1 else jnp.logical_not(upper)

        def stack(ref, fill):
            return jnp.concatenate(
                [jnp.where(mine, ref[0, :, r * LANES:(r + 1) * LANES], fill) for r in range(NSA_GROUP)], axis=0)

        cend = lax.broadcasted_iota(I32, (rows, ncp), 1) * CMP_STRIDE + (CMP_LEN - 1)
        p_c = _masked_softmax(_nt(stack(q_ref, 0), kc), cend <= t_rows, 1)
        o_c.append(_mm(p_c.astype(BF16), vc))
        psum = p_c[0:tq]
        for r in range(1, NSA_GROUP):
            psum = psum + p_c[r * tq:(r + 1) * tq]
        psum_scr[...] = psum.T
        cps = SEL_BLK // CMP_STRIDE
        imp = psum_scr[pl.ds(0, nsel, stride=cps), :]
        for u in range(1, cps):
            imp = imp + psum_scr[pl.ds(u, nsel, stride=cps), :]
        j = lax.broadcasted_iota(I32, (nsel, tq), 0)
        cur = (s0 + lax.broadcasted_iota(I32, (nsel, tq), 1)) >> _log2(SEL_BLK)
        avail = j <= cur
        forced = (j == 0) | (j == cur) | (j == cur - 1)
        score = jnp.where(avail, jnp.where(forced, FORCE_SCORE, imp), -FORCE_SCORE)
        sc_scr[...] = score

        def rank_body(grp, cnt):
            for u in range(RANK_UNROLL):
                jp = grp * RANK_UNROLL + u
                other = sc_scr[pl.ds(jp, 1), :]
                beats = (other > score) | ((other == score) & (jp < j))
                cnt = cnt + beats.astype(I32)
            return cnt

        n_live = ((s0 + tq - 1) >> _log2(SEL_BLK)) + 1
        n_grp = jnp.minimum((n_live + RANK_UNROLL - 1) // RANK_UNROLL, nsel // RANK_UNROLL)
        cnt = lax.fori_loop(0, n_grp, rank_body, jnp.zeros((nsel, tq), I32))
        bias_t = jnp.where((cnt < topk) & avail, 0.0, MASK_BIAS)
        if nsel < HEAD_DIM:
            bias_t = jnp.concatenate([bias_t, jnp.full((HEAD_DIM - nsel, tq), MASK_BIAS, F32)], axis=0)
        bias = jnp.concatenate([bias_t, bias_t], axis=0).T.astype(BF16)
        qs = stack(qr_ref, bias)
        glanes = slice(g * LANES, (g + 1) * LANES)
        mine_v = upper_v if g == 1 else jnp.logical_not(upper_v)

        def scores(k0):
            return _nt(ksa_ref[0, pl.ds(k0, tk), glanes], qs)

        def col_max(s):
            return tuple(jnp.max(s[c * hk:(c + 1) * hk], axis=0, keepdims=True) for c in range(2))

        def update(s, mx, k0, stats):
            new = []
            for c, (m, acc) in enumerate(stats):
                v_one = jnp.where(mine_v, vst_ref[0, :, pl.ds(k0 + c * hk, hk)], 1.0).astype(BF16)
                m_new = jnp.maximum(m, mx[c])
                p = jnp.exp((s[c * hk:(c + 1) * hk] - m_new).astype(BF16))
                new.append((m_new, jnp.exp(m - m_new) * acc + _mm(v_one, p)))
            return tuple(new)

        def sel_body(kk, carry):
            s_cur, mx_cur, stats = carry
            k0 = pl.multiple_of(kk * tk, tk)
            s_next = scores(k0 + tk)
            stats = update(s_cur, mx_cur, k0, stats)
            return s_next, col_max(s_next), stats

        n_full = s0 // tk
        stats0 = tuple((jnp.full((1, rows), NEG, F32), jnp.zeros((LANES, rows), F32)) for _ in range(2))
        s_first = scores(0)
        s_last, _, stats = lax.fori_loop(0, n_full, sel_body, (s_first, col_max(s_first), stats0))
        k_last = pl.multiple_of(n_full * tk, tk)
        causal = k_last + lax.broadcasted_iota(I32, (tk, rows), 0) <= t_cols
        s_last = jnp.where(causal, s_last, NEG)
        (m_a, acc_a), (m_b, acc_b) = update(s_last, col_max(s_last), k_last, stats)
        m_s = jnp.maximum(m_a, m_b)
        acc_s = jnp.exp(m_a - m_s) * acc_a + jnp.exp(m_b - m_s) * acc_b
        num = acc_s[g * HEAD_DIM:(g + 1) * HEAD_DIM]
        den = acc_s[(1 - g) * HEAD_DIM:(2 - g) * HEAD_DIM]
        o_st.append(num / jnp.maximum(den, 1e-20))
        wl = WINDOW + tq
        ws = pl.multiple_of(jnp.maximum(s0 - WINDOW, 0), tq)
        kpos = ws + lax.broadcasted_iota(I32, (rows, wl), 1)
        wmask = (kpos <= t_rows) & (kpos > t_rows - WINDOW)
        p_w = _masked_softmax(_nt(stack(qr_ref, 0), kw_ref[0, pl.ds(ws, wl), :]), wmask, 1)
        o_w.append(_mm(p_w.astype(BF16), vw_ref[0, pl.ds(ws, wl), :]))
    out_slots = []
    for r in range(NSA_GROUP):
        rs = slice(r * tq, (r + 1) * tq)
        o_s = jnp.concatenate([o_st[0][:, rs], o_st[1][:, rs]], axis=0).T
        res = []
        for g in range(NSA_KV):
            h = g * NSA_GROUP + r
            res.append(gsig[:, 3 * h:3 * h + 1] * o_c[g][rs] + gsig[:, 3 * h + 1:3 * h + 2] * o_s
                       + gsig[:, 3 * h + 2:3 * h + 3] * o_w[g][rs])
        out_slots.append(jnp.where(upper, res[1], res[0]))
    o_ref[0] = jnp.concatenate(out_slots, axis=1)


def _nsa_prompt(q, qr, gt, kc, vc, ksa, vst, kwb, vwb, tq, tk):
    b, t = q.shape[:2]
    ncp = t // CMP_STRIDE
    nsel = t // SEL_BLK
    assert t % tk == 0 and tk % tq == 0 and t >= WINDOW + tq and nsel <= HEAD_DIM and WINDOW % tq == 0
    assert nsel % RANK_UNROLL == 0
    blk = lambda bb, i: (bb, i, 0)
    whole = lambda bb, i: (bb, 0, 0)
    return pl.pallas_call(
        functools.partial(_nsa_prompt_kernel, tq=tq, tk=tk, ncp=ncp, nsel=nsel, topk=min(SEL_TOPK, nsel)),
        grid=(b, t // tq),
        in_specs=[pl.BlockSpec((1, tq, NSA_W), blk), pl.BlockSpec((1, tq, NSA_W), blk),
                  pl.BlockSpec((1, tq, LANES), blk),
                  pl.BlockSpec((1, ncp, LANES), whole), pl.BlockSpec((1, ncp, LANES), whole),
                  pl.BlockSpec((1, t, 2 * LANES), whole), pl.BlockSpec((1, LANES, t), whole),
                  pl.BlockSpec((1, t, LANES), whole), pl.BlockSpec((1, t, LANES), whole)],
        out_specs=pl.BlockSpec((1, tq, NSA_W), blk),
        out_shape=jax.ShapeDtypeStruct((b, t, NSA_W), F32),
        scratch_shapes=[pltpu.VMEM((ncp, tq), F32), pltpu.VMEM((nsel, tq), F32)],
        compiler_params=_cparams(("parallel", "parallel")),
        name="nsa_prompt",
    )(q, qr, gt, kc, vc, ksa, vst, kwb, vwb)


def _nsa_sample_kernel(pt_ref, q_ref, qr_ref, gt_ref, kc_ref, vc_ref, ksn_ref, vsn_ref, kwn_ref, vwn_ref,
                       wink_ref, winv_ref, oht_ref, gmat_ref, poolk_hbm, poolv_hbm,
                       o_ref, nwk_ref, nwv_ref, kbuf, vbuf, semk, semv,
                       *, n_pages, past, ts, tp, tn, wb, topk_past):
    slot = _gather_step(pt_ref, (poolk_hbm, poolv_hbm), (kbuf, vbuf), (semk, semv), n_pages)
    nseg = n_pages * (PAGE_SIZE // CMP_STRIDE)
    npb = past // SEL_BLK
    rows = NSA_HEADS * tp
    lane = lax.broadcasted_iota(I32, (tp, LANES), 1)
    upper = lane >= HEAD_DIM

    def stack(ref):
        parts = []
        for g in range(NSA_KV):
            mine = upper if g == 1 else jnp.logical_not(upper)
            parts += [jnp.where(mine, ref[0, :, r * LANES:(r + 1) * LANES], 0) for r in range(NSA_GROUP)]
        return jnp.concatenate(parts, axis=0)

    t_rows = lax.broadcasted_iota(I32, (rows, 1), 0) & (tp - 1)
    qc = stack(q_ref)
    cend = lax.broadcasted_iota(I32, (rows, nseg), 1) * CMP_STRIDE + (CMP_LEN - 1)
    p_c = _masked_softmax(_nt(qc, kc_ref[0]), cend <= past + t_rows, 1)
    o_c = _mm(p_c.astype(BF16), vc_ref[0])
    impn = []
    for g in range(NSA_KV):
        acc = p_c[(g * NSA_GROUP) * tp:(g * NSA_GROUP + 1) * tp]
        for r in range(1, NSA_GROUP):
            acc = acc + p_c[(g * NSA_GROUP + r) * tp:(g * NSA_GROUP + r + 1) * tp]
        impn.append(acc)
    impn = jnp.concatenate(impn, axis=0)
    imp = jnp.dot(impn, gmat_ref[...], precision=lax.Precision.HIGHEST, preferred_element_type=F32)
    j = lax.broadcasted_iota(I32, (NSA_KV * tp, LANES), 1)
    avail = j < npb
    forced = (j == 0) | (j == npb - 1)
    score = jnp.where(avail, jnp.where(forced, FORCE_SCORE, imp), -FORCE_SCORE)
    cnt = jnp.zeros((NSA_KV * tp, LANES), I32)
    for jp in range(npb):
        other = score[:, jp:jp + 1]
        beats = (other > score) | ((other == score) & (jp < j))
        cnt = cnt + beats.astype(I32)
    bias = jnp.where((cnt < topk_past) & avail, 0.0, MASK_BIAS).astype(BF16)
    bias = jnp.concatenate([bias[0:tp]] * NSA_GROUP + [bias[tp:2 * tp]] * NSA_GROUP, axis=0)
    qs = stack(qr_ref)
    key_t = lax.broadcasted_iota(I32, (rows, tn), 1)
    new_mask = (key_t <= t_rows) & (key_t < ts)
    s_p = _mm(qs, kbuf[slot].astype(BF16)) + _mm(bias, oht_ref[...])
    s_n = jnp.where(new_mask, _nt(qs, ksn_ref[0].astype(BF16)), NEG)
    m_s = jnp.maximum(jnp.max(s_p, axis=1, keepdims=True), jnp.max(s_n, axis=1, keepdims=True))
    e_p = jnp.exp(s_p - m_s)
    e_n = jnp.where(new_mask, jnp.exp(s_n - m_s), 0.0)
    l_s = jnp.sum(e_p, axis=1, keepdims=True) + jnp.sum(e_n, axis=1, keepdims=True)
    o_s = (_nt(e_p.astype(BF16), vbuf[slot].astype(BF16)) + _mm(e_n.astype(BF16), vsn_ref[0].astype(BF16))) \
        / jnp.maximum(l_s, 1e-20)
    wk = wink_ref[0]
    wv = winv_ref[0]
    iw = lax.broadcasted_iota(I32, (rows, wb), 1)
    wmask = (iw > t_rows + (wb - WINDOW)) & (iw >= wb - past)
    s_w = jnp.where(wmask, _nt(qs, wk.astype(BF16)), NEG)
    s_wn = jnp.where(new_mask, _nt(qs, kwn_ref[0].astype(BF16)), NEG)
    m_w = jnp.maximum(jnp.max(s_w, axis=1, keepdims=True), jnp.max(s_wn, axis=1, keepdims=True))
    e_w = jnp.where(wmask, jnp.exp(s_w - m_w), 0.0)
    e_wn = jnp.where(new_mask, jnp.exp(s_wn - m_w), 0.0)
    inv = 1.0 / jnp.maximum(jnp.sum(e_w, axis=1, keepdims=True) + jnp.sum(e_wn, axis=1, keepdims=True), 1e-20)
    o_w = _mm((e_w * inv).astype(BF16), wv.astype(BF16)) + _mm((e_wn * inv).astype(BF16), vwn_ref[0].astype(BF16))
    gsig = jax.nn.sigmoid(gt_ref[0])
    slots = []
    for r in range(NSA_GROUP):
        per_g = []
        for g in range(NSA_KV):
            h = g * NSA_GROUP + r
            rs = slice(h * tp, (h + 1) * tp)
            per_g.append(gsig[:, 3 * h:3 * h + 1] * o_c[rs] + gsig[:, 3 * h + 1:3 * h + 2] * o_s[rs]
                         + gsig[:, 3 * h + 2:3 * h + 3] * o_w[rs])
        slots.append(jnp.where(upper, per_g[1], per_g[0]))
    o_ref[0] = jnp.concatenate(slots, axis=1)
    sub = lax.broadcasted_iota(I32, (8, LANES), 0)
    for win, new_ref, out in ((wk, kwn_ref, nwk_ref), (wv, vwn_ref, nwv_ref)):
        rolled = pltpu.roll(win, wb - ts, 0)
        shifted = pltpu.roll(new_ref[0, 0:8, :], 8 - ts, 0)
        out[0, 0:wb - 8, :] = rolled[0:wb - 8]
        out[0, wb - 8:wb, :] = jnp.where(sub >= 8 - ts, shifted, rolled[wb - 8:wb])


def _nsa_sample(page_table_flat, q, qr, gt, kc, vc, ksn, vsn, kwn, vwn, win_k, win_v, oh, gmat, pool_k, pool_v,
                n_pages, past, ts):
    db, tp = q.shape[:2]
    tn = ksn.shape[1]
    wb = win_k.shape[1]
    nseg = n_pages * (PAGE_SIZE // CMP_STRIDE)
    npb = past // SEL_BLK
    assert ts <= 8 and npb <= LANES and wb >= 8 and past % PAGE_SIZE == 0
    per_b = lambda b, pt: (b, 0, 0)
    const2 = lambda b, pt: (0, 0)
    gs = pltpu.PrefetchScalarGridSpec(
        num_scalar_prefetch=1,
        grid=(db,),
        in_specs=[pl.BlockSpec((1, tp, NSA_W), per_b), pl.BlockSpec((1, tp, NSA_W), per_b),
                  pl.BlockSpec((1, tp, LANES), per_b),
                  pl.BlockSpec((1, nseg, LANES), per_b), pl.BlockSpec((1, nseg, LANES), per_b),
                  pl.BlockSpec((1, tn, LANES), per_b), pl.BlockSpec((1, tn, LANES), per_b),
                  pl.BlockSpec((1, tn, LANES), per_b), pl.BlockSpec((1, tn, LANES), per_b),
                  pl.BlockSpec((1, wb, LANES), per_b), pl.BlockSpec((1, wb, LANES), per_b),
                  pl.BlockSpec((LANES, past), const2), pl.BlockSpec((nseg, LANES), const2),
                  pl.BlockSpec(memory_space=pl.ANY), pl.BlockSpec(memory_space=pl.ANY)],
        out_specs=[pl.BlockSpec((1, tp, NSA_W), per_b), pl.BlockSpec((1, wb, LANES), per_b),
                   pl.BlockSpec((1, wb, LANES), per_b)],
        scratch_shapes=[pltpu.VMEM((2, LANES, past), F32), pltpu.VMEM((2, LANES, past), F32),
                        pltpu.SemaphoreType.DMA((2,)), pltpu.SemaphoreType.DMA((2,))],
    )
    return pl.pallas_call(
        functools.partial(_nsa_sample_kernel, n_pages=n_pages, past=past, ts=ts, tp=tp, tn=tn, wb=wb,
                          topk_past=min(SEL_TOPK, npb + 1) - 1),
        grid_spec=gs,
        out_shape=[jax.ShapeDtypeStruct((db, tp, NSA_W), F32), jax.ShapeDtypeStruct((db, wb, LANES), F32),
                   jax.ShapeDtypeStruct((db, wb, LANES), F32)],
        compiler_params=_cparams(("arbitrary",)),
        name="nsa_sample",
    )(page_table_flat, q, qr, gt, kc, vc, ksn, vsn, kwn, vwn, win_k, win_v, oh, gmat, pool_k, pool_v)


def _hgrn_kernel(hm_ref, lb_ref, og_ref, s0_ref, o_ref, sout_ref, st_scr, *, chunk, t_real, levels):
    c = pl.program_id(1)
    nc = pl.num_programs(1)

    @pl.when(c == 0)
    def _():
        for hd in range(HG_HEADS):
            st_scr[hd] = s0_ref[0, hd].T

    row = lax.broadcasted_iota(I32, (chunk, 1), 0)
    valid = (c * chunk + row) < t_real
    row_c = lax.broadcasted_iota(I32, (chunk, chunk), 0)
    col_c = lax.broadcasted_iota(I32, (chunk, chunk), 1)
    outs = []
    for hd in range(HG_HEADS):
        sl = slice(hd * HG_DK, (hd + 1) * HG_DK)
        q = hm_ref[0, :, hd * HG_DK:(hd + 1) * HG_DK]
        z = hm_ref[0, :, HG_W + hd * HG_DK:HG_W + (hd + 1) * HG_DK]
        v = hm_ref[0, :, 2 * HG_W + hd * HG_DV:2 * HG_W + (hd + 1) * HG_DV]
        gate = hm_ref[0, :, 3 * HG_W + hd * HG_DV:3 * HG_W + (hd + 1) * HG_DV]
        lb = lb_ref[:, sl]
        logf = jnp.where(valid, jnp.log(lb + (1.0 - lb) * jax.nn.sigmoid(z)), 0.0)
        k = jnp.where(valid, (1.0 - lb) * jax.nn.sigmoid(-z), 0.0)
        b = logf
        step = 1
        while step < chunk:
            b = b + jnp.where(row >= step, pltpu.roll(b, step, 0), 0.0)
            step *= 2
        st = st_scr[hd]
        o = _nt((q * jnp.exp(b)).astype(BF16), st.astype(BF16))
        if levels:
            a_mat = jnp.zeros((chunk, chunk), F32)
            for h in levels:
                if 2 * h <= 8:
                    b3 = b.reshape(chunk // 8, 8, HG_DK)
                    m = None
                    for blk in range(8 // (2 * h)):
                        mid_row = blk * 2 * h + h - 1
                        cand = jnp.broadcast_to(b3[:, mid_row:mid_row + 1, :], b3.shape).reshape(chunk, HG_DK)
                        m = cand if m is None else jnp.where((row & 7) >= blk * 2 * h, cand, m)
                else:
                    m = jnp.concatenate(
                        [jnp.broadcast_to(b[blk * 2 * h + h - 1:blk * 2 * h + h, :], (2 * h, HG_DK))
                         for blk in range(chunk // (2 * h))], axis=0)
                second = ((row >> _log2(h)) & 1) == 1
                qs = jnp.where(second, q * jnp.exp(jnp.minimum(b - m, 0.0)), 0.0)
                ks = jnp.where(second, 0.0, k * jnp.exp(jnp.minimum(m - b, 0.0)))
                same = (row_c >> _log2(2 * h)) == (col_c >> _log2(2 * h))
                a_mat = a_mat + jnp.where(same, _nt(qs.astype(BF16), ks.astype(BF16)), 0.0)
            o = o + _mm(a_mat.astype(BF16), v.astype(BF16))
        dblk = min(levels) if levels else chunk
        for d in range(dblk):
            if d == 0:
                o = o + jnp.sum(q * k, axis=1, keepdims=True) * v
                continue
            kd, bd, vd = (pltpu.roll(x, d, 0) for x in (k, b, v))
            p = jnp.where((row & (dblk - 1)) >= d, q * kd * jnp.exp(jnp.minimum(b - bd, 0.0)), 0.0)
            o = o + jnp.sum(p, axis=1, keepdims=True) * vd
        bl = b[chunk - 1:chunk, :]
        st_scr[hd] = jnp.exp(bl) * st + _tn(v.astype(BF16), (k * jnp.exp(bl - b)).astype(BF16))
        outs.append(_rms(o, og_ref[:, sl]) * (gate * jax.nn.sigmoid(gate)))
    o_ref[0] = jnp.concatenate(outs, axis=1)

    @pl.when(c == nc - 1)
    def _():
        for hd in range(HG_HEADS):
            sout_ref[0, hd] = st_scr[hd].T


def _hgrn(hm, lb, og, s0, chunk, t_real):
    b, tpad = hm.shape[:2]
    assert chunk <= 8 or chunk % 16 == 0
    levels = tuple(h for h in (64, 32, 16, 8, 4, 2, 1) if 2 * h <= chunk and chunk >= 16)
    return pl.pallas_call(
        functools.partial(_hgrn_kernel, chunk=chunk, t_real=t_real, levels=levels),
        grid=(b, tpad // chunk),
        in_specs=[pl.BlockSpec((1, chunk, 4 * HG_W), lambda bb, c: (bb, c, 0)),
                  pl.BlockSpec((1, HG_W), lambda bb, c: (0, 0)), pl.BlockSpec((1, HG_W), lambda bb, c: (0, 0)),
                  pl.BlockSpec((1, HG_HEADS, HG_DK, HG_DV), lambda bb, c: (bb, 0, 0, 0))],
        out_specs=[pl.BlockSpec((1, chunk, HG_W), lambda bb, c: (bb, c, 0)),
                   pl.BlockSpec((1, HG_HEADS, HG_DK, HG_DV), lambda bb, c: (bb, 0, 0, 0))],
        out_shape=[jax.ShapeDtypeStruct((b, tpad, HG_W), F32),
                   jax.ShapeDtypeStruct((b, HG_HEADS, HG_DK, HG_DV), F32)],
        scratch_shapes=[pltpu.VMEM((HG_HEADS, HG_DV, HG_DK), F32)],
        compiler_params=_cparams(("parallel", "arbitrary")),
        name="hgrn",
    )(hm, lb, og, s0)


def _mixout_kernel(x_ref, on_ref, oh_ref, gn_ref, woa_ref, wob_ref, gca_ref, wq_ref, x1_ref, q_ref):
    a = _rms(on_ref[...], gn_ref[...]).astype(BF16)
    x1 = x_ref[...] + _mm(a, woa_ref[...]) + _mm(oh_ref[...].astype(BF16), wob_ref[...])
    x1_ref[...] = x1
    hn = _rms(x1, gca_ref[...]).astype(BF16)
    q_ref[...] = (_mm(hn, wq_ref[...]) * (CA_HD ** -0.5)).astype(BF16)


def _mixout(x, o_nsa, o_hg, gn, woa, wob, gca, wq, tm):
    m = x.shape[0]
    row = lambda i: (i, 0)
    const = lambda i: (0, 0)
    return pl.pallas_call(
        _mixout_kernel,
        grid=(m // tm,),
        in_specs=[pl.BlockSpec((tm, D_MODEL), row), pl.BlockSpec((tm, NSA_W), row), pl.BlockSpec((tm, HG_W), row),
                  pl.BlockSpec((1, NSA_W), const), pl.BlockSpec((NSA_W, D_MODEL), const),
                  pl.BlockSpec((HG_W, D_MODEL), const), pl.BlockSpec((1, D_MODEL), const),
                  pl.BlockSpec((D_MODEL, D_MODEL), const)],
        out_specs=[pl.BlockSpec((tm, D_MODEL), row), pl.BlockSpec((tm, D_MODEL), row)],
        out_shape=[jax.ShapeDtypeStruct((m, D_MODEL), F32), jax.ShapeDtypeStruct((m, D_MODEL), BF16)],
        compiler_params=_cparams(("parallel",)),
        name="mixout",
    )(x, o_nsa, o_hg, gn, woa, wob, gca, wq)


def _norm_mm_kernel(x_ref, g_ref, w_ref, o_ref):
    o_ref[...] = _mm(_rms(x_ref[...], g_ref[...]).astype(BF16), w_ref[...])


def _norm_mm(x, g, w, tm):
    m, n = x.shape[0], w.shape[1]
    return pl.pallas_call(
        _norm_mm_kernel,
        grid=(m // tm,),
        in_specs=[pl.BlockSpec((tm, D_MODEL), lambda i: (i, 0)), pl.BlockSpec((1, D_MODEL), lambda i: (0, 0)),
                  pl.BlockSpec((D_MODEL, n), lambda i: (0, 0))],
        out_specs=pl.BlockSpec((tm, n), lambda i: (i, 0)),
        out_shape=jax.ShapeDtypeStruct((m, n), F32),
        compiler_params=_cparams(("parallel",)),
        name="norm_mm",
    )(x, g, w)


def _ca_kernel(q_ref, mk_ref, mv_ref, *rest, ml, tiled):
    o_ref = rest[-1]
    nchunk = CA_HD // LANES

    def head(ref, hh):
        if tiled:
            return jnp.concatenate([ref[0, pl.ds(c * CA_HEADS + hh, ml, stride=nchunk * CA_HEADS), :]
                                    for c in range(nchunk)], axis=1)
        return ref[0, :, hh * CA_HD:(hh + 1) * CA_HD]

    outs = []
    for hh in range(CA_HEADS):
        s = _nt(q_ref[0, :, hh * CA_HD:(hh + 1) * CA_HD], head(mk_ref, hh).astype(BF16))
        e = jnp.exp(s - jnp.max(s, axis=1, keepdims=True))
        p = e / jnp.sum(e, axis=1, keepdims=True)
        outs.append(_mm(p.astype(BF16), head(mv_ref, hh).astype(BF16)))
    o = jnp.concatenate(outs, axis=1).astype(BF16)
    if len(rest) == 3:
        wo_ref, res_ref, _ = rest
        o_ref[0] = res_ref[0] + _mm(o, wo_ref[...])
    else:
        o_ref[0] = o


def _cross_attn(q, mk, mv, tm, ml, tiled, wo=None, res=None):
    b, t = q.shape[:2]
    mem_block = mk.shape[1:]
    rows = lambda bb, i: (bb, i, 0)
    in_specs = [pl.BlockSpec((1, tm, D_MODEL), rows),
                pl.BlockSpec((1,) + mem_block, lambda bb, i: (bb, 0, 0)),
                pl.BlockSpec((1,) + mem_block, lambda bb, i: (bb, 0, 0))]
    args = (q, mk, mv)
    if wo is not None:
        in_specs += [pl.BlockSpec((D_MODEL, D_MODEL), lambda bb, i: (0, 0)), pl.BlockSpec((1, tm, D_MODEL), rows)]
        args += (wo, res)
    return pl.pallas_call(
        functools.partial(_ca_kernel, ml=ml, tiled=tiled),
        grid=(b, t // tm),
        in_specs=in_specs,
        out_specs=pl.BlockSpec((1, tm, D_MODEL), rows),
        out_shape=jax.ShapeDtypeStruct((b, t, D_MODEL), BF16 if wo is None else F32),
        compiler_params=_cparams(("parallel", "parallel")),
        name="cross_attn",
    )(*args)


def _mm_res_kernel(a_ref, w_ref, r_ref, o_ref):
    o_ref[...] = r_ref[...] + _mm(a_ref[...], w_ref[...])


def _mm_res(a, w, res, tm):
    m, k = a.shape
    n = w.shape[1]
    return pl.pallas_call(
        _mm_res_kernel,
        grid=(m // tm,),
        in_specs=[pl.BlockSpec((tm, k), lambda i: (i, 0)), pl.BlockSpec((k, n), lambda i: (0, 0)),
                  pl.BlockSpec((tm, n), lambda i: (i, 0))],
        out_specs=pl.BlockSpec((tm, n), lambda i: (i, 0)),
        out_shape=jax.ShapeDtypeStruct((m, n), F32),
        compiler_params=_cparams(("parallel",)),
        name="mm_res",
    )(a, w, res)


def _ffn_kernel(x_ref, gn_ref, wg_ref, wu_ref, wd_ref, gf_ref, y_ref, h_scr, acc_scr):
    j = pl.program_id(1)

    @pl.when(j == 0)
    def _():
        h_scr[...] = _rms(x_ref[...], gn_ref[...]).astype(BF16)
        acc_scr[...] = jnp.zeros_like(acc_scr)

    h = h_scr[...]
    gate = _mm(h, wg_ref[...])
    act = (gate * jax.nn.sigmoid(gate)) * _mm(h, wu_ref[...])
    acc_scr[...] += _mm(act.astype(BF16), wd_ref[...])

    @pl.when(j == pl.num_programs(1) - 1)
    def _():
        y_ref[...] = _rms(x_ref[...] + acc_scr[...], gf_ref[...])


def _ffn(x, gn, wg, wu, wd, gf, tm, th):
    m = x.shape[0]
    hid = wg.shape[1]
    return pl.pallas_call(
        _ffn_kernel,
        grid=(m // tm, hid // th),
        in_specs=[pl.BlockSpec((tm, D_MODEL), lambda i, j: (i, 0)), pl.BlockSpec((1, D_MODEL), lambda i, j: (0, 0)),
                  pl.BlockSpec((D_MODEL, th), lambda i, j: (0, j)), pl.BlockSpec((D_MODEL, th), lambda i, j: (0, j)),
                  pl.BlockSpec((th, D_MODEL), lambda i, j: (j, 0)), pl.BlockSpec((1, D_MODEL), lambda i, j: (0, 0))],
        out_specs=pl.BlockSpec((tm, D_MODEL), lambda i, j: (i, 0)),
        out_shape=jax.ShapeDtypeStruct((m, D_MODEL), F32),
        scratch_shapes=[pltpu.VMEM((tm, D_MODEL), BF16), pltpu.VMEM((tm, D_MODEL), F32)],
        compiler_params=_cparams(("parallel", "arbitrary")),
        name="ffn",
    )(x, gn, wg, wu, wd, gf)


def _pair(a, axis):
    shp = a.shape
    a = a.reshape(shp[:axis] + (NSA_KV, NSA_GROUP, HEAD_DIM) + shp[axis + 1:])
    a = jnp.swapaxes(a, axis, axis + 1)
    return a.reshape(shp)


def _rope_tables(pos):
    half = ROPE_DIM // 2
    inv = ROPE_THETA ** (-jnp.arange(half, dtype=F32) / half)
    ang = pos.astype(F32)[:, None] * inv[None, :]
    cos, sin = jnp.cos(ang), jnp.sin(ang)
    l = np.arange(LANES) % HEAD_DIM
    idx = l % half
    c = jnp.where(l < ROPE_DIM, cos[:, idx], 1.0)
    sa = jnp.where(l < half, -sin[:, idx], 0.0)
    sb = jnp.where((l >= half) & (l < ROPE_DIM), sin[:, idx], 0.0)
    return c, sa, sb


def _prep_w_in(w):
    offs = np.cumsum([0, NSA_W, 3 * NSA_HEADS] + [NSA_KV * HEAD_DIM] * 6 + [HG_W] * 4)
    seg = lambda i: w[:, offs[i]:offs[i + 1]]
    wq = _pair(seg(0), 1)
    wks = seg(4)
    z = jnp.zeros((D_MODEL, HEAD_DIM), w.dtype)
    aug = jnp.concatenate([wks[:, :HEAD_DIM], z, z, wks[:, HEAD_DIM:]], axis=1)
    gates = jnp.pad(seg(1), ((0, 0), (0, LANES - 3 * NSA_HEADS)))
    out = jnp.concatenate([wq] + [seg(i) for i in range(2, 8)] + [aug] + [seg(i) for i in range(8, 12)] + [gates],
                          axis=1)
    assert out.shape[1] == N_PROJ
    return out.astype(BF16)


def _prep_compress(pe, w1, w2):
    npair = CMP_STRIDE // 2
    pe_big = jnp.broadcast_to(pe.reshape(2, CMP_STRIDE, 1, HEAD_DIM), (2, CMP_STRIDE, NSA_KV, HEAD_DIM))
    pe_big = jnp.pad(pe_big.reshape(2, CMP_STRIDE * LANES), ((0, 6), (0, 0)))
    w1r = w1.reshape(2, npair, 2, HEAD_DIM, CMP_HID)
    eye = jnp.eye(NSA_KV, dtype=w1.dtype)
    w1_big = jnp.einsum('cjldh,ge->jlgdceh', w1r, eye).reshape(npair, 2 * LANES, 4 * CMP_HID)
    w2_big = jnp.einsum('hd,ge->ghed', w2, eye).reshape(NSA_KV * CMP_HID, LANES)
    return pe_big, w1_big.astype(BF16), w2_big.astype(BF16)


def _pad_rows(a, n):
    return jnp.pad(a, ((0, 0), (0, n - a.shape[1]), (0, 0)))


def kernel(x_prompt, x_sample, cache_cmp_k, cache_cmp_v, cache_sel_k, cache_sel_v, cache_win_k, cache_win_v,
           state_hgrn, cache_mem_k, cache_mem_v, page_table, mem_prompt, norm_mix, w_in, cmp_pe_k, cmp_w1_k,
           cmp_w2_k, cmp_pe_v, cmp_w1_v, cmp_w2_v, nsa_out_norm, hg_lb_logits, hg_out_norm, w_out, norm_ca,
           norm_mem, ca_wq, ca_wk, ca_wv, ca_wo, norm_ffn, ffn_w_gate, ffn_w_up, ffn_w_down, final_norm):
    B, T = x_prompt.shape[:2]
    DB, TS = x_sample.shape[:2]
    n_pages = page_table.shape[1]
    past = n_pages * PAGE_SIZE
    n_pool = cache_cmp_k.shape[1]
    wb = cache_win_k.shape[2]
    ml = mem_prompt.shape[1]
    assert w_in.shape[0] == 1, "single layer"
    row2 = lambda a: a.reshape(1, -1)

    w_in_p = _prep_w_in(w_in[0])
    pek, w1k, w2k = _prep_compress(cmp_pe_k[0], cmp_w1_k[0], cmp_w2_k[0])
    pev, w1v, w2v = _prep_compress(cmp_pe_v[0], cmp_w1_v[0], cmp_w2_v[0])
    lb = jnp.cumsum(jax.nn.softmax(hg_lb_logits.astype(F32), axis=0), axis=0)[0].reshape(1, HG_W)
    gn_nsa = row2(_pair(nsa_out_norm[0], 0))
    wo_a = _pair(w_out[0][:NSA_W], 0).astype(BF16)
    wo_b = w_out[0][NSA_W:].astype(BF16)
    wq_ca = ca_wq[0].astype(BF16)
    wo_ca = ca_wo[0].astype(BF16)
    w_mem = jnp.concatenate([ca_wk[0], ca_wv[0]], axis=1).astype(BF16)
    wg, wu, wd = ffn_w_gate[0].astype(BF16), ffn_w_up[0].astype(BF16), ffn_w_down[0].astype(BF16)
    g_mix, g_ca, g_ffn, g_fin = row2(norm_mix[0]), row2(norm_ca[0]), row2(norm_ffn[0]), row2(final_norm)
    g_hg = row2(hg_out_norm[0])

    tabs_p = _rope_tables(jnp.arange(T))
    tabs_s = _rope_tables(past + (jnp.arange(DB * TS) % TS))
    blk = np.arange(T) // SEL_BLK
    lanes2 = np.arange(2 * LANES)
    oh_np = ((lanes2[None, :] >= HEAD_DIM) & (lanes2[None, :] < 3 * HEAD_DIM)
             & ((lanes2[None, :] - HEAD_DIM) % HEAD_DIM == blk[:, None]))
    oh_p = jnp.asarray(oh_np.astype(np.float32))
    oh_s = jnp.zeros((DB * TS, 2 * LANES), F32)

    def tail(x, o_nsa, o_hg, mk, mv, nb, tm, tiled):
        rows = x.shape[0]
        x1, qca = _mixout(x, o_nsa, o_hg, gn_nsa, wo_a, wo_b, g_ca, wq_ca, tm)
        per = rows // nb
        tq = min(per, ROW_TILE)
        if per % 16:
            tq = -(-per // 16) * 16
            qca3 = _pad_rows(qca.reshape(nb, per, D_MODEL), tq)
            oca = _cross_attn(qca3, mk, mv, tq, ml, tiled)[:, :per].reshape(rows, D_MODEL)
            x2 = _mm_res(oca, wo_ca, x1, tm)
        else:
            x2 = _cross_attn(qca.reshape(nb, per, D_MODEL), mk, mv, tq, ml, tiled, wo_ca,
                             x1.reshape(nb, per, D_MODEL)).reshape(rows, D_MODEL)
        return _ffn(x2, g_ffn, wg, wu, wd, g_fin, min(rows, FFN_ROW_TILE), wg.shape[1] // 2)

    M = B * T
    (q, qr, kcp, vcp, _, _, _, _, ksa, kwb, vwb, hm, gt, kct, vct, kst, vst, kwt, vwt) = _proj_in(
        x_prompt.reshape(M, D_MODEL), g_mix, w_in_p, tabs_p, oh_p, ROW_TILE, B)
    r3 = lambda a: a.reshape(B, T, -1)
    cmp_w = (pek, pev, w1k, w1v, w2k, w2v)
    kc_p, vc_p = _compress_rows(r3(kcp), r3(vcp), cmp_w)
    o_nsa = _nsa_prompt(r3(q), r3(qr), r3(gt), kc_p, vc_p, r3(ksa), vst, r3(kwb), r3(vwb), NSA_Q_TILE, NSA_KEY_TILE)
    o_hg, s_p = _hgrn(r3(hm), lb, g_hg, jnp.zeros((B, HG_HEADS, HG_DK, HG_DV), F32), HG_CHUNK, T)
    mkv = _norm_mm(mem_prompt.reshape(B * ml, D_MODEL), row2(norm_mem[0]), w_mem, ROW_TILE)
    mk_p = mkv[:, :D_MODEL].reshape(B, ml, D_MODEL)
    mv_p = mkv[:, D_MODEL:].reshape(B, ml, D_MODEL)
    y_p = tail(x_prompt.reshape(M, D_MODEL), o_nsa.reshape(M, NSA_W), o_hg.reshape(M, HG_W), mk_p, mv_p, B, ROW_TILE, False)

    kv5 = lambda a, n: a.reshape(1, -1, n, NSA_KV, HEAD_DIM)
    wbp = min(WINDOW, T)
    nat = lambda a: jnp.transpose(a.reshape(B, NSA_KV, HEAD_DIM, -1), (0, 3, 1, 2))[None]
    outs_p = (nat(kct), nat(vct), nat(kst), nat(vst),
              nat(kwt[:, :, T - wbp:]), nat(vwt[:, :, T - wbp:]), s_p[None],
              mk_p.reshape(1, B, ml, CA_HEADS, CA_HD), mv_p.reshape(1, B, ml, CA_HEADS, CA_HD))

    MS = DB * TS
    (q, qr, kcs, vcs, kss, vss, kws, vws, _, _, _, hm, gt) = _proj_in(
        x_sample.reshape(MS, D_MODEL), g_mix, w_in_p, tabs_s, oh_s, MS, 1)[:13]
    pt_flat = page_table.reshape(-1).astype(I32)
    pool_t = lambda a: jnp.transpose(a[0], (0, 2, 3, 1)).reshape(n_pool, LANES, PAGE_SIZE)
    kc_s, vc_s = _compress_paged(pt_flat, pool_t(cache_cmp_k), pool_t(cache_cmp_v), cmp_w, DB, n_pages)
    tp, tn = SAMPLE_Q_PAD, SAMPLE_KEY_PAD
    s3 = lambda a, n: _pad_rows(a.reshape(DB, TS, -1), n)
    key_blk = np.arange(past) // SEL_BLK
    oh_keys = jnp.asarray((np.arange(LANES)[:, None] == key_blk[None, :]).astype(np.float32)).astype(BF16)
    nseg = past // CMP_STRIDE
    gmat = jnp.asarray((np.arange(nseg)[:, None] // (SEL_BLK // CMP_STRIDE)
                        == np.arange(LANES)[None, :]).astype(np.float32))
    o_nsa_s, nwk, nwv = _nsa_sample(
        pt_flat, s3(q, tp), s3(qr, tp), s3(gt, tp), kc_s, vc_s, s3(kss, tn), s3(vss, tn), s3(kws, tn), s3(vws, tn),
        cache_win_k[0].reshape(DB, wb, LANES), cache_win_v[0].reshape(DB, wb, LANES), oh_keys, gmat,
        pool_t(cache_sel_k), pool_t(cache_sel_v), n_pages, past, TS)
    o_hg_s, s_s = _hgrn(s3(hm, SAMPLE_Q_PAD), lb, g_hg, state_hgrn[0].astype(F32), SAMPLE_Q_PAD, TS)
    nchunk = CA_HD // LANES
    mem_t = lambda a: jnp.transpose(a[0].reshape(DB, ml, CA_HEADS, nchunk, LANES), (0, 1, 3, 2, 4)).reshape(
        DB, ml * nchunk * CA_HEADS, LANES)
    y_s = tail(x_sample.reshape(MS, D_MODEL), o_nsa_s[:, :TS].reshape(MS, NSA_W), o_hg_s[:, :TS].reshape(MS, HG_W),
               mem_t(cache_mem_k), mem_t(cache_mem_v), DB, MS, True)

    outs_s = (kv5(kcs, TS), kv5(vcs, TS), kv5(kss, TS), kv5(vss, TS),
              nwk.reshape(1, DB, wb, NSA_KV, HEAD_DIM), nwv.reshape(1, DB, wb, NSA_KV, HEAD_DIM), s_s[None])
    return (y_p.reshape(B, T, D_MODEL), y_s.reshape(DB, TS, D_MODEL)) + outs_p + outs_s
```

```python
import functools

import numpy as np
import jax
import jax.numpy as jnp
from jax import lax
from jax.experimental import pallas as pl
from jax.experimental.pallas import tpu as pltpu

F32 = jnp.float32
BF16 = jnp.bfloat16
I32 = jnp.int32

D_MODEL = 1024
NSA_HEADS = 8
NSA_KV = 2
HEAD_DIM = 64
NSA_GROUP = NSA_HEADS // NSA_KV
NSA_W = NSA_HEADS * HEAD_DIM
CMP_STRIDE = 16
CMP_LEN = 32
CMP_HID = 128
SEL_BLK = 64
SEL_TOPK = 16
WINDOW = 512
ROPE_THETA = 500000.0
ROPE_DIM = HEAD_DIM // 4
HG_HEADS = 4
HG_DK = 128
HG_DV = 128
HG_W = HG_HEADS * HG_DV
CA_HEADS = 4
CA_HD = D_MODEL // CA_HEADS
PAGE_SIZE = 128
EPS = 1e-6
FORCE_SCORE = 1e4
NEG = -1e30
MASK_BIAS = -1e9
LANES = 128
VMEM_LIMIT = 56 * 1024 * 1024
ROW_TILE = 256
FFN_ROW_TILE = 512
NSA_Q_TILE = 128
NSA_KEY_TILE = 512
RANK_UNROLL = 8
PAGE_UNROLL = 8
HG_CHUNK = 128
SAMPLE_Q_PAD = 8
SAMPLE_KEY_PAD = 16

C_Q = 0
C_KC, C_VC, C_KS, C_VS, C_KW, C_VW = 512, 640, 768, 896, 1024, 1152
C_AUG = 1280
C_HG = 1536
C_GT = 3584
N_PROJ = 3712


def _cparams(sem):
    return pltpu.CompilerParams(dimension_semantics=sem, vmem_limit_bytes=VMEM_LIMIT)


def _nt(a, b):
    return lax.dot_general(a, b, (((1,), (1,)), ((), ())), preferred_element_type=F32)


def _tn(a, b):
    return lax.dot_general(a, b, (((0,), (0,)), ((), ())), preferred_element_type=F32)


def _mm(a, b):
    return jnp.dot(a, b, preferred_element_type=F32)


def _rms(x, g):
    return x * lax.rsqrt(jnp.mean(x * x, axis=-1, keepdims=True) + EPS) * g


def _masked_softmax(s, mask, axis):
    s = jnp.where(mask, s, NEG)
    m = jnp.max(s, axis=axis, keepdims=True)
    e = jnp.where(mask, jnp.exp(s - m), 0.0)
    return e / jnp.maximum(jnp.sum(e, axis=axis, keepdims=True), 1e-20)


def _log2(n):
    l = int(n).bit_length() - 1
    assert (1 << l) == n, n
    return l


def _proj_in_kernel(x_ref, g_ref, w_ref, c_ref, sa_ref, sb_ref, oh_ref,
                    q_ref, qr_ref, kc_ref, vc_ref, ks_ref, vs_ref, kw_ref, vw_ref,
                    ksa_ref, kwb_ref, vwb_ref, hm_ref, gt_ref,
                    kct_ref, vct_ref, kst_ref, vst_ref, kwt_ref, vwt_ref):
    xn = _rms(x_ref[...], g_ref[...]).astype(BF16)
    c, sa, sb = c_ref[...], sa_ref[...], sb_ref[...]

    def mm(lo, hi):
        return _mm(xn, w_ref[:, lo:hi])

    def rope(y):
        n = y.shape[1] // LANES
        cc, aa, bb = (jnp.concatenate([t] * n, axis=1) if n > 1 else t for t in (c, sa, sb))
        w = y.shape[1]
        return y * cc + pltpu.roll(y, w - ROPE_DIM // 2, 1) * aa + pltpu.roll(y, ROPE_DIM // 2, 1) * bb

    yq = mm(C_Q, C_Q + NSA_W) * (HEAD_DIM ** -0.5)
    q_ref[...] = yq.astype(BF16)
    qr_ref[...] = rope(yq).astype(BF16)
    kw = vw = None
    for col, roped, ref, tref in ((C_KC, False, kc_ref, kct_ref), (C_VC, False, vc_ref, vct_ref),
                                  (C_KS, True, ks_ref, kst_ref), (C_VS, False, vs_ref, vst_ref),
                                  (C_KW, True, kw_ref, kwt_ref), (C_VW, False, vw_ref, vwt_ref)):
        y = mm(col, col + LANES)
        y = rope(y) if roped else y
        ref[...] = y
        tref[0] = y.T
        kw = y if col == C_KW else kw
        vw = y if col == C_VW else vw
    kwb_ref[...] = kw.astype(BF16)
    vwb_ref[...] = vw.astype(BF16)
    ksa_ref[...] = (rope(mm(C_AUG, C_AUG + 2 * LANES)) + oh_ref[...]).astype(BF16)
    hm_ref[...] = mm(C_HG, C_HG + 4 * HG_W)
    gt_ref[...] = mm(C_GT, C_GT + LANES)


def _proj_in(x2d, g, w, tabs, oh, tm, nb):
    m = x2d.shape[0]
    per = m // nb
    nt_tab = tabs[0].shape[0] // tm
    nt_seq = per // tm
    row = lambda i: (i, 0)
    tab = lambda i: (i % nt_tab, 0)
    const = lambda i: (0, 0)
    widths = [(NSA_W, BF16), (NSA_W, BF16)] + [(LANES, F32)] * 6 + [(2 * LANES, BF16)] + [(LANES, BF16)] * 2 \
        + [(4 * HG_W, F32), (LANES, F32)]
    return pl.pallas_call(
        _proj_in_kernel,
        grid=(m // tm,),
        in_specs=[pl.BlockSpec((tm, D_MODEL), row), pl.BlockSpec((1, D_MODEL), const),
                  pl.BlockSpec((D_MODEL, N_PROJ), const),
                  pl.BlockSpec((tm, LANES), tab), pl.BlockSpec((tm, LANES), tab), pl.BlockSpec((tm, LANES), tab),
                  pl.BlockSpec((tm, 2 * LANES), tab)],
        out_specs=[pl.BlockSpec((tm, wd), row) for wd, _ in widths]
        + [pl.BlockSpec((1, LANES, tm), lambda i: (i // nt_seq, 0, i % nt_seq))] * 6,
        out_shape=[jax.ShapeDtypeStruct((m, wd), dt) for wd, dt in widths]
        + [jax.ShapeDtypeStruct((nb, LANES, per), F32)] * 6,
        compiler_params=_cparams(("parallel",)),
        name="proj_in",
    )(x2d, g, w, *tabs, oh)


def _page_copy(pool_hbm, page, buf, slot, p, sem):
    cols = pl.ds(pl.multiple_of(p * PAGE_SIZE, PAGE_SIZE), PAGE_SIZE)
    return pltpu.make_async_copy(pool_hbm.at[page], buf.at[slot, :, cols], sem)


def _gather_start(pt_ref, pools, bufs, sems, b, slot, n_pages):
    def body(p, carry):
        page = pt_ref[b * n_pages + p]
        for pool, buf, sem in zip(pools, bufs, sems):
            _page_copy(pool, page, buf, slot, p, sem.at[slot]).start()
        return carry
    lax.fori_loop(0, n_pages, body, 0, unroll=PAGE_UNROLL)


def _gather_wait(pools, bufs, sems, slot, n_pages):
    def body(p, carry):
        for pool, buf, sem in zip(pools, bufs, sems):
            _page_copy(pool, 0, buf, slot, p, sem.at[slot]).wait()
        return carry
    lax.fori_loop(0, n_pages, body, 0, unroll=PAGE_UNROLL)


def _gather_step(pt_ref, pools, bufs, sems, n_pages):
    b = pl.program_id(0)
    nb = pl.num_programs(0)
    slot = b % 2

    @pl.when(b == 0)
    def _():
        _gather_start(pt_ref, pools, bufs, sems, 0, 0, n_pages)

    @pl.when(b + 1 < nb)
    def _():
        _gather_start(pt_ref, pools, bufs, sems, b + 1, 1 - slot, n_pages)

    _gather_wait(pools, bufs, sems, slot, n_pages)
    return slot


def _compress_core(tok, pe_ref, w1_ref, w2_ref, bias_scr, first, out, nseg, transposed_out=False):
    npair = CMP_STRIDE // 2
    nh = NSA_KV * CMP_HID

    @pl.when(first)
    def _():
        acc = _mm(pe_ref[:, 0:2 * LANES].astype(BF16), w1_ref[0])
        for jj in range(1, npair):
            acc = acc + _mm(pe_ref[:, jj * 2 * LANES:(jj + 1) * 2 * LANES].astype(BF16), w1_ref[jj])
        bias_scr[...] = acc

    acc = None
    for jj in range(npair):
        xp = jnp.concatenate([tok[pl.ds(2 * jj, nseg, stride=CMP_STRIDE), :],
                              tok[pl.ds(2 * jj + 1, nseg, stride=CMP_STRIDE), :]], axis=1).astype(BF16)
        part = _mm(xp, w1_ref[jj])
        acc = part if acc is None else acc + part
    ha = acc[:, 0:nh] + bias_scr[0:1, 0:nh]
    hb = acc[:, nh:2 * nh] + bias_scr[1:2, nh:2 * nh]
    h = jax.nn.gelu(ha + pltpu.roll(hb, nseg - 1, 0))
    row = lax.broadcasted_iota(I32, (nseg, 1), 0)
    h = jnp.where(row < nseg - 1, h, 0.0)
    res = _mm(h.astype(BF16), w2_ref[...]).astype(BF16)
    out[...] = res.T if transposed_out else res


def _compress_rows_kernel(k_ref, v_ref, pek_ref, pev_ref, w1k_ref, w1v_ref, w2k_ref, w2v_ref,
                          kc_ref, vct_ref, bias_scr, *, nseg):
    first = pl.program_id(0) == 0
    for i, (tok, pe, w1, w2, out) in enumerate(((k_ref, pek_ref, w1k_ref, w2k_ref, kc_ref),
                                                (v_ref, pev_ref, w1v_ref, w2v_ref, vct_ref))):
        _compress_core(tok.at[0], pe, w1, w2, bias_scr.at[i], first, out.at[0], nseg, transposed_out=i == 1)


def _compress_paged_kernel(pt_ref, poolk_hbm, poolv_hbm, pek_ref, pev_ref, w1k_ref, w1v_ref, w2k_ref, w2v_ref,
                           kc_ref, vc_ref, xk_buf, xv_buf, tokk_scr, tokv_scr, bias_scr, semk, semv, *, n_pages):
    slot = _gather_step(pt_ref, (poolk_hbm, poolv_hbm), (xk_buf, xv_buf), (semk, semv), n_pages)
    first = pl.program_id(0) == 0
    nseg = n_pages * (PAGE_SIZE // CMP_STRIDE)
    for i, (xbuf, tok, pe, w1, w2, out) in enumerate(((xk_buf, tokk_scr, pek_ref, w1k_ref, w2k_ref, kc_ref),
                                                      (xv_buf, tokv_scr, pev_ref, w1v_ref, w2v_ref, vc_ref))):
        tok[...] = xbuf[slot].T
        _compress_core(tok, pe, w1, w2, bias_scr.at[i], first, out.at[0], nseg)


def _compress_weight_specs(const2, const3):
    npair = CMP_STRIDE // 2
    return [pl.BlockSpec((8, CMP_STRIDE * LANES), const2), pl.BlockSpec((8, CMP_STRIDE * LANES), const2),
            pl.BlockSpec((npair, 2 * LANES, 4 * CMP_HID), const3), pl.BlockSpec((npair, 2 * LANES, 4 * CMP_HID), const3),
            pl.BlockSpec((NSA_KV * CMP_HID, LANES), const2), pl.BlockSpec((NSA_KV * CMP_HID, LANES), const2)]


def _compress_rows(k_rows, v_rows, weights):
    nb, ntok = k_rows.shape[:2]
    nseg = ntok // CMP_STRIDE
    per_b = lambda b: (b, 0, 0)
    return pl.pallas_call(
        functools.partial(_compress_rows_kernel, nseg=nseg),
        grid=(nb,),
        in_specs=[pl.BlockSpec((1, ntok, LANES), per_b), pl.BlockSpec((1, ntok, LANES), per_b)]
        + _compress_weight_specs(lambda b: (0, 0), lambda b: (0, 0, 0)),
        out_specs=[pl.BlockSpec((1, nseg, LANES), per_b), pl.BlockSpec((1, LANES, nseg), per_b)],
        out_shape=[jax.ShapeDtypeStruct((nb, nseg, LANES), BF16), jax.ShapeDtypeStruct((nb, LANES, nseg), BF16)],
        scratch_shapes=[pltpu.VMEM((2, 8, 4 * CMP_HID), F32)],
        compiler_params=_cparams(("arbitrary",)),
        name="compress_rows",
    )(k_rows, v_rows, *weights)


def _compress_paged(page_table_flat, pool_k, pool_v, weights, n_batch, n_pages):
    past = n_pages * PAGE_SIZE
    nseg = past // CMP_STRIDE
    gs = pltpu.PrefetchScalarGridSpec(
        num_scalar_prefetch=1,
        grid=(n_batch,),
        in_specs=[pl.BlockSpec(memory_space=pl.ANY), pl.BlockSpec(memory_space=pl.ANY)]
        + _compress_weight_specs(lambda b, pt: (0, 0), lambda b, pt: (0, 0, 0)),
        out_specs=[pl.BlockSpec((1, nseg, LANES), lambda b, pt: (b, 0, 0))] * 2,
        scratch_shapes=[pltpu.VMEM((2, LANES, past), F32), pltpu.VMEM((2, LANES, past), F32),
                        pltpu.VMEM((past, LANES), F32), pltpu.VMEM((past, LANES), F32),
                        pltpu.VMEM((2, 8, 4 * CMP_HID), F32),
                        pltpu.SemaphoreType.DMA((2,)), pltpu.SemaphoreType.DMA((2,))],
    )
    return pl.pallas_call(
        functools.partial(_compress_paged_kernel, n_pages=n_pages),
        grid_spec=gs,
        out_shape=[jax.ShapeDtypeStruct((n_batch, nseg, LANES), BF16)] * 2,
        compiler_params=_cparams(("arbitrary",)),
        name="compress_paged",
    )(page_table_flat, pool_k, pool_v, *weights)


def _nsa_prompt_kernel(q_ref, qr_ref, gt_ref, kc_ref, vct_ref, ksa_ref, vst_ref, kw_ref, vwt_ref, o_ref,
                       psum_scr, sc_scr, *, tq, tk, ncp, nsel, topk):
    i = pl.program_id(1)
    s0 = i * tq
    rows = NSA_GROUP * tq
    hk = tk // 2
    wl = WINDOW + tq
    upper = lax.broadcasted_iota(I32, (tq, LANES), 1) >= HEAD_DIM
    kc = kc_ref[0]
    t_cols = s0 + (lax.broadcasted_iota(I32, (1, rows), 1) & (tq - 1))
    cmask = lax.broadcasted_iota(I32, (ncp, rows), 0) * CMP_STRIDE + (CMP_LEN - 1) <= t_cols
    ws = pl.multiple_of(jnp.maximum(s0 - WINDOW, 0), tq)
    kpos_w = ws + lax.broadcasted_iota(I32, (wl, rows), 0)
    wmask = (kpos_w <= t_cols) & (kpos_w > t_cols - WINDOW)
    kw_t = kw_ref[0, pl.ds(ws, wl), :]
    o_ct, o_wt, o_st = [], [], []
    for g in range(NSA_KV):
        mine = upper if g == 1 else jnp.logical_not(upper)

        def stack(ref, fill):
            return jnp.concatenate(
                [jnp.where(mine, ref[0, :, r * LANES:(r + 1) * LANES], fill) for r in range(NSA_GROUP)], axis=0)

        def ones_rows(v):
            up = lax.broadcasted_iota(I32, v.shape, 0) >= HEAD_DIM
            return jnp.where(up if g == 1 else jnp.logical_not(up), v, 1.0).astype(BF16)

        def dim_rows(x):
            return x[g * HEAD_DIM:(g + 1) * HEAD_DIM]

        def den_rows(x):
            return x[(1 - g) * HEAD_DIM:(2 - g) * HEAD_DIM]

        p_c = _masked_softmax(_nt(kc, stack(q_ref, 0)), cmask, 0)
        o_ct.append(dim_rows(_mm(vct_ref[0], p_c.astype(BF16))))
        psum = p_c[:, 0:tq]
        for r in range(1, NSA_GROUP):
            psum = psum + p_c[:, r * tq:(r + 1) * tq]
        psum_scr[...] = psum
        cps = SEL_BLK // CMP_STRIDE
        imp = psum_scr[pl.ds(0, nsel, stride=cps), :]
        for u in range(1, cps):
            imp = imp + psum_scr[pl.ds(u, nsel, stride=cps), :]
        j = lax.broadcasted_iota(I32, (nsel, tq), 0)
        cur = (s0 + lax.broadcasted_iota(I32, (nsel, tq), 1)) >> _log2(SEL_BLK)
        avail = j <= cur
        forced = (j == 0) | (j == cur) | (j == cur - 1)
        score = jnp.where(avail, jnp.where(forced, FORCE_SCORE, imp), -FORCE_SCORE)
        sc_scr[...] = score

        def rank_body(grp, cnt):
            for u in range(RANK_UNROLL):
                jp = grp * RANK_UNROLL + u
                other = sc_scr[pl.ds(jp, 1), :]
                beats = (other > score) | ((other == score) & (jp < j))
                cnt = cnt + beats.astype(I32)
            return cnt

        n_live = ((s0 + tq - 1) >> _log2(SEL_BLK)) + 1
        n_grp = jnp.minimum((n_live + RANK_UNROLL - 1) // RANK_UNROLL, nsel // RANK_UNROLL)
        cnt = lax.fori_loop(0, n_grp, rank_body, jnp.zeros((nsel, tq), I32))
        bias_t = jnp.where((cnt < topk) & avail, 0.0, MASK_BIAS)
        if nsel < HEAD_DIM:
            bias_t = jnp.concatenate([bias_t, jnp.full((HEAD_DIM - nsel, tq), MASK_BIAS, F32)], axis=0)
        bias = jnp.concatenate([bias_t, bias_t], axis=0).T.astype(BF16)
        qs = stack(qr_ref, bias)
        glanes = slice(g * LANES, (g + 1) * LANES)

        def scores(k0):
            return _nt(ksa_ref[0, pl.ds(k0, tk), glanes], qs)

        def col_max(s):
            return tuple(jnp.max(s[c * hk:(c + 1) * hk], axis=0, keepdims=True) for c in range(2))

        def update(s, mx, k0, stats):
            new = []
            for c, (m, acc) in enumerate(stats):
                v_one = ones_rows(vst_ref[0, :, pl.ds(k0 + c * hk, hk)])
                m_new = jnp.maximum(m, mx[c])
                p = jnp.exp((s[c * hk:(c + 1) * hk] - m_new).astype(BF16))
                new.append((m_new, jnp.exp(m - m_new) * acc + _mm(v_one, p)))
            return tuple(new)

        def sel_body(kk, carry):
            s_cur, mx_cur, stats = carry
            k0 = pl.multiple_of(kk * tk, tk)
            s_next = scores(k0 + tk)
            stats = update(s_cur, mx_cur, k0, stats)
            return s_next, col_max(s_next), stats

        n_full = s0 // tk
        stats0 = tuple((jnp.full((1, rows), NEG, F32), jnp.zeros((LANES, rows), F32)) for _ in range(2))
        s_first = scores(0)
        s_last, _, stats = lax.fori_loop(0, n_full, sel_body, (s_first, col_max(s_first), stats0))
        k_last = pl.multiple_of(n_full * tk, tk)
        causal = k_last + lax.broadcasted_iota(I32, (tk, rows), 0) <= t_cols
        s_last = jnp.where(causal, s_last, NEG)
        (m_a, acc_a), (m_b, acc_b) = update(s_last, col_max(s_last), k_last, stats)
        m_s = jnp.maximum(m_a, m_b)
        acc_s = jnp.exp(m_a - m_s) * acc_a + jnp.exp(m_b - m_s) * acc_b
        o_st.append(dim_rows(acc_s) / jnp.maximum(den_rows(acc_s), 1e-20))
        s_w = jnp.where(wmask, _nt(kw_t, stack(qr_ref, 0)), NEG)
        e_w = jnp.exp((s_w - jnp.max(s_w, axis=0, keepdims=True)).astype(BF16))
        acc_w = _mm(ones_rows(vwt_ref[0, :, pl.ds(ws, wl)]), e_w)
        o_wt.append(dim_rows(acc_w) / jnp.maximum(den_rows(acc_w), 1e-20))
    gsig_t = jax.nn.sigmoid(gt_ref[0]).T
    out_slots = []
    for r in range(NSA_GROUP):
        rs = slice(r * tq, (r + 1) * tq)
        halves = []
        for g in range(NSA_KV):
            c0 = 3 * (g * NSA_GROUP + r)
            halves.append(gsig_t[c0:c0 + 1] * o_ct[g][:, rs] + gsig_t[c0 + 1:c0 + 2] * o_st[g][:, rs]
                          + gsig_t[c0 + 2:c0 + 3] * o_wt[g][:, rs])
        out_slots.append(jnp.concatenate(halves, axis=0).T)
    o_ref[0] = jnp.concatenate(out_slots, axis=1)


def _nsa_prompt(q, qr, gt, kc, vct, ksa, vst, kwb, vwt, tq, tk):
    b, t = q.shape[:2]
    ncp = t // CMP_STRIDE
    nsel = t // SEL_BLK
    assert t % tk == 0 and tk % tq == 0 and t >= WINDOW + tq and nsel <= HEAD_DIM and WINDOW % tq == 0
    assert nsel % RANK_UNROLL == 0
    blk = lambda bb, i: (bb, i, 0)
    whole = lambda bb, i: (bb, 0, 0)
    return pl.pallas_call(
        functools.partial(_nsa_prompt_kernel, tq=tq, tk=tk, ncp=ncp, nsel=nsel, topk=min(SEL_TOPK, nsel)),
        grid=(b, t // tq),
        in_specs=[pl.BlockSpec((1, tq, NSA_W), blk), pl.BlockSpec((1, tq, NSA_W), blk),
                  pl.BlockSpec((1, tq, LANES), blk),
                  pl.BlockSpec((1, ncp, LANES), whole), pl.BlockSpec((1, LANES, ncp), whole),
                  pl.BlockSpec((1, t, 2 * LANES), whole), pl.BlockSpec((1, LANES, t), whole),
                  pl.BlockSpec((1, t, LANES), whole), pl.BlockSpec((1, LANES, t), whole)],
        out_specs=pl.BlockSpec((1, tq, NSA_W), blk),
        out_shape=jax.ShapeDtypeStruct((b, t, NSA_W), F32),
        scratch_shapes=[pltpu.VMEM((ncp, tq), F32), pltpu.VMEM((nsel, tq), F32)],
        compiler_params=_cparams(("parallel", "parallel")),
        name="nsa_prompt",
    )(q, qr, gt, kc, vct, ksa, vst, kwb, vwt)


def _nsa_sample_kernel(pt_ref, q_ref, qr_ref, gt_ref, kc_ref, vc_ref, ksn_ref, vsn_ref, kwn_ref, vwn_ref,
                       wink_ref, winv_ref, oht_ref, gmat_ref, poolk_hbm, poolv_hbm,
                       o_ref, nwk_ref, nwv_ref, kbuf, vbuf, semk, semv,
                       *, n_pages, past, ts, tp, tn, wb, topk_past):
    slot = _gather_step(pt_ref, (poolk_hbm, poolv_hbm), (kbuf, vbuf), (semk, semv), n_pages)
    nseg = n_pages * (PAGE_SIZE // CMP_STRIDE)
    npb = past // SEL_BLK
    rows = NSA_HEADS * tp
    lane = lax.broadcasted_iota(I32, (tp, LANES), 1)
    upper = lane >= HEAD_DIM

    def stack(ref):
        parts = []
        for g in range(NSA_KV):
            mine = upper if g == 1 else jnp.logical_not(upper)
            parts += [jnp.where(mine, ref[0, :, r * LANES:(r + 1) * LANES], 0) for r in range(NSA_GROUP)]
        return jnp.concatenate(parts, axis=0)

    t_rows = lax.broadcasted_iota(I32, (rows, 1), 0) & (tp - 1)
    qc = stack(q_ref)
    cend = lax.broadcasted_iota(I32, (rows, nseg), 1) * CMP_STRIDE + (CMP_LEN - 1)
    p_c = _masked_softmax(_nt(qc, kc_ref[0]), cend <= past + t_rows, 1)
    o_c = _mm(p_c.astype(BF16), vc_ref[0])
    impn = []
    for g in range(NSA_KV):
        acc = p_c[(g * NSA_GROUP) * tp:(g * NSA_GROUP + 1) * tp]
        for r in range(1, NSA_GROUP):
            acc = acc + p_c[(g * NSA_GROUP + r) * tp:(g * NSA_GROUP + r + 1) * tp]
        impn.append(acc)
    impn = jnp.concatenate(impn, axis=0)
    imp = jnp.dot(impn, gmat_ref[...], precision=lax.Precision.HIGHEST, preferred_element_type=F32)
    j = lax.broadcasted_iota(I32, (NSA_KV * tp, LANES), 1)
    avail = j < npb
    forced = (j == 0) | (j == npb - 1)
    score = jnp.where(avail, jnp.where(forced, FORCE_SCORE, imp), -FORCE_SCORE)
    cnt = jnp.zeros((NSA_KV * tp, LANES), I32)
    for jp in range(npb):
        other = score[:, jp:jp + 1]
        beats = (other > score) | ((other == score) & (jp < j))
        cnt = cnt + beats.astype(I32)
    bias = jnp.where((cnt < topk_past) & avail, 0.0, MASK_BIAS).astype(BF16)
    bias = jnp.concatenate([bias[0:tp]] * NSA_GROUP + [bias[tp:2 * tp]] * NSA_GROUP, axis=0)
    qs = stack(qr_ref)
    key_t = lax.broadcasted_iota(I32, (rows, tn), 1)
    new_mask = (key_t <= t_rows) & (key_t < ts)
    s_p = _mm(qs, kbuf[slot].astype(BF16)) + _mm(bias, oht_ref[...])
    s_n = jnp.where(new_mask, _nt(qs, ksn_ref[0].astype(BF16)), NEG)
    m_s = jnp.maximum(jnp.max(s_p, axis=1, keepdims=True), jnp.max(s_n, axis=1, keepdims=True))
    e_p = jnp.exp(s_p - m_s)
    e_n = jnp.where(new_mask, jnp.exp(s_n - m_s), 0.0)
    l_s = jnp.sum(e_p, axis=1, keepdims=True) + jnp.sum(e_n, axis=1, keepdims=True)
    o_s = (_nt(e_p.astype(BF16), vbuf[slot].astype(BF16)) + _mm(e_n.astype(BF16), vsn_ref[0].astype(BF16))) \
        / jnp.maximum(l_s, 1e-20)
    wk = wink_ref[0]
    wv = winv_ref[0]
    iw = lax.broadcasted_iota(I32, (rows, wb), 1)
    wmask = (iw > t_rows + (wb - WINDOW)) & (iw >= wb - past)
    s_w = jnp.where(wmask, _nt(qs, wk.astype(BF16)), NEG)
    s_wn = jnp.where(new_mask, _nt(qs, kwn_ref[0].astype(BF16)), NEG)
    m_w = jnp.maximum(jnp.max(s_w, axis=1, keepdims=True), jnp.max(s_wn, axis=1, keepdims=True))
    e_w = jnp.where(wmask, jnp.exp(s_w - m_w), 0.0)
    e_wn = jnp.where(new_mask, jnp.exp(s_wn - m_w), 0.0)
    inv = 1.0 / jnp.maximum(jnp.sum(e_w, axis=1, keepdims=True) + jnp.sum(e_wn, axis=1, keepdims=True), 1e-20)
    o_w = _mm((e_w * inv).astype(BF16), wv.astype(BF16)) + _mm((e_wn * inv).astype(BF16), vwn_ref[0].astype(BF16))
    gsig = jax.nn.sigmoid(gt_ref[0])
    slots = []
    for r in range(NSA_GROUP):
        per_g = []
        for g in range(NSA_KV):
            h = g * NSA_GROUP + r
            rs = slice(h * tp, (h + 1) * tp)
            per_g.append(gsig[:, 3 * h:3 * h + 1] * o_c[rs] + gsig[:, 3 * h + 1:3 * h + 2] * o_s[rs]
                         + gsig[:, 3 * h + 2:3 * h + 3] * o_w[rs])
        slots.append(jnp.where(upper, per_g[1], per_g[0]))
    o_ref[0] = jnp.concatenate(slots, axis=1)
    sub = lax.broadcasted_iota(I32, (8, LANES), 0)
    for win, new_ref, out in ((wk, kwn_ref, nwk_ref), (wv, vwn_ref, nwv_ref)):
        rolled = pltpu.roll(win, wb - ts, 0)
        shifted = pltpu.roll(new_ref[0, 0:8, :], 8 - ts, 0)
        out[0, 0:wb - 8, :] = rolled[0:wb - 8]
        out[0, wb - 8:wb, :] = jnp.where(sub >= 8 - ts, shifted, rolled[wb - 8:wb])


def _nsa_sample(page_table_flat, q, qr, gt, kc, vc, ksn, vsn, kwn, vwn, win_k, win_v, oh, gmat, pool_k, pool_v,
                n_pages, past, ts):
    db, tp = q.shape[:2]
    tn = ksn.shape[1]
    wb = win_k.shape[1]
    nseg = n_pages * (PAGE_SIZE // CMP_STRIDE)
    npb = past // SEL_BLK
    assert ts <= 8 and npb <= LANES and wb >= 8 and past % PAGE_SIZE == 0
    per_b = lambda b, pt: (b, 0, 0)
    const2 = lambda b, pt: (0, 0)
    gs = pltpu.PrefetchScalarGridSpec(
        num_scalar_prefetch=1,
        grid=(db,),
        in_specs=[pl.BlockSpec((1, tp, NSA_W), per_b), pl.BlockSpec((1, tp, NSA_W), per_b),
                  pl.BlockSpec((1, tp, LANES), per_b),
                  pl.BlockSpec((1, nseg, LANES), per_b), pl.BlockSpec((1, nseg, LANES), per_b),
                  pl.BlockSpec((1, tn, LANES), per_b), pl.BlockSpec((1, tn, LANES), per_b),
                  pl.BlockSpec((1, tn, LANES), per_b), pl.BlockSpec((1, tn, LANES), per_b),
                  pl.BlockSpec((1, wb, LANES), per_b), pl.BlockSpec((1, wb, LANES), per_b),
                  pl.BlockSpec((LANES, past), const2), pl.BlockSpec((nseg, LANES), const2),
                  pl.BlockSpec(memory_space=pl.ANY), pl.BlockSpec(memory_space=pl.ANY)],
        out_specs=[pl.BlockSpec((1, tp, NSA_W), per_b), pl.BlockSpec((1, wb, LANES), per_b),
                   pl.BlockSpec((1, wb, LANES), per_b)],
        scratch_shapes=[pltpu.VMEM((2, LANES, past), F32), pltpu.VMEM((2, LANES, past), F32),
                        pltpu.SemaphoreType.DMA((2,)), pltpu.SemaphoreType.DMA((2,))],
    )
    return pl.pallas_call(
        functools.partial(_nsa_sample_kernel, n_pages=n_pages, past=past, ts=ts, tp=tp, tn=tn, wb=wb,
                          topk_past=min(SEL_TOPK, npb + 1) - 1),
        grid_spec=gs,
        out_shape=[jax.ShapeDtypeStruct((db, tp, NSA_W), F32), jax.ShapeDtypeStruct((db, wb, LANES), F32),
                   jax.ShapeDtypeStruct((db, wb, LANES), F32)],
        compiler_params=_cparams(("arbitrary",)),
        name="nsa_sample",
    )(page_table_flat, q, qr, gt, kc, vc, ksn, vsn, kwn, vwn, win_k, win_v, oh, gmat, pool_k, pool_v)


def _hgrn_kernel(hm_ref, lb_ref, og_ref, s0_ref, o_ref, sout_ref, st_scr, *, chunk, t_real, levels):
    c = pl.program_id(1)
    nc = pl.num_programs(1)

    @pl.when(c == 0)
    def _():
        for hd in range(HG_HEADS):
            st_scr[hd] = s0_ref[0, hd].T

    row = lax.broadcasted_iota(I32, (chunk, 1), 0)
    valid = (c * chunk + row) < t_real
    row_c = lax.broadcasted_iota(I32, (chunk, chunk), 0)
    col_c = lax.broadcasted_iota(I32, (chunk, chunk), 1)
    outs = []
    for hd in range(HG_HEADS):
        sl = slice(hd * HG_DK, (hd + 1) * HG_DK)
        q = hm_ref[0, :, hd * HG_DK:(hd + 1) * HG_DK]
        z = hm_ref[0, :, HG_W + hd * HG_DK:HG_W + (hd + 1) * HG_DK]
        v = hm_ref[0, :, 2 * HG_W + hd * HG_DV:2 * HG_W + (hd + 1) * HG_DV]
        gate = hm_ref[0, :, 3 * HG_W + hd * HG_DV:3 * HG_W + (hd + 1) * HG_DV]
        lb = lb_ref[:, sl]
        logf = jnp.where(valid, jnp.log(lb + (1.0 - lb) * jax.nn.sigmoid(z)), 0.0)
        k = jnp.where(valid, (1.0 - lb) * jax.nn.sigmoid(-z), 0.0)
        b = logf
        step = 1
        while step < chunk:
            b = b + jnp.where(row >= step, pltpu.roll(b, step, 0), 0.0)
            step *= 2
        st = st_scr[hd]
        o = _nt((q * jnp.exp(b)).astype(BF16), st.astype(BF16))
        if levels:
            a_mat = jnp.zeros((chunk, chunk), F32)
            for h in levels:
                if 2 * h <= 8:
                    b3 = b.reshape(chunk // 8, 8, HG_DK)
                    m = None
                    for blk in range(8 // (2 * h)):
                        mid_row = blk * 2 * h + h - 1
                        cand = jnp.broadcast_to(b3[:, mid_row:mid_row + 1, :], b3.shape).reshape(chunk, HG_DK)
                        m = cand if m is None else jnp.where((row & 7) >= blk * 2 * h, cand, m)
                else:
                    m = jnp.concatenate(
                        [jnp.broadcast_to(b[blk * 2 * h + h - 1:blk * 2 * h + h, :], (2 * h, HG_DK))
                         for blk in range(chunk // (2 * h))], axis=0)
                second = ((row >> _log2(h)) & 1) == 1
                qs = jnp.where(second, q * jnp.exp(jnp.minimum(b - m, 0.0)), 0.0)
                ks = jnp.where(second, 0.0, k * jnp.exp(jnp.minimum(m - b, 0.0)))
                same = (row_c >> _log2(2 * h)) == (col_c >> _log2(2 * h))
                a_mat = a_mat + jnp.where(same, _nt(qs.astype(BF16), ks.astype(BF16)), 0.0)
            o = o + _mm(a_mat.astype(BF16), v.astype(BF16))
        dblk = min(levels) if levels else chunk
        for d in range(dblk):
            if d == 0:
                o = o + jnp.sum(q * k, axis=1, keepdims=True) * v
                continue
            kd, bd, vd = (pltpu.roll(x, d, 0) for x in (k, b, v))
            p = jnp.where((row & (dblk - 1)) >= d, q * kd * jnp.exp(jnp.minimum(b - bd, 0.0)), 0.0)
            o = o + jnp.sum(p, axis=1, keepdims=True) * vd
        bl = b[chunk - 1:chunk, :]
        st_scr[hd] = jnp.exp(bl) * st + _tn(v.astype(BF16), (k * jnp.exp(bl - b)).astype(BF16))
        outs.append(_rms(o, og_ref[:, sl]) * (gate * jax.nn.sigmoid(gate)))
    o_ref[0] = jnp.concatenate(outs, axis=1)

    @pl.when(c == nc - 1)
    def _():
        for hd in range(HG_HEADS):
            sout_ref[0, hd] = st_scr[hd].T


def _hgrn(hm, lb, og, s0, chunk, t_real):
    b, tpad = hm.shape[:2]
    assert chunk <= 8 or chunk % 16 == 0
    levels = tuple(h for h in (64, 32, 16, 8, 4, 2, 1) if 2 * h <= chunk and chunk >= 16)
    return pl.pallas_call(
        functools.partial(_hgrn_kernel, chunk=chunk, t_real=t_real, levels=levels),
        grid=(b, tpad // chunk),
        in_specs=[pl.BlockSpec((1, chunk, 4 * HG_W), lambda bb, c: (bb, c, 0)),
                  pl.BlockSpec((1, HG_W), lambda bb, c: (0, 0)), pl.BlockSpec((1, HG_W), lambda bb, c: (0, 0)),
                  pl.BlockSpec((1, HG_HEADS, HG_DK, HG_DV), lambda bb, c: (bb, 0, 0, 0))],
        out_specs=[pl.BlockSpec((1, chunk, HG_W), lambda bb, c: (bb, c, 0)),
                   pl.BlockSpec((1, HG_HEADS, HG_DK, HG_DV), lambda bb, c: (bb, 0, 0, 0))],
        out_shape=[jax.ShapeDtypeStruct((b, tpad, HG_W), F32),
                   jax.ShapeDtypeStruct((b, HG_HEADS, HG_DK, HG_DV), F32)],
        scratch_shapes=[pltpu.VMEM((HG_HEADS, HG_DV, HG_DK), F32)],
        compiler_params=_cparams(("parallel", "arbitrary")),
        name="hgrn",
    )(hm, lb, og, s0)


def _mixout_kernel(x_ref, on_ref, oh_ref, gn_ref, woa_ref, wob_ref, gca_ref, wq_ref, x1_ref, q_ref):
    a = _rms(on_ref[...], gn_ref[...]).astype(BF16)
    x1 = x_ref[...] + _mm(a, woa_ref[...]) + _mm(oh_ref[...].astype(BF16), wob_ref[...])
    x1_ref[...] = x1
    hn = _rms(x1, gca_ref[...]).astype(BF16)
    q_ref[...] = (_mm(hn, wq_ref[...]) * (CA_HD ** -0.5)).astype(BF16)


def _mixout(x, o_nsa, o_hg, gn, woa, wob, gca, wq, tm):
    m = x.shape[0]
    row = lambda i: (i, 0)
    const = lambda i: (0, 0)
    return pl.pallas_call(
        _mixout_kernel,
        grid=(m // tm,),
        in_specs=[pl.BlockSpec((tm, D_MODEL), row), pl.BlockSpec((tm, NSA_W), row), pl.BlockSpec((tm, HG_W), row),
                  pl.BlockSpec((1, NSA_W), const), pl.BlockSpec((NSA_W, D_MODEL), const),
                  pl.BlockSpec((HG_W, D_MODEL), const), pl.BlockSpec((1, D_MODEL), const),
                  pl.BlockSpec((D_MODEL, D_MODEL), const)],
        out_specs=[pl.BlockSpec((tm, D_MODEL), row), pl.BlockSpec((tm, D_MODEL), row)],
        out_shape=[jax.ShapeDtypeStruct((m, D_MODEL), F32), jax.ShapeDtypeStruct((m, D_MODEL), BF16)],
        compiler_params=_cparams(("parallel",)),
        name="mixout",
    )(x, o_nsa, o_hg, gn, woa, wob, gca, wq)


def _norm_mm_kernel(x_ref, g_ref, w_ref, o_ref):
    o_ref[...] = _mm(_rms(x_ref[...], g_ref[...]).astype(BF16), w_ref[...])


def _norm_mm(x, g, w, tm):
    m, n = x.shape[0], w.shape[1]
    return pl.pallas_call(
        _norm_mm_kernel,
        grid=(m // tm,),
        in_specs=[pl.BlockSpec((tm, D_MODEL), lambda i: (i, 0)), pl.BlockSpec((1, D_MODEL), lambda i: (0, 0)),
                  pl.BlockSpec((D_MODEL, n), lambda i: (0, 0))],
        out_specs=pl.BlockSpec((tm, n), lambda i: (i, 0)),
        out_shape=jax.ShapeDtypeStruct((m, n), F32),
        compiler_params=_cparams(("parallel",)),
        name="norm_mm",
    )(x, g, w)


def _ca_kernel(q_ref, mk_ref, mv_ref, *rest, ml, tiled):
    o_ref = rest[-1]
    nchunk = CA_HD // LANES

    def head(ref, hh):
        if tiled:
            return jnp.concatenate([ref[0, pl.ds(c * CA_HEADS + hh, ml, stride=nchunk * CA_HEADS), :]
                                    for c in range(nchunk)], axis=1)
        return ref[0, :, hh * CA_HD:(hh + 1) * CA_HD]

    outs = []
    for hh in range(CA_HEADS):
        s = _nt(q_ref[0, :, hh * CA_HD:(hh + 1) * CA_HD], head(mk_ref, hh).astype(BF16))
        e = jnp.exp(s - jnp.max(s, axis=1, keepdims=True))
        p = e / jnp.sum(e, axis=1, keepdims=True)
        outs.append(_mm(p.astype(BF16), head(mv_ref, hh).astype(BF16)))
    o = jnp.concatenate(outs, axis=1).astype(BF16)
    if len(rest) == 3:
        wo_ref, res_ref, _ = rest
        o_ref[0] = res_ref[0] + _mm(o, wo_ref[...])
    else:
        o_ref[0] = o


def _cross_attn(q, mk, mv, tm, ml, tiled, wo=None, res=None):
    b, t = q.shape[:2]
    mem_block = mk.shape[1:]
    rows = lambda bb, i: (bb, i, 0)
    in_specs = [pl.BlockSpec((1, tm, D_MODEL), rows),
                pl.BlockSpec((1,) + mem_block, lambda bb, i: (bb, 0, 0)),
                pl.BlockSpec((1,) + mem_block, lambda bb, i: (bb, 0, 0))]
    args = (q, mk, mv)
    if wo is not None:
        in_specs += [pl.BlockSpec((D_MODEL, D_MODEL), lambda bb, i: (0, 0)), pl.BlockSpec((1, tm, D_MODEL), rows)]
        args += (wo, res)
    return pl.pallas_call(
        functools.partial(_ca_kernel, ml=ml, tiled=tiled),
        grid=(b, t // tm),
        in_specs=in_specs,
        out_specs=pl.BlockSpec((1, tm, D_MODEL), rows),
        out_shape=jax.ShapeDtypeStruct((b, t, D_MODEL), BF16 if wo is None else F32),
        compiler_params=_cparams(("parallel", "parallel")),
        name="cross_attn",
    )(*args)


def _mm_res_kernel(a_ref, w_ref, r_ref, o_ref):
    o_ref[...] = r_ref[...] + _mm(a_ref[...], w_ref[...])


def _mm_res(a, w, res, tm):
    m, k = a.shape
    n = w.shape[1]
    return pl.pallas_call(
        _mm_res_kernel,
        grid=(m // tm,),
        in_specs=[pl.BlockSpec((tm, k), lambda i: (i, 0)), pl.BlockSpec((k, n), lambda i: (0, 0)),
                  pl.BlockSpec((tm, n), lambda i: (i, 0))],
        out_specs=pl.BlockSpec((tm, n), lambda i: (i, 0)),
        out_shape=jax.ShapeDtypeStruct((m, n), F32),
        compiler_params=_cparams(("parallel",)),
        name="mm_res",
    )(a, w, res)


def _ffn_kernel(x_ref, gn_ref, wg_ref, wu_ref, wd_ref, gf_ref, y_ref, h_scr, acc_scr):
    j = pl.program_id(1)

    @pl.when(j == 0)
    def _():
        h_scr[...] = _rms(x_ref[...], gn_ref[...]).astype(BF16)
        acc_scr[...] = jnp.zeros_like(acc_scr)

    h = h_scr[...]
    gate = _mm(h, wg_ref[...])
    act = (gate * jax.nn.sigmoid(gate)) * _mm(h, wu_ref[...])
    acc_scr[...] += _mm(act.astype(BF16), wd_ref[...])

    @pl.when(j == pl.num_programs(1) - 1)
    def _():
        y_ref[...] = _rms(x_ref[...] + acc_scr[...], gf_ref[...])


def _ffn(x, gn, wg, wu, wd, gf, tm, th):
    m = x.shape[0]
    hid = wg.shape[1]
    return pl.pallas_call(
        _ffn_kernel,
        grid=(m // tm, hid // th),
        in_specs=[pl.BlockSpec((tm, D_MODEL), lambda i, j: (i, 0)), pl.BlockSpec((1, D_MODEL), lambda i, j: (0, 0)),
                  pl.BlockSpec((D_MODEL, th), lambda i, j: (0, j)), pl.BlockSpec((D_MODEL, th), lambda i, j: (0, j)),
                  pl.BlockSpec((th, D_MODEL), lambda i, j: (j, 0)), pl.BlockSpec((1, D_MODEL), lambda i, j: (0, 0))],
        out_specs=pl.BlockSpec((tm, D_MODEL), lambda i, j: (i, 0)),
        out_shape=jax.ShapeDtypeStruct((m, D_MODEL), F32),
        scratch_shapes=[pltpu.VMEM((tm, D_MODEL), BF16), pltpu.VMEM((tm, D_MODEL), F32)],
        compiler_params=_cparams(("parallel", "arbitrary")),
        name="ffn",
    )(x, gn, wg, wu, wd, gf)


def _pair(a, axis):
    shp = a.shape
    a = a.reshape(shp[:axis] + (NSA_KV, NSA_GROUP, HEAD_DIM) + shp[axis + 1:])
    a = jnp.swapaxes(a, axis, axis + 1)
    return a.reshape(shp)


def _rope_tables(pos):
    half = ROPE_DIM // 2
    inv = ROPE_THETA ** (-jnp.arange(half, dtype=F32) / half)
    ang = pos.astype(F32)[:, None] * inv[None, :]
    cos, sin = jnp.cos(ang), jnp.sin(ang)
    l = np.arange(LANES) % HEAD_DIM
    idx = l % half
    c = jnp.where(l < ROPE_DIM, cos[:, idx], 1.0)
    sa = jnp.where(l < half, -sin[:, idx], 0.0)
    sb = jnp.where((l >= half) & (l < ROPE_DIM), sin[:, idx], 0.0)
    return c, sa, sb


def _prep_w_in(w):
    offs = np.cumsum([0, NSA_W, 3 * NSA_HEADS] + [NSA_KV * HEAD_DIM] * 6 + [HG_W] * 4)
    seg = lambda i: w[:, offs[i]:offs[i + 1]]
    wq = _pair(seg(0), 1)
    wks = seg(4)
    z = jnp.zeros((D_MODEL, HEAD_DIM), w.dtype)
    aug = jnp.concatenate([wks[:, :HEAD_DIM], z, z, wks[:, HEAD_DIM:]], axis=1)
    gates = jnp.pad(seg(1), ((0, 0), (0, LANES - 3 * NSA_HEADS)))
    out = jnp.concatenate([wq] + [seg(i) for i in range(2, 8)] + [aug] + [seg(i) for i in range(8, 12)] + [gates],
                          axis=1)
    assert out.shape[1] == N_PROJ
    return out.astype(BF16)


def _prep_compress(pe, w1, w2):
    npair = CMP_STRIDE // 2
    pe_big = jnp.broadcast_to(pe.reshape(2, CMP_STRIDE, 1, HEAD_DIM), (2, CMP_STRIDE, NSA_KV, HEAD_DIM))
    pe_big = jnp.pad(pe_big.reshape(2, CMP_STRIDE * LANES), ((0, 6), (0, 0)))
    w1r = w1.reshape(2, npair, 2, HEAD_DIM, CMP_HID)
    eye = jnp.eye(NSA_KV, dtype=w1.dtype)
    w1_big = jnp.einsum('cjldh,ge->jlgdceh', w1r, eye).reshape(npair, 2 * LANES, 4 * CMP_HID)
    w2_big = jnp.einsum('hd,ge->ghed', w2, eye).reshape(NSA_KV * CMP_HID, LANES)
    return pe_big, w1_big.astype(BF16), w2_big.astype(BF16)


def _pad_rows(a, n):
    return jnp.pad(a, ((0, 0), (0, n - a.shape[1]), (0, 0)))


def kernel(x_prompt, x_sample, cache_cmp_k, cache_cmp_v, cache_sel_k, cache_sel_v, cache_win_k, cache_win_v,
           state_hgrn, cache_mem_k, cache_mem_v, page_table, mem_prompt, norm_mix, w_in, cmp_pe_k, cmp_w1_k,
           cmp_w2_k, cmp_pe_v, cmp_w1_v, cmp_w2_v, nsa_out_norm, hg_lb_logits, hg_out_norm, w_out, norm_ca,
           norm_mem, ca_wq, ca_wk, ca_wv, ca_wo, norm_ffn, ffn_w_gate, ffn_w_up, ffn_w_down, final_norm):
    B, T = x_prompt.shape[:2]
    DB, TS = x_sample.shape[:2]
    n_pages = page_table.shape[1]
    past = n_pages * PAGE_SIZE
    n_pool = cache_cmp_k.shape[1]
    wb = cache_win_k.shape[2]
    ml = mem_prompt.shape[1]
    assert w_in.shape[0] == 1, "single layer"
    row2 = lambda a: a.reshape(1, -1)

    w_in_p = _prep_w_in(w_in[0])
    pek, w1k, w2k = _prep_compress(cmp_pe_k[0], cmp_w1_k[0], cmp_w2_k[0])
    pev, w1v, w2v = _prep_compress(cmp_pe_v[0], cmp_w1_v[0], cmp_w2_v[0])
    lb = jnp.cumsum(jax.nn.softmax(hg_lb_logits.astype(F32), axis=0), axis=0)[0].reshape(1, HG_W)
    gn_nsa = row2(_pair(nsa_out_norm[0], 0))
    wo_a = _pair(w_out[0][:NSA_W], 0).astype(BF16)
    wo_b = w_out[0][NSA_W:].astype(BF16)
    wq_ca = ca_wq[0].astype(BF16)
    wo_ca = ca_wo[0].astype(BF16)
    w_mem = jnp.concatenate([ca_wk[0], ca_wv[0]], axis=1).astype(BF16)
    wg, wu, wd = ffn_w_gate[0].astype(BF16), ffn_w_up[0].astype(BF16), ffn_w_down[0].astype(BF16)
    g_mix, g_ca, g_ffn, g_fin = row2(norm_mix[0]), row2(norm_ca[0]), row2(norm_ffn[0]), row2(final_norm)
    g_hg = row2(hg_out_norm[0])

    tabs_p = _rope_tables(jnp.arange(T))
    tabs_s = _rope_tables(past + (jnp.arange(DB * TS) % TS))
    blk = np.arange(T) // SEL_BLK
    lanes2 = np.arange(2 * LANES)
    oh_np = ((lanes2[None, :] >= HEAD_DIM) & (lanes2[None, :] < 3 * HEAD_DIM)
             & ((lanes2[None, :] - HEAD_DIM) % HEAD_DIM == blk[:, None]))
    oh_p = jnp.asarray(oh_np.astype(np.float32))
    oh_s = jnp.zeros((DB * TS, 2 * LANES), F32)

    def tail(x, o_nsa, o_hg, mk, mv, nb, tm, tiled):
        rows = x.shape[0]
        x1, qca = _mixout(x, o_nsa, o_hg, gn_nsa, wo_a, wo_b, g_ca, wq_ca, tm)
        per = rows // nb
        tq = min(per, ROW_TILE)
        if per % 16:
            tq = -(-per // 16) * 16
            qca3 = _pad_rows(qca.reshape(nb, per, D_MODEL), tq)
            oca = _cross_attn(qca3, mk, mv, tq, ml, tiled)[:, :per].reshape(rows, D_MODEL)
            x2 = _mm_res(oca, wo_ca, x1, tm)
        else:
            x2 = _cross_attn(qca.reshape(nb, per, D_MODEL), mk, mv, tq, ml, tiled, wo_ca,
                             x1.reshape(nb, per, D_MODEL)).reshape(rows, D_MODEL)
        return _ffn(x2, g_ffn, wg, wu, wd, g_fin, min(rows, FFN_ROW_TILE), wg.shape[1] // 2)

    M = B * T
    (q, qr, kcp, vcp, _, _, _, _, ksa, kwb, vwb, hm, gt, kct, vct, kst, vst, kwt, vwt) = _proj_in(
        x_prompt.reshape(M, D_MODEL), g_mix, w_in_p, tabs_p, oh_p, ROW_TILE, B)
    r3 = lambda a: a.reshape(B, T, -1)
    cmp_w = (pek, pev, w1k, w1v, w2k, w2v)
    kc_p, vct_p = _compress_rows(r3(kcp), r3(vcp), cmp_w)
    o_nsa = _nsa_prompt(r3(q), r3(qr), r3(gt), kc_p, vct_p, r3(ksa), vst, r3(kwb), vwt, NSA_Q_TILE, NSA_KEY_TILE)
    o_hg, s_p = _hgrn(r3(hm), lb, g_hg, jnp.zeros((B, HG_HEADS, HG_DK, HG_DV), F32), HG_CHUNK, T)
    mkv = _norm_mm(mem_prompt.reshape(B * ml, D_MODEL), row2(norm_mem[0]), w_mem, ROW_TILE)
    mk_p = mkv[:, :D_MODEL].reshape(B, ml, D_MODEL)
    mv_p = mkv[:, D_MODEL:].reshape(B, ml, D_MODEL)
    y_p = tail(x_prompt.reshape(M, D_MODEL), o_nsa.reshape(M, NSA_W), o_hg.reshape(M, HG_W), mk_p, mv_p, B, ROW_TILE, False)

    kv5 = lambda a, n: a.reshape(1, -1, n, NSA_KV, HEAD_DIM)
    wbp = min(WINDOW, T)
    nat = lambda a: jnp.transpose(a.reshape(B, NSA_KV, HEAD_DIM, -1), (0, 3, 1, 2))[None]
    outs_p = (nat(kct), nat(vct), nat(kst), nat(vst),
              nat(kwt[:, :, T - wbp:]), nat(vwt[:, :, T - wbp:]), s_p[None],
              mk_p.reshape(1, B, ml, CA_HEADS, CA_HD), mv_p.reshape(1, B, ml, CA_HEADS, CA_HD))

    MS = DB * TS
    (q, qr, kcs, vcs, kss, vss, kws, vws, _, _, _, hm, gt) = _proj_in(
        x_sample.reshape(MS, D_MODEL), g_mix, w_in_p, tabs_s, oh_s, MS, 1)[:13]
    pt_flat = page_table.reshape(-1).astype(I32)
    pool_t = lambda a: jnp.transpose(a[0], (0, 2, 3, 1)).reshape(n_pool, LANES, PAGE_SIZE)
    kc_s, vc_s = _compress_paged(pt_flat, pool_t(cache_cmp_k), pool_t(cache_cmp_v), cmp_w, DB, n_pages)
    tp, tn = SAMPLE_Q_PAD, SAMPLE_KEY_PAD
    s3 = lambda a, n: _pad_rows(a.reshape(DB, TS, -1), n)
    key_blk = np.arange(past) // SEL_BLK
    oh_keys = jnp.asarray((np.arange(LANES)[:, None] == key_blk[None, :]).astype(np.float32)).astype(BF16)
    nseg = past // CMP_STRIDE
    gmat = jnp.asarray((np.arange(nseg)[:, None] // (SEL_BLK // CMP_STRIDE)
                        == np.arange(LANES)[None, :]).astype(np.float32))
    o_nsa_s, nwk, nwv = _nsa_sample(
        pt_flat, s3(q, tp), s3(qr, tp), s3(gt, tp), kc_s, vc_s, s3(kss, tn), s3(vss, tn), s3(kws, tn), s3(vws, tn),
        cache_win_k[0].reshape(DB, wb, LANES), cache_win_v[0].reshape(DB, wb, LANES), oh_keys, gmat,
        pool_t(cache_sel_k), pool_t(cache_sel_v), n_pages, past, TS)
    o_hg_s, s_s = _hgrn(s3(hm, SAMPLE_Q_PAD), lb, g_hg, state_hgrn[0].astype(F32), SAMPLE_Q_PAD, TS)
    nchunk = CA_HD // LANES
    mem_t = lambda a: jnp.transpose(a[0].reshape(DB, ml, CA_HEADS, nchunk, LANES), (0, 1, 3, 2, 4)).reshape(
        DB, ml * nchunk * CA_HEADS, LANES)
    y_s = tail(x_sample.reshape(MS, D_MODEL), o_nsa_s[:, :TS].reshape(MS, NSA_W), o_hg_s[:, :TS].reshape(MS, HG_W),
               mem_t(cache_mem_k), mem_t(cache_mem_v), DB, MS, True)

    outs_s = (kv5(kcs, TS), kv5(vcs, TS), kv5(kss, TS), kv5(vss, TS),
              nwk.reshape(1, DB, wb, NSA_KV, HEAD_DIM), nwv.reshape(1, DB, wb, NSA_KV, HEAD_DIM), s_s[None])
    return (y_p.reshape(B, T, D_MODEL), y_s.reshape(DB, TS, D_MODEL)) + outs_p + outs_s
```

```python
import functools

import numpy as np
import jax
import jax.numpy as jnp
from jax import lax
from jax.experimental import pallas as pl
from jax.experimental.pallas import tpu as pltpu

F32 = jnp.float32
BF16 = jnp.bfloat16
I32 = jnp.int32

D_MODEL = 1024
NSA_HEADS = 8
NSA_KV = 2
HEAD_DIM = 64
NSA_GROUP = NSA_HEADS // NSA_KV
NSA_W = NSA_HEADS * HEAD_DIM
CMP_STRIDE = 16
CMP_LEN = 32
CMP_HID = 128
SEL_BLK = 64
SEL_TOPK = 16
WINDOW = 512
ROPE_THETA = 500000.0
ROPE_DIM = HEAD_DIM // 4
HG_HEADS = 4
HG_DK = 128
HG_DV = 128
HG_W = HG_HEADS * HG_DV
CA_HEADS = 4
CA_HD = D_MODEL // CA_HEADS
PAGE_SIZE = 128
EPS = 1e-6
FORCE_SCORE = 1e4
NEG = -1e30
MASK_BIAS = -1e9
LANES = 128
VMEM_LIMIT = 56 * 1024 * 1024
ROW_TILE = 256
PROJ_ROW_TILE = 512
FFN_ROW_TILE = 512
NSA_Q_TILE = 128
NSA_KEY_TILE = 512
RANK_UNROLL = 8
PAGE_UNROLL = 16
HG_CHUNK = 128
SAMPLE_Q_PAD = 8
SAMPLE_KEY_PAD = 16

C_Q = 0
C_KC, C_VC, C_KS, C_VS, C_KW, C_VW = 512, 640, 768, 896, 1024, 1152
C_AUG = 1280
C_HG = 1536
C_GT = 3584
N_PROJ = 3712


def _cparams(sem):
    return pltpu.CompilerParams(dimension_semantics=sem, vmem_limit_bytes=VMEM_LIMIT)


def _nt(a, b):
    return lax.dot_general(a, b, (((1,), (1,)), ((), ())), preferred_element_type=F32)


def _tn(a, b):
    return lax.dot_general(a, b, (((0,), (0,)), ((), ())), preferred_element_type=F32)


def _mm(a, b):
    return jnp.dot(a, b, preferred_element_type=F32)


def _rms(x, g):
    return x * lax.rsqrt(jnp.mean(x * x, axis=-1, keepdims=True) + EPS) * g


def _masked_softmax(s, mask, axis):
    s = jnp.where(mask, s, NEG)
    m = jnp.max(s, axis=axis, keepdims=True)
    e = jnp.where(mask, jnp.exp(s - m), 0.0)
    return e / jnp.maximum(jnp.sum(e, axis=axis, keepdims=True), 1e-20)


def _log2(n):
    l = int(n).bit_length() - 1
    assert (1 << l) == n, n
    return l


def _proj_in_kernel(x_ref, g_ref, w_ref, c_ref, sa_ref, sb_ref, oh_ref,
                    q_ref, qr_ref, kc_ref, vc_ref, ks_ref, vs_ref, kw_ref, vw_ref,
                    ksa_ref, kwb_ref, vwb_ref, hm_ref, gt_ref,
                    kct_ref, vct_ref, kst_ref, vst_ref, kwt_ref, vwt_ref):
    xn = _rms(x_ref[...], g_ref[...]).astype(BF16)
    c, sa, sb = c_ref[...], sa_ref[...], sb_ref[...]

    def mm(lo, hi):
        return _mm(xn, w_ref[:, lo:hi])

    def rope(y):
        n = y.shape[1] // LANES
        cc, aa, bb = (jnp.concatenate([t] * n, axis=1) if n > 1 else t for t in (c, sa, sb))
        w = y.shape[1]
        return y * cc + pltpu.roll(y, w - ROPE_DIM // 2, 1) * aa + pltpu.roll(y, ROPE_DIM // 2, 1) * bb

    yq = mm(C_Q, C_Q + NSA_W) * (HEAD_DIM ** -0.5)
    q_ref[...] = yq.astype(BF16)
    qr_ref[...] = rope(yq).astype(BF16)
    kw = vw = None
    for col, roped, ref, tref in ((C_KC, False, kc_ref, kct_ref), (C_VC, False, vc_ref, vct_ref),
                                  (C_KS, True, ks_ref, kst_ref), (C_VS, False, vs_ref, vst_ref),
                                  (C_KW, True, kw_ref, kwt_ref), (C_VW, False, vw_ref, vwt_ref)):
        y = mm(col, col + LANES)
        y = rope(y) if roped else y
        ref[...] = y
        tref[0] = y.T
        kw = y if col == C_KW else kw
        vw = y if col == C_VW else vw
    kwb_ref[...] = kw.astype(BF16)
    vwb_ref[...] = vw.astype(BF16)
    ksa_ref[...] = (rope(mm(C_AUG, C_AUG + 2 * LANES)) + oh_ref[...]).astype(BF16)
    hm_ref[...] = mm(C_HG, C_HG + 4 * HG_W)
    gt_ref[...] = mm(C_GT, C_GT + LANES)


def _proj_in(x2d, g, w, tabs, oh, tm, nb):
    m = x2d.shape[0]
    per = m // nb
    nt_tab = tabs[0].shape[0] // tm
    nt_seq = per // tm
    row = lambda i: (i, 0)
    tab = lambda i: (i % nt_tab, 0)
    const = lambda i: (0, 0)
    widths = [(NSA_W, BF16), (NSA_W, BF16)] + [(LANES, F32)] * 6 + [(2 * LANES, BF16)] + [(LANES, BF16)] * 2 \
        + [(4 * HG_W, F32), (LANES, F32)]
    return pl.pallas_call(
        _proj_in_kernel,
        grid=(m // tm,),
        in_specs=[pl.BlockSpec((tm, D_MODEL), row), pl.BlockSpec((1, D_MODEL), const),
                  pl.BlockSpec((D_MODEL, N_PROJ), const),
                  pl.BlockSpec((tm, LANES), tab), pl.BlockSpec((tm, LANES), tab), pl.BlockSpec((tm, LANES), tab),
                  pl.BlockSpec((tm, 2 * LANES), tab)],
        out_specs=[pl.BlockSpec((tm, wd), row) for wd, _ in widths]
        + [pl.BlockSpec((1, LANES, tm), lambda i: (i // nt_seq, 0, i % nt_seq))] * 6,
        out_shape=[jax.ShapeDtypeStruct((m, wd), dt) for wd, dt in widths]
        + [jax.ShapeDtypeStruct((nb, LANES, per), F32)] * 6,
        compiler_params=_cparams(("parallel",)),
        name="proj_in",
    )(x2d, g, w, *tabs, oh)


def _page_copy(pool_hbm, page, buf, slot, p, sem):
    cols = pl.ds(pl.multiple_of(p * PAGE_SIZE, PAGE_SIZE), PAGE_SIZE)
    return pltpu.make_async_copy(pool_hbm.at[page], buf.at[slot, :, cols], sem)


def _gather_start(pt_ref, pools, bufs, sems, b, slot, n_pages):
    def body(p, carry):
        page = pt_ref[b * n_pages + p]
        for pool, buf, sem in zip(pools, bufs, sems):
            _page_copy(pool, page, buf, slot, p, sem.at[slot]).start()
        return carry
    lax.fori_loop(0, n_pages, body, 0, unroll=PAGE_UNROLL)


def _gather_wait(pools, bufs, sems, slot, n_pages):
    def body(p, carry):
        for pool, buf, sem in zip(pools, bufs, sems):
            _page_copy(pool, 0, buf, slot, p, sem.at[slot]).wait()
        return carry
    lax.fori_loop(0, n_pages, body, 0, unroll=PAGE_UNROLL)


def _gather_step(pt_ref, pools, bufs, sems, n_pages):
    b = pl.program_id(0)
    nb = pl.num_programs(0)
    slot = b % 2

    @pl.when(b == 0)
    def _():
        _gather_start(pt_ref, pools, bufs, sems, 0, 0, n_pages)

    @pl.when(b + 1 < nb)
    def _():
        _gather_start(pt_ref, pools, bufs, sems, b + 1, 1 - slot, n_pages)

    _gather_wait(pools, bufs, sems, slot, n_pages)
    return slot


def _compress_core(tok, pe_ref, w1_ref, w2_ref, bias_scr, first, out, nseg, transposed_out=False):
    npair = CMP_STRIDE // 2
    nh = NSA_KV * CMP_HID

    @pl.when(first)
    def _():
        acc = _mm(pe_ref[:, 0:2 * LANES].astype(BF16), w1_ref[0])
        for jj in range(1, npair):
            acc = acc + _mm(pe_ref[:, jj * 2 * LANES:(jj + 1) * 2 * LANES].astype(BF16), w1_ref[jj])
        bias_scr[...] = acc

    acc = None
    for jj in range(npair):
        xp = jnp.concatenate([tok[pl.ds(2 * jj, nseg, stride=CMP_STRIDE), :],
                              tok[pl.ds(2 * jj + 1, nseg, stride=CMP_STRIDE), :]], axis=1).astype(BF16)
        part = _mm(xp, w1_ref[jj])
        acc = part if acc is None else acc + part
    ha = acc[:, 0:nh] + bias_scr[0:1, 0:nh]
    hb = acc[:, nh:2 * nh] + bias_scr[1:2, nh:2 * nh]
    h = jax.nn.gelu(ha + pltpu.roll(hb, nseg - 1, 0))
    row = lax.broadcasted_iota(I32, (nseg, 1), 0)
    h = jnp.where(row < nseg - 1, h, 0.0)
    res = _mm(h.astype(BF16), w2_ref[...]).astype(BF16)
    out[...] = res.T if transposed_out else res


def _compress_rows_kernel(k_ref, v_ref, pek_ref, pev_ref, w1k_ref, w1v_ref, w2k_ref, w2v_ref,
                          kc_ref, vct_ref, bias_scr, *, nseg):
    first = pl.program_id(0) == 0
    for i, (tok, pe, w1, w2, out) in enumerate(((k_ref, pek_ref, w1k_ref, w2k_ref, kc_ref),
                                                (v_ref, pev_ref, w1v_ref, w2v_ref, vct_ref))):
        _compress_core(tok.at[0], pe, w1, w2, bias_scr.at[i], first, out.at[0], nseg, transposed_out=i == 1)


def _compress_paged_kernel(pt_ref, poolk_hbm, poolv_hbm, pek_ref, pev_ref, w1k_ref, w1v_ref, w2k_ref, w2v_ref,
                           kc_ref, vc_ref, xk_buf, xv_buf, tokk_scr, tokv_scr, bias_scr, semk, semv, *, n_pages):
    slot = _gather_step(pt_ref, (poolk_hbm, poolv_hbm), (xk_buf, xv_buf), (semk, semv), n_pages)
    first = pl.program_id(0) == 0
    nseg = n_pages * (PAGE_SIZE // CMP_STRIDE)
    for i, (xbuf, tok, pe, w1, w2, out) in enumerate(((xk_buf, tokk_scr, pek_ref, w1k_ref, w2k_ref, kc_ref),
                                                      (xv_buf, tokv_scr, pev_ref, w1v_ref, w2v_ref, vc_ref))):
        tok[...] = xbuf[slot].T
        _compress_core(tok, pe, w1, w2, bias_scr.at[i], first, out.at[0], nseg)


def _compress_weight_specs(const2, const3):
    npair = CMP_STRIDE // 2
    return [pl.BlockSpec((8, CMP_STRIDE * LANES), const2), pl.BlockSpec((8, CMP_STRIDE * LANES), const2),
            pl.BlockSpec((npair, 2 * LANES, 4 * CMP_HID), const3), pl.BlockSpec((npair, 2 * LANES, 4 * CMP_HID), const3),
            pl.BlockSpec((NSA_KV * CMP_HID, LANES), const2), pl.BlockSpec((NSA_KV * CMP_HID, LANES), const2)]


def _compress_rows(k_rows, v_rows, weights):
    nb, ntok = k_rows.shape[:2]
    nseg = ntok // CMP_STRIDE
    per_b = lambda b: (b, 0, 0)
    return pl.pallas_call(
        functools.partial(_compress_rows_kernel, nseg=nseg),
        grid=(nb,),
        in_specs=[pl.BlockSpec((1, ntok, LANES), per_b), pl.BlockSpec((1, ntok, LANES), per_b)]
        + _compress_weight_specs(lambda b: (0, 0), lambda b: (0, 0, 0)),
        out_specs=[pl.BlockSpec((1, nseg, LANES), per_b), pl.BlockSpec((1, LANES, nseg), per_b)],
        out_shape=[jax.ShapeDtypeStruct((nb, nseg, LANES), BF16), jax.ShapeDtypeStruct((nb, LANES, nseg), BF16)],
        scratch_shapes=[pltpu.VMEM((2, 8, 4 * CMP_HID), F32)],
        compiler_params=_cparams(("arbitrary",)),
        name="compress_rows",
    )(k_rows, v_rows, *weights)


def _compress_paged(page_table_flat, pool_k, pool_v, weights, n_batch, n_pages):
    past = n_pages * PAGE_SIZE
    nseg = past // CMP_STRIDE
    gs = pltpu.PrefetchScalarGridSpec(
        num_scalar_prefetch=1,
        grid=(n_batch,),
        in_specs=[pl.BlockSpec(memory_space=pl.ANY), pl.BlockSpec(memory_space=pl.ANY)]
        + _compress_weight_specs(lambda b, pt: (0, 0), lambda b, pt: (0, 0, 0)),
        out_specs=[pl.BlockSpec((1, nseg, LANES), lambda b, pt: (b, 0, 0))] * 2,
        scratch_shapes=[pltpu.VMEM((2, LANES, past), F32), pltpu.VMEM((2, LANES, past), F32),
                        pltpu.VMEM((past, LANES), F32), pltpu.VMEM((past, LANES), F32),
                        pltpu.VMEM((2, 8, 4 * CMP_HID), F32),
                        pltpu.SemaphoreType.DMA((2,)), pltpu.SemaphoreType.DMA((2,))],
    )
    return pl.pallas_call(
        functools.partial(_compress_paged_kernel, n_pages=n_pages),
        grid_spec=gs,
        out_shape=[jax.ShapeDtypeStruct((n_batch, nseg, LANES), BF16)] * 2,
        compiler_params=_cparams(("arbitrary",)),
        name="compress_paged",
    )(page_table_flat, pool_k, pool_v, *weights)


def _nsa_prompt_kernel(q_ref, qr_ref, gt_ref, kc_ref, vct_ref, ksa_ref, vst_ref, kw_ref, vwt_ref, o_ref,
                       psum_scr, sc_scr, *, tq, tk, ncp, nsel, topk):
    i = pl.program_id(1)
    s0 = i * tq
    rows = NSA_GROUP * tq
    hk = tk // 2
    wl = WINDOW + tq
    upper = lax.broadcasted_iota(I32, (tq, LANES), 1) >= HEAD_DIM
    kc = kc_ref[0]
    t_cols = s0 + (lax.broadcasted_iota(I32, (1, rows), 1) & (tq - 1))
    cmask = lax.broadcasted_iota(I32, (ncp, rows), 0) * CMP_STRIDE + (CMP_LEN - 1) <= t_cols
    ws = pl.multiple_of(jnp.maximum(s0 - WINDOW, 0), tq)
    kpos_w = ws + lax.broadcasted_iota(I32, (wl, rows), 0)
    wmask = (kpos_w <= t_cols) & (kpos_w > t_cols - WINDOW)
    kw_t = kw_ref[0, pl.ds(ws, wl), :]
    o_ct, o_wt, o_st = [], [], []
    for g in range(NSA_KV):
        mine = upper if g == 1 else jnp.logical_not(upper)

        def stack(ref, fill):
            return jnp.concatenate(
                [jnp.where(mine, ref[0, :, r * LANES:(r + 1) * LANES], fill) for r in range(NSA_GROUP)], axis=0)

        def ones_rows(v):
            up = lax.broadcasted_iota(I32, v.shape, 0) >= HEAD_DIM
            return jnp.where(up if g == 1 else jnp.logical_not(up), v, 1.0).astype(BF16)

        def dim_rows(x):
            return x[g * HEAD_DIM:(g + 1) * HEAD_DIM]

        def den_rows(x):
            return x[(1 - g) * HEAD_DIM:(2 - g) * HEAD_DIM]

        p_c = _masked_softmax(_nt(kc, stack(q_ref, 0)), cmask, 0)
        o_ct.append(dim_rows(_mm(vct_ref[0], p_c.astype(BF16))))
        psum = p_c[:, 0:tq]
        for r in range(1, NSA_GROUP):
            psum = psum + p_c[:, r * tq:(r + 1) * tq]
        psum_scr[...] = psum
        cps = SEL_BLK // CMP_STRIDE
        imp = psum_scr[pl.ds(0, nsel, stride=cps), :]
        for u in range(1, cps):
            imp = imp + psum_scr[pl.ds(u, nsel, stride=cps), :]
        j = lax.broadcasted_iota(I32, (nsel, tq), 0)
        cur = (s0 + lax.broadcasted_iota(I32, (nsel, tq), 1)) >> _log2(SEL_BLK)
        avail = j <= cur
        forced = (j == 0) | (j == cur) | (j == cur - 1)
        score = jnp.where(avail, jnp.where(forced, FORCE_SCORE, imp), -FORCE_SCORE)
        sc_scr[...] = score

        def rank_body(grp, cnt):
            for u in range(RANK_UNROLL):
                jp = grp * RANK_UNROLL + u
                other = sc_scr[pl.ds(jp, 1), :]
                beats = (other > score) | ((other == score) & (jp < j))
                cnt = cnt + beats.astype(I32)
            return cnt

        n_live = ((s0 + tq - 1) >> _log2(SEL_BLK)) + 1
        n_grp = jnp.minimum((n_live + RANK_UNROLL - 1) // RANK_UNROLL, nsel // RANK_UNROLL)
        cnt = lax.fori_loop(0, n_grp, rank_body, jnp.zeros((nsel, tq), I32))
        bias_t = jnp.where((cnt < topk) & avail, 0.0, MASK_BIAS)
        if nsel < HEAD_DIM:
            bias_t = jnp.concatenate([bias_t, jnp.full((HEAD_DIM - nsel, tq), MASK_BIAS, F32)], axis=0)
        bias = jnp.concatenate([bias_t, bias_t], axis=0).T.astype(BF16)
        qs = stack(qr_ref, bias)
        glanes = slice(g * LANES, (g + 1) * LANES)

        def scores(k0):
            return _nt(ksa_ref[0, pl.ds(k0, tk), glanes], qs)

        def col_max(s):
            return tuple(jnp.max(s[c * hk:(c + 1) * hk], axis=0, keepdims=True) for c in range(2))

        def update(s, mx, k0, stats):
            new = []
            for c, (m, acc) in enumerate(stats):
                v_one = ones_rows(vst_ref[0, :, pl.ds(k0 + c * hk, hk)])
                m_new = jnp.maximum(m, mx[c])
                p = jnp.exp((s[c * hk:(c + 1) * hk] - m_new).astype(BF16))
                new.append((m_new, jnp.exp(m - m_new) * acc + _mm(v_one, p)))
            return tuple(new)

        def sel_body(kk, carry):
            s_cur, mx_cur, stats = carry
            k0 = pl.multiple_of(kk * tk, tk)
            s_next = scores(k0 + tk)
            stats = update(s_cur, mx_cur, k0, stats)
            return s_next, col_max(s_next), stats

        n_full = s0 // tk
        stats0 = tuple((jnp.full((1, rows), NEG, F32), jnp.zeros((LANES, rows), F32)) for _ in range(2))
        s_first = scores(0)
        s_last, _, stats = lax.fori_loop(0, n_full, sel_body, (s_first, col_max(s_first), stats0))
        k_last = pl.multiple_of(n_full * tk, tk)
        causal = k_last + lax.broadcasted_iota(I32, (tk, rows), 0) <= t_cols
        s_last = jnp.where(causal, s_last, NEG)
        (m_a, acc_a), (m_b, acc_b) = update(s_last, col_max(s_last), k_last, stats)
        m_s = jnp.maximum(m_a, m_b)
        acc_s = jnp.exp(m_a - m_s) * acc_a + jnp.exp(m_b - m_s) * acc_b
        o_st.append(dim_rows(acc_s) / jnp.maximum(den_rows(acc_s), 1e-20))
        s_w = jnp.where(wmask, _nt(kw_t, stack(qr_ref, 0)), NEG)
        e_w = jnp.exp((s_w - jnp.max(s_w, axis=0, keepdims=True)).astype(BF16))
        acc_w = _mm(ones_rows(vwt_ref[0, :, pl.ds(ws, wl)]), e_w)
        o_wt.append(dim_rows(acc_w) / jnp.maximum(den_rows(acc_w), 1e-20))
    gsig_t = jax.nn.sigmoid(gt_ref[0]).T
    out_slots = []
    for r in range(NSA_GROUP):
        rs = slice(r * tq, (r + 1) * tq)
        halves = []
        for g in range(NSA_KV):
            c0 = 3 * (g * NSA_GROUP + r)
            halves.append(gsig_t[c0:c0 + 1] * o_ct[g][:, rs] + gsig_t[c0 + 1:c0 + 2] * o_st[g][:, rs]
                          + gsig_t[c0 + 2:c0 + 3] * o_wt[g][:, rs])
        out_slots.append(jnp.concatenate(halves, axis=0).T)
    o_ref[0] = jnp.concatenate(out_slots, axis=1)


def _nsa_prompt(q, qr, gt, kc, vct, ksa, vst, kwb, vwt, tq, tk):
    b, t = q.shape[:2]
    ncp = t // CMP_STRIDE
    nsel = t // SEL_BLK
    assert t % tk == 0 and tk % tq == 0 and t >= WINDOW + tq and nsel <= HEAD_DIM and WINDOW % tq == 0
    assert nsel % RANK_UNROLL == 0
    blk = lambda bb, i: (bb, i, 0)
    whole = lambda bb, i: (bb, 0, 0)
    return pl.pallas_call(
        functools.partial(_nsa_prompt_kernel, tq=tq, tk=tk, ncp=ncp, nsel=nsel, topk=min(SEL_TOPK, nsel)),
        grid=(b, t // tq),
        in_specs=[pl.BlockSpec((1, tq, NSA_W), blk), pl.BlockSpec((1, tq, NSA_W), blk),
                  pl.BlockSpec((1, tq, LANES), blk),
                  pl.BlockSpec((1, ncp, LANES), whole), pl.BlockSpec((1, LANES, ncp), whole),
                  pl.BlockSpec((1, t, 2 * LANES), whole), pl.BlockSpec((1, LANES, t), whole),
                  pl.BlockSpec((1, t, LANES), whole), pl.BlockSpec((1, LANES, t), whole)],
        out_specs=pl.BlockSpec((1, tq, NSA_W), blk),
        out_shape=jax.ShapeDtypeStruct((b, t, NSA_W), F32),
        scratch_shapes=[pltpu.VMEM((ncp, tq), F32), pltpu.VMEM((nsel, tq), F32)],
        compiler_params=_cparams(("parallel", "parallel")),
        name="nsa_prompt",
    )(q, qr, gt, kc, vct, ksa, vst, kwb, vwt)


def _nsa_sample_kernel(pt_ref, q_ref, qr_ref, gt_ref, kc_ref, vc_ref, ksn_ref, vsn_ref, kwn_ref, vwn_ref,
                       wink_ref, winv_ref, oht_ref, gmat_ref, poolk_hbm, poolv_hbm,
                       o_ref, nwk_ref, nwv_ref, kbuf, vbuf, semk, semv,
                       *, n_pages, past, ts, tp, tn, wb, topk_past):
    slot = _gather_step(pt_ref, (poolk_hbm, poolv_hbm), (kbuf, vbuf), (semk, semv), n_pages)
    nseg = n_pages * (PAGE_SIZE // CMP_STRIDE)
    npb = past // SEL_BLK
    rows = NSA_HEADS * tp
    lane = lax.broadcasted_iota(I32, (tp, LANES), 1)
    upper = lane >= HEAD_DIM

    def stack(ref):
        parts = []
        for g in range(NSA_KV):
            mine = upper if g == 1 else jnp.logical_not(upper)
            parts += [jnp.where(mine, ref[0, :, r * LANES:(r + 1) * LANES], 0) for r in range(NSA_GROUP)]
        return jnp.concatenate(parts, axis=0)

    t_rows = lax.broadcasted_iota(I32, (rows, 1), 0) & (tp - 1)
    qc = stack(q_ref)
    cend = lax.broadcasted_iota(I32, (rows, nseg), 1) * CMP_STRIDE + (CMP_LEN - 1)
    p_c = _masked_softmax(_nt(qc, kc_ref[0]), cend <= past + t_rows, 1)
    o_c = _mm(p_c.astype(BF16), vc_ref[0])
    impn = []
    for g in range(NSA_KV):
        acc = p_c[(g * NSA_GROUP) * tp:(g * NSA_GROUP + 1) * tp]
        for r in range(1, NSA_GROUP):
            acc = acc + p_c[(g * NSA_GROUP + r) * tp:(g * NSA_GROUP + r + 1) * tp]
        impn.append(acc)
    impn = jnp.concatenate(impn, axis=0)
    imp = jnp.dot(impn, gmat_ref[...], precision=lax.Precision.HIGHEST, preferred_element_type=F32)
    j = lax.broadcasted_iota(I32, (NSA_KV * tp, LANES), 1)
    avail = j < npb
    forced = (j == 0) | (j == npb - 1)
    score = jnp.where(avail, jnp.where(forced, FORCE_SCORE, imp), -FORCE_SCORE)
    cnt = jnp.zeros((NSA_KV * tp, LANES), I32)
    for jp in range(npb):
        other = score[:, jp:jp + 1]
        beats = (other > score) | ((other == score) & (jp < j))
        cnt = cnt + beats.astype(I32)
    bias = jnp.where((cnt < topk_past) & avail, 0.0, MASK_BIAS).astype(BF16)
    bias = jnp.concatenate([bias[0:tp]] * NSA_GROUP + [bias[tp:2 * tp]] * NSA_GROUP, axis=0)
    qs = stack(qr_ref)
    key_t = lax.broadcasted_iota(I32, (rows, tn), 1)
    new_mask = (key_t <= t_rows) & (key_t < ts)
    s_p = _mm(qs, kbuf[slot].astype(BF16)) + _mm(bias, oht_ref[...])
    s_n = jnp.where(new_mask, _nt(qs, ksn_ref[0].astype(BF16)), NEG)
    m_s = jnp.maximum(jnp.max(s_p, axis=1, keepdims=True), jnp.max(s_n, axis=1, keepdims=True))
    e_p = jnp.exp(s_p - m_s)
    e_n = jnp.where(new_mask, jnp.exp(s_n - m_s), 0.0)
    l_s = jnp.sum(e_p, axis=1, keepdims=True) + jnp.sum(e_n, axis=1, keepdims=True)
    o_s = (_nt(e_p.astype(BF16), vbuf[slot].astype(BF16)) + _mm(e_n.astype(BF16), vsn_ref[0].astype(BF16))) \
        / jnp.maximum(l_s, 1e-20)
    wk = wink_ref[0]
    wv = winv_ref[0]
    iw = lax.broadcasted_iota(I32, (rows, wb), 1)
    wmask = (iw > t_rows + (wb - WINDOW)) & (iw >= wb - past)
    s_w = jnp.where(wmask, _nt(qs, wk.astype(BF16)), NEG)
    s_wn = jnp.where(new_mask, _nt(qs, kwn_ref[0].astype(BF16)), NEG)
    m_w = jnp.maximum(jnp.max(s_w, axis=1, keepdims=True), jnp.max(s_wn, axis=1, keepdims=True))
    e_w = jnp.where(wmask, jnp.exp(s_w - m_w), 0.0)
    e_wn = jnp.where(new_mask, jnp.exp(s_wn - m_w), 0.0)
    inv = 1.0 / jnp.maximum(jnp.sum(e_w, axis=1, keepdims=True) + jnp.sum(e_wn, axis=1, keepdims=True), 1e-20)
    o_w = _mm((e_w * inv).astype(BF16), wv.astype(BF16)) + _mm((e_wn * inv).astype(BF16), vwn_ref[0].astype(BF16))
    gsig = jax.nn.sigmoid(gt_ref[0])
    slots = []
    for r in range(NSA_GROUP):
        per_g = []
        for g in range(NSA_KV):
            h = g * NSA_GROUP + r
            rs = slice(h * tp, (h + 1) * tp)
            per_g.append(gsig[:, 3 * h:3 * h + 1] * o_c[rs] + gsig[:, 3 * h + 1:3 * h + 2] * o_s[rs]
                         + gsig[:, 3 * h + 2:3 * h + 3] * o_w[rs])
        slots.append(jnp.where(upper, per_g[1], per_g[0]))
    o_ref[0] = jnp.concatenate(slots, axis=1)
    sub = lax.broadcasted_iota(I32, (8, LANES), 0)
    for win, new_ref, out in ((wk, kwn_ref, nwk_ref), (wv, vwn_ref, nwv_ref)):
        rolled = pltpu.roll(win, wb - ts, 0)
        shifted = pltpu.roll(new_ref[0, 0:8, :], 8 - ts, 0)
        out[0, 0:wb - 8, :] = rolled[0:wb - 8]
        out[0, wb - 8:wb, :] = jnp.where(sub >= 8 - ts, shifted, rolled[wb - 8:wb])


def _nsa_sample(page_table_flat, q, qr, gt, kc, vc, ksn, vsn, kwn, vwn, win_k, win_v, oh, gmat, pool_k, pool_v,
                n_pages, past, ts):
    db, tp = q.shape[:2]
    tn = ksn.shape[1]
    wb = win_k.shape[1]
    nseg = n_pages * (PAGE_SIZE // CMP_STRIDE)
    npb = past // SEL_BLK
    assert ts <= 8 and npb <= LANES and wb >= 8 and past % PAGE_SIZE == 0
    per_b = lambda b, pt: (b, 0, 0)
    const2 = lambda b, pt: (0, 0)
    gs = pltpu.PrefetchScalarGridSpec(
        num_scalar_prefetch=1,
        grid=(db,),
        in_specs=[pl.BlockSpec((1, tp, NSA_W), per_b), pl.BlockSpec((1, tp, NSA_W), per_b),
                  pl.BlockSpec((1, tp, LANES), per_b),
                  pl.BlockSpec((1, nseg, LANES), per_b), pl.BlockSpec((1, nseg, LANES), per_b),
                  pl.BlockSpec((1, tn, LANES), per_b), pl.BlockSpec((1, tn, LANES), per_b),
                  pl.BlockSpec((1, tn, LANES), per_b), pl.BlockSpec((1, tn, LANES), per_b),
                  pl.BlockSpec((1, wb, LANES), per_b), pl.BlockSpec((1, wb, LANES), per_b),
                  pl.BlockSpec((LANES, past), const2), pl.BlockSpec((nseg, LANES), const2),
                  pl.BlockSpec(memory_space=pl.ANY), pl.BlockSpec(memory_space=pl.ANY)],
        out_specs=[pl.BlockSpec((1, tp, NSA_W), per_b), pl.BlockSpec((1, wb, LANES), per_b),
                   pl.BlockSpec((1, wb, LANES), per_b)],
        scratch_shapes=[pltpu.VMEM((2, LANES, past), F32), pltpu.VMEM((2, LANES, past), F32),
                        pltpu.SemaphoreType.DMA((2,)), pltpu.SemaphoreType.DMA((2,))],
    )
    return pl.pallas_call(
        functools.partial(_nsa_sample_kernel, n_pages=n_pages, past=past, ts=ts, tp=tp, tn=tn, wb=wb,
                          topk_past=min(SEL_TOPK, npb + 1) - 1),
        grid_spec=gs,
        out_shape=[jax.ShapeDtypeStruct((db, tp, NSA_W), F32), jax.ShapeDtypeStruct((db, wb, LANES), F32),
                   jax.ShapeDtypeStruct((db, wb, LANES), F32)],
        compiler_params=_cparams(("arbitrary",)),
        name="nsa_sample",
    )(page_table_flat, q, qr, gt, kc, vc, ksn, vsn, kwn, vwn, win_k, win_v, oh, gmat, pool_k, pool_v)


def _hgrn_kernel(hm_ref, lb_ref, og_ref, s0_ref, o_ref, sout_ref, st_scr, *, chunk, t_real, levels):
    c = pl.program_id(1)
    nc = pl.num_programs(1)

    @pl.when(c == 0)
    def _():
        for hd in range(HG_HEADS):
            st_scr[hd] = s0_ref[0, hd].T

    row = lax.broadcasted_iota(I32, (chunk, 1), 0)
    valid = (c * chunk + row) < t_real
    row_c = lax.broadcasted_iota(I32, (chunk, chunk), 0)
    col_c = lax.broadcasted_iota(I32, (chunk, chunk), 1)
    outs = []
    for hd in range(HG_HEADS):
        sl = slice(hd * HG_DK, (hd + 1) * HG_DK)
        q = hm_ref[0, :, hd * HG_DK:(hd + 1) * HG_DK]
        z = hm_ref[0, :, HG_W + hd * HG_DK:HG_W + (hd + 1) * HG_DK]
        v = hm_ref[0, :, 2 * HG_W + hd * HG_DV:2 * HG_W + (hd + 1) * HG_DV]
        gate = hm_ref[0, :, 3 * HG_W + hd * HG_DV:3 * HG_W + (hd + 1) * HG_DV]
        lb = lb_ref[:, sl]
        logf = jnp.where(valid, jnp.log(lb + (1.0 - lb) * jax.nn.sigmoid(z)), 0.0)
        k = jnp.where(valid, (1.0 - lb) * jax.nn.sigmoid(-z), 0.0)
        b = logf
        step = 1
        while step < chunk:
            b = b + jnp.where(row >= step, pltpu.roll(b, step, 0), 0.0)
            step *= 2
        st = st_scr[hd]
        o = _nt((q * jnp.exp(b)).astype(BF16), st.astype(BF16))
        if levels:
            a_mat = jnp.zeros((chunk, chunk), F32)
            for h in levels:
                if 2 * h <= 8:
                    b3 = b.reshape(chunk // 8, 8, HG_DK)
                    m = None
                    for blk in range(8 // (2 * h)):
                        mid_row = blk * 2 * h + h - 1
                        cand = jnp.broadcast_to(b3[:, mid_row:mid_row + 1, :], b3.shape).reshape(chunk, HG_DK)
                        m = cand if m is None else jnp.where((row & 7) >= blk * 2 * h, cand, m)
                else:
                    m = jnp.concatenate(
                        [jnp.broadcast_to(b[blk * 2 * h + h - 1:blk * 2 * h + h, :], (2 * h, HG_DK))
                         for blk in range(chunk // (2 * h))], axis=0)
                second = ((row >> _log2(h)) & 1) == 1
                qs = jnp.where(second, q * jnp.exp(jnp.minimum(b - m, 0.0)), 0.0)
                ks = jnp.where(second, 0.0, k * jnp.exp(jnp.minimum(m - b, 0.0)))
                same = (row_c >> _log2(2 * h)) == (col_c >> _log2(2 * h))
                a_mat = jnp.where(same, _nt(qs.astype(BF16), ks.astype(BF16)), a_mat)
            o = o + _mm(a_mat.astype(BF16), v.astype(BF16))
        dblk = min(levels) if levels else chunk
        for d in range(dblk):
            if d == 0:
                o = o + jnp.sum(q * k, axis=1, keepdims=True) * v
                continue
            kd, bd, vd = (pltpu.roll(x, d, 0) for x in (k, b, v))
            p = jnp.where((row & (dblk - 1)) >= d, q * kd * jnp.exp(jnp.minimum(b - bd, 0.0)), 0.0)
            o = o + jnp.sum(p, axis=1, keepdims=True) * vd
        bl = b[chunk - 1:chunk, :]
        st_scr[hd] = jnp.exp(bl) * st + _tn(v.astype(BF16), (k * jnp.exp(bl - b)).astype(BF16))
        outs.append(_rms(o, og_ref[:, sl]) * (gate * jax.nn.sigmoid(gate)))
    o_ref[0] = jnp.concatenate(outs, axis=1)

    @pl.when(c == nc - 1)
    def _():
        for hd in range(HG_HEADS):
            sout_ref[0, hd] = st_scr[hd].T


def _hgrn(hm, lb, og, s0, chunk, t_real):
    b, tpad = hm.shape[:2]
    assert chunk <= 8 or chunk % 16 == 0
    levels = tuple(h for h in (64, 32, 16, 8, 4, 2, 1) if 2 * h <= chunk and chunk >= 16)
    return pl.pallas_call(
        functools.partial(_hgrn_kernel, chunk=chunk, t_real=t_real, levels=levels),
        grid=(b, tpad // chunk),
        in_specs=[pl.BlockSpec((1, chunk, 4 * HG_W), lambda bb, c: (bb, c, 0)),
                  pl.BlockSpec((1, HG_W), lambda bb, c: (0, 0)), pl.BlockSpec((1, HG_W), lambda bb, c: (0, 0)),
                  pl.BlockSpec((1, HG_HEADS, HG_DK, HG_DV), lambda bb, c: (bb, 0, 0, 0))],
        out_specs=[pl.BlockSpec((1, chunk, HG_W), lambda bb, c: (bb, c, 0)),
                   pl.BlockSpec((1, HG_HEADS, HG_DK, HG_DV), lambda bb, c: (bb, 0, 0, 0))],
        out_shape=[jax.ShapeDtypeStruct((b, tpad, HG_W), F32),
                   jax.ShapeDtypeStruct((b, HG_HEADS, HG_DK, HG_DV), F32)],
        scratch_shapes=[pltpu.VMEM((HG_HEADS, HG_DV, HG_DK), F32)],
        compiler_params=_cparams(("parallel", "arbitrary")),
        name="hgrn",
    )(hm, lb, og, s0)


def _mixout_kernel(x_ref, on_ref, oh_ref, gn_ref, woa_ref, wob_ref, gca_ref, wq_ref, x1_ref, q_ref):
    a = _rms(on_ref[...], gn_ref[...]).astype(BF16)
    x1 = x_ref[...] + _mm(a, woa_ref[...]) + _mm(oh_ref[...].astype(BF16), wob_ref[...])
    x1_ref[...] = x1
    hn = _rms(x1, gca_ref[...]).astype(BF16)
    q_ref[...] = (_mm(hn, wq_ref[...]) * (CA_HD ** -0.5)).astype(BF16)


def _mixout(x, o_nsa, o_hg, gn, woa, wob, gca, wq, tm):
    m = x.shape[0]
    row = lambda i: (i, 0)
    const = lambda i: (0, 0)
    return pl.pallas_call(
        _mixout_kernel,
        grid=(m // tm,),
        in_specs=[pl.BlockSpec((tm, D_MODEL), row), pl.BlockSpec((tm, NSA_W), row), pl.BlockSpec((tm, HG_W), row),
                  pl.BlockSpec((1, NSA_W), const), pl.BlockSpec((NSA_W, D_MODEL), const),
                  pl.BlockSpec((HG_W, D_MODEL), const), pl.BlockSpec((1, D_MODEL), const),
                  pl.BlockSpec((D_MODEL, D_MODEL), const)],
        out_specs=[pl.BlockSpec((tm, D_MODEL), row), pl.BlockSpec((tm, D_MODEL), row)],
        out_shape=[jax.ShapeDtypeStruct((m, D_MODEL), F32), jax.ShapeDtypeStruct((m, D_MODEL), BF16)],
        compiler_params=_cparams(("parallel",)),
        name="mixout",
    )(x, o_nsa, o_hg, gn, woa, wob, gca, wq)


def _norm_mm_kernel(x_ref, g_ref, w_ref, o_ref):
    o_ref[...] = _mm(_rms(x_ref[...], g_ref[...]).astype(BF16), w_ref[...])


def _norm_mm(x, g, w, tm):
    m, n = x.shape[0], w.shape[1]
    return pl.pallas_call(
        _norm_mm_kernel,
        grid=(m // tm,),
        in_specs=[pl.BlockSpec((tm, D_MODEL), lambda i: (i, 0)), pl.BlockSpec((1, D_MODEL), lambda i: (0, 0)),
                  pl.BlockSpec((D_MODEL, n), lambda i: (0, 0))],
        out_specs=pl.BlockSpec((tm, n), lambda i: (i, 0)),
        out_shape=jax.ShapeDtypeStruct((m, n), F32),
        compiler_params=_cparams(("parallel",)),
        name="norm_mm",
    )(x, g, w)


def _ca_kernel(q_ref, mk_ref, mv_ref, *rest, ml, tiled):
    o_ref = rest[-1]
    nchunk = CA_HD // LANES

    def head(ref, hh):
        if tiled:
            return jnp.concatenate([ref[0, pl.ds(c * CA_HEADS + hh, ml, stride=nchunk * CA_HEADS), :]
                                    for c in range(nchunk)], axis=1)
        return ref[0, :, hh * CA_HD:(hh + 1) * CA_HD]

    outs = []
    for hh in range(CA_HEADS):
        s = _nt(q_ref[0, :, hh * CA_HD:(hh + 1) * CA_HD], head(mk_ref, hh).astype(BF16))
        e = jnp.exp(s - jnp.max(s, axis=1, keepdims=True))
        p = e / jnp.sum(e, axis=1, keepdims=True)
        outs.append(_mm(p.astype(BF16), head(mv_ref, hh).astype(BF16)))
    o = jnp.concatenate(outs, axis=1).astype(BF16)
    if len(rest) == 3:
        wo_ref, res_ref, _ = rest
        o_ref[0] = res_ref[0] + _mm(o, wo_ref[...])
    else:
        o_ref[0] = o


def _cross_attn(q, mk, mv, tm, ml, tiled, wo=None, res=None):
    b, t = q.shape[:2]
    mem_block = mk.shape[1:]
    rows = lambda bb, i: (bb, i, 0)
    in_specs = [pl.BlockSpec((1, tm, D_MODEL), rows),
                pl.BlockSpec((1,) + mem_block, lambda bb, i: (bb, 0, 0)),
                pl.BlockSpec((1,) + mem_block, lambda bb, i: (bb, 0, 0))]
    args = (q, mk, mv)
    if wo is not None:
        in_specs += [pl.BlockSpec((D_MODEL, D_MODEL), lambda bb, i: (0, 0)), pl.BlockSpec((1, tm, D_MODEL), rows)]
        args += (wo, res)
    return pl.pallas_call(
        functools.partial(_ca_kernel, ml=ml, tiled=tiled),
        grid=(b, t // tm),
        in_specs=in_specs,
        out_specs=pl.BlockSpec((1, tm, D_MODEL), rows),
        out_shape=jax.ShapeDtypeStruct((b, t, D_MODEL), BF16 if wo is None else F32),
        compiler_params=_cparams(("parallel", "parallel")),
        name="cross_attn",
    )(*args)


def _mm_res_kernel(a_ref, w_ref, r_ref, o_ref):
    o_ref[...] = r_ref[...] + _mm(a_ref[...], w_ref[...])


def _mm_res(a, w, res, tm):
    m, k = a.shape
    n = w.shape[1]
    return pl.pallas_call(
        _mm_res_kernel,
        grid=(m // tm,),
        in_specs=[pl.BlockSpec((tm, k), lambda i: (i, 0)), pl.BlockSpec((k, n), lambda i: (0, 0)),
                  pl.BlockSpec((tm, n), lambda i: (i, 0))],
        out_specs=pl.BlockSpec((tm, n), lambda i: (i, 0)),
        out_shape=jax.ShapeDtypeStruct((m, n), F32),
        compiler_params=_cparams(("parallel",)),
        name="mm_res",
    )(a, w, res)


def _ffn_kernel(x_ref, gn_ref, wg_ref, wu_ref, wd_ref, gf_ref, y_ref, h_scr, acc_scr):
    j = pl.program_id(1)

    @pl.when(j == 0)
    def _():
        h_scr[...] = _rms(x_ref[...], gn_ref[...]).astype(BF16)
        acc_scr[...] = jnp.zeros_like(acc_scr)

    h = h_scr[...]
    gate = _mm(h, wg_ref[...])
    act = (gate * jax.nn.sigmoid(gate)) * _mm(h, wu_ref[...])
    acc_scr[...] += _mm(act.astype(BF16), wd_ref[...])

    @pl.when(j == pl.num_programs(1) - 1)
    def _():
        y_ref[...] = _rms(x_ref[...] + acc_scr[...], gf_ref[...])


def _ffn(x, gn, wg, wu, wd, gf, tm, th):
    m = x.shape[0]
    hid = wg.shape[1]
    return pl.pallas_call(
        _ffn_kernel,
        grid=(m // tm, hid // th),
        in_specs=[pl.BlockSpec((tm, D_MODEL), lambda i, j: (i, 0)), pl.BlockSpec((1, D_MODEL), lambda i, j: (0, 0)),
                  pl.BlockSpec((D_MODEL, th), lambda i, j: (0, j)), pl.BlockSpec((D_MODEL, th), lambda i, j: (0, j)),
                  pl.BlockSpec((th, D_MODEL), lambda i, j: (j, 0)), pl.BlockSpec((1, D_MODEL), lambda i, j: (0, 0))],
        out_specs=pl.BlockSpec((tm, D_MODEL), lambda i, j: (i, 0)),
        out_shape=jax.ShapeDtypeStruct((m, D_MODEL), F32),
        scratch_shapes=[pltpu.VMEM((tm, D_MODEL), BF16), pltpu.VMEM((tm, D_MODEL), F32)],
        compiler_params=_cparams(("parallel", "arbitrary")),
        name="ffn",
    )(x, gn, wg, wu, wd, gf)


def _pair(a, axis):
    shp = a.shape
    a = a.reshape(shp[:axis] + (NSA_KV, NSA_GROUP, HEAD_DIM) + shp[axis + 1:])
    a = jnp.swapaxes(a, axis, axis + 1)
    return a.reshape(shp)


def _rope_tables(pos):
    half = ROPE_DIM // 2
    inv = ROPE_THETA ** (-jnp.arange(half, dtype=F32) / half)
    ang = pos.astype(F32)[:, None] * inv[None, :]
    cos, sin = jnp.cos(ang), jnp.sin(ang)
    l = np.arange(LANES) % HEAD_DIM
    idx = l % half
    c = jnp.where(l < ROPE_DIM, cos[:, idx], 1.0)
    sa = jnp.where(l < half, -sin[:, idx], 0.0)
    sb = jnp.where((l >= half) & (l < ROPE_DIM), sin[:, idx], 0.0)
    return c, sa, sb


def _prep_w_in(w):
    offs = np.cumsum([0, NSA_W, 3 * NSA_HEADS] + [NSA_KV * HEAD_DIM] * 6 + [HG_W] * 4)
    seg = lambda i: w[:, offs[i]:offs[i + 1]]
    wq = _pair(seg(0), 1)
    wks = seg(4)
    z = jnp.zeros((D_MODEL, HEAD_DIM), w.dtype)
    aug = jnp.concatenate([wks[:, :HEAD_DIM], z, z, wks[:, HEAD_DIM:]], axis=1)
    gates = jnp.pad(seg(1), ((0, 0), (0, LANES - 3 * NSA_HEADS)))
    out = jnp.concatenate([wq] + [seg(i) for i in range(2, 8)] + [aug] + [seg(i) for i in range(8, 12)] + [gates],
                          axis=1)
    assert out.shape[1] == N_PROJ
    return out.astype(BF16)


def _prep_compress(pe, w1, w2):
    npair = CMP_STRIDE // 2
    pe_big = jnp.broadcast_to(pe.reshape(2, CMP_STRIDE, 1, HEAD_DIM), (2, CMP_STRIDE, NSA_KV, HEAD_DIM))
    pe_big = jnp.pad(pe_big.reshape(2, CMP_STRIDE * LANES), ((0, 6), (0, 0)))
    w1r = w1.reshape(2, npair, 2, HEAD_DIM, CMP_HID)
    eye = jnp.eye(NSA_KV, dtype=w1.dtype)
    w1_big = jnp.einsum('cjldh,ge->jlgdceh', w1r, eye).reshape(npair, 2 * LANES, 4 * CMP_HID)
    w2_big = jnp.einsum('hd,ge->ghed', w2, eye).reshape(NSA_KV * CMP_HID, LANES)
    return pe_big, w1_big.astype(BF16), w2_big.astype(BF16)


def _pad_rows(a, n):
    return jnp.pad(a, ((0, 0), (0, n - a.shape[1]), (0, 0)))


def kernel(x_prompt, x_sample, cache_cmp_k, cache_cmp_v, cache_sel_k, cache_sel_v, cache_win_k, cache_win_v,
           state_hgrn, cache_mem_k, cache_mem_v, page_table, mem_prompt, norm_mix, w_in, cmp_pe_k, cmp_w1_k,
           cmp_w2_k, cmp_pe_v, cmp_w1_v, cmp_w2_v, nsa_out_norm, hg_lb_logits, hg_out_norm, w_out, norm_ca,
           norm_mem, ca_wq, ca_wk, ca_wv, ca_wo, norm_ffn, ffn_w_gate, ffn_w_up, ffn_w_down, final_norm):
    B, T = x_prompt.shape[:2]
    DB, TS = x_sample.shape[:2]
    n_pages = page_table.shape[1]
    past = n_pages * PAGE_SIZE
    n_pool = cache_cmp_k.shape[1]
    wb = cache_win_k.shape[2]
    ml = mem_prompt.shape[1]
    assert w_in.shape[0] == 1, "single layer"
    row2 = lambda a: a.reshape(1, -1)

    w_in_p = _prep_w_in(w_in[0])
    pek, w1k, w2k = _prep_compress(cmp_pe_k[0], cmp_w1_k[0], cmp_w2_k[0])
    pev, w1v, w2v = _prep_compress(cmp_pe_v[0], cmp_w1_v[0], cmp_w2_v[0])
    lb = jnp.cumsum(jax.nn.softmax(hg_lb_logits.astype(F32), axis=0), axis=0)[0].reshape(1, HG_W)
    gn_nsa = row2(_pair(nsa_out_norm[0], 0))
    wo_a = _pair(w_out[0][:NSA_W], 0).astype(BF16)
    wo_b = w_out[0][NSA_W:].astype(BF16)
    wq_ca = ca_wq[0].astype(BF16)
    wo_ca = ca_wo[0].astype(BF16)
    w_mem = jnp.concatenate([ca_wk[0], ca_wv[0]], axis=1).astype(BF16)
    wg, wu, wd = ffn_w_gate[0].astype(BF16), ffn_w_up[0].astype(BF16), ffn_w_down[0].astype(BF16)
    g_mix, g_ca, g_ffn, g_fin = row2(norm_mix[0]), row2(norm_ca[0]), row2(norm_ffn[0]), row2(final_norm)
    g_hg = row2(hg_out_norm[0])

    tabs_p = _rope_tables(jnp.arange(T))
    tabs_s = _rope_tables(past + (jnp.arange(DB * TS) % TS))
    blk = np.arange(T) // SEL_BLK
    lanes2 = np.arange(2 * LANES)
    oh_np = ((lanes2[None, :] >= HEAD_DIM) & (lanes2[None, :] < 3 * HEAD_DIM)
             & ((lanes2[None, :] - HEAD_DIM) % HEAD_DIM == blk[:, None]))
    oh_p = jnp.asarray(oh_np.astype(np.float32))
    oh_s = jnp.zeros((DB * TS, 2 * LANES), F32)

    def tail(x, o_nsa, o_hg, mk, mv, nb, tm, tiled):
        rows = x.shape[0]
        x1, qca = _mixout(x, o_nsa, o_hg, gn_nsa, wo_a, wo_b, g_ca, wq_ca, tm)
        per = rows // nb
        tq = min(per, ROW_TILE)
        if per % 16:
            tq = -(-per // 16) * 16
            qca3 = _pad_rows(qca.reshape(nb, per, D_MODEL), tq)
            oca = _cross_attn(qca3, mk, mv, tq, ml, tiled)[:, :per].reshape(rows, D_MODEL)
            x2 = _mm_res(oca, wo_ca, x1, tm)
        else:
            x2 = _cross_attn(qca.reshape(nb, per, D_MODEL), mk, mv, tq, ml, tiled, wo_ca,
                             x1.reshape(nb, per, D_MODEL)).reshape(rows, D_MODEL)
        return _ffn(x2, g_ffn, wg, wu, wd, g_fin, min(rows, FFN_ROW_TILE), wg.shape[1] // 2)

    M = B * T
    (q, qr, kcp, vcp, _, _, _, _, ksa, kwb, vwb, hm, gt, kct, vct, kst, vst, kwt, vwt) = _proj_in(
        x_prompt.reshape(M, D_MODEL), g_mix, w_in_p, tabs_p, oh_p, PROJ_ROW_TILE, B)
    r3 = lambda a: a.reshape(B, T, -1)
    cmp_w = (pek, pev, w1k, w1v, w2k, w2v)
    kc_p, vct_p = _compress_rows(r3(kcp), r3(vcp), cmp_w)
    o_nsa = _nsa_prompt(r3(q), r3(qr), r3(gt), kc_p, vct_p, r3(ksa), vst, r3(kwb), vwt, NSA_Q_TILE, NSA_KEY_TILE)
    o_hg, s_p = _hgrn(r3(hm), lb, g_hg, jnp.zeros((B, HG_HEADS, HG_DK, HG_DV), F32), HG_CHUNK, T)
    mkv = _norm_mm(mem_prompt.reshape(B * ml, D_MODEL), row2(norm_mem[0]), w_mem, ROW_TILE)
    mk_p = mkv[:, :D_MODEL].reshape(B, ml, D_MODEL)
    mv_p = mkv[:, D_MODEL:].reshape(B, ml, D_MODEL)
    y_p = tail(x_prompt.reshape(M, D_MODEL), o_nsa.reshape(M, NSA_W), o_hg.reshape(M, HG_W), mk_p, mv_p, B, ROW_TILE, False)

    kv5 = lambda a, n: a.reshape(1, -1, n, NSA_KV, HEAD_DIM)
    wbp = min(WINDOW, T)
    nat = lambda a: jnp.transpose(a.reshape(B, NSA_KV, HEAD_DIM, -1), (0, 3, 1, 2))[None]
    outs_p = (nat(kct), nat(vct), nat(kst), nat(vst),
              nat(kwt[:, :, T - wbp:]), nat(vwt[:, :, T - wbp:]), s_p[None],
              mk_p.reshape(1, B, ml, CA_HEADS, CA_HD), mv_p.reshape(1, B, ml, CA_HEADS, CA_HD))

    MS = DB * TS
    (q, qr, kcs, vcs, kss, vss, kws, vws, _, _, _, hm, gt) = _proj_in(
        x_sample.reshape(MS, D_MODEL), g_mix, w_in_p, tabs_s, oh_s, MS, 1)[:13]
    pt_flat = page_table.reshape(-1).astype(I32)
    pool_t = lambda a: jnp.transpose(a[0], (0, 2, 3, 1)).reshape(n_pool, LANES, PAGE_SIZE)
    kc_s, vc_s = _compress_paged(pt_flat, pool_t(cache_cmp_k), pool_t(cache_cmp_v), cmp_w, DB, n_pages)
    tp, tn = SAMPLE_Q_PAD, SAMPLE_KEY_PAD
    s3 = lambda a, n: _pad_rows(a.reshape(DB, TS, -1), n)
    key_blk = np.arange(past) // SEL_BLK
    oh_keys = jnp.asarray((np.arange(LANES)[:, None] == key_blk[None, :]).astype(np.float32)).astype(BF16)
    nseg = past // CMP_STRIDE
    gmat = jnp.asarray((np.arange(nseg)[:, None] // (SEL_BLK // CMP_STRIDE)
                        == np.arange(LANES)[None, :]).astype(np.float32))
    o_nsa_s, nwk, nwv = _nsa_sample(
        pt_flat, s3(q, tp), s3(qr, tp), s3(gt, tp), kc_s, vc_s, s3(kss, tn), s3(vss, tn), s3(kws, tn), s3(vws, tn),
        cache_win_k[0].reshape(DB, wb, LANES), cache_win_v[0].reshape(DB, wb, LANES), oh_keys, gmat,
        pool_t(cache_sel_k), pool_t(cache_sel_v), n_pages, past, TS)
    o_hg_s, s_s = _hgrn(s3(hm, SAMPLE_Q_PAD), lb, g_hg, state_hgrn[0].astype(F32), SAMPLE_Q_PAD, TS)
    nchunk = CA_HD // LANES
    mem_t = lambda a: jnp.transpose(a[0].reshape(DB, ml, CA_HEADS, nchunk, LANES), (0, 1, 3, 2, 4)).reshape(
        DB, ml * nchunk * CA_HEADS, LANES)
    y_s = tail(x_sample.reshape(MS, D_MODEL), o_nsa_s[:, :TS].reshape(MS, NSA_W), o_hg_s[:, :TS].reshape(MS, HG_W),
               mem_t(cache_mem_k), mem_t(cache_mem_v), DB, MS, True)

    outs_s = (kv5(kcs, TS), kv5(vcs, TS), kv5(kss, TS), kv5(vss, TS),
              nwk.reshape(1, DB, wb, NSA_KV, HEAD_DIM), nwv.reshape(1, DB, wb, NSA_KV, HEAD_DIM), s_s[None])
    return (y_p.reshape(B, T, D_MODEL), y_s.reshape(DB, TS, D_MODEL)) + outs_p + outs_s
```

```python
import functools

import numpy as np
import jax
import jax.numpy as jnp
from jax import lax
from jax.experimental import pallas as pl
from jax.experimental.pallas import tpu as pltpu

F32 = jnp.float32
BF16 = jnp.bfloat16
I32 = jnp.int32

D_MODEL = 1024
NSA_HEADS = 8
NSA_KV = 2
HEAD_DIM = 64
NSA_GROUP = NSA_HEADS // NSA_KV
NSA_W = NSA_HEADS * HEAD_DIM
CMP_STRIDE = 16
CMP_LEN = 32
CMP_HID = 128
SEL_BLK = 64
SEL_TOPK = 16
WINDOW = 512
ROPE_THETA = 500000.0
ROPE_DIM = HEAD_DIM // 4
HG_HEADS = 4
HG_DK = 128
HG_DV = 128
HG_W = HG_HEADS * HG_DV
CA_HEADS = 4
CA_HD = D_MODEL // CA_HEADS
PAGE_SIZE = 128
EPS = 1e-6
FORCE_SCORE = 1e4
NEG = -1e30
MASK_BIAS = -1e9
LANES = 128
VMEM_LIMIT = 56 * 1024 * 1024
ROW_TILE = 512
FFN_ROW_TILE = 512
NSA_Q_TILE = 128
NSA_KEY_TILE = 512
RANK_UNROLL = 8
PAGE_UNROLL = 16
HG_CHUNK = 128
SAMPLE_Q_PAD = 8
SAMPLE_KEY_PAD = 16

C_Q = 0
C_KC, C_VC, C_KS, C_VS, C_KW, C_VW = 512, 640, 768, 896, 1024, 1152
C_AUG = 1280
C_HG = 1536
C_GT = 3584
N_PROJ = 3712


def _cparams(sem):
    return pltpu.CompilerParams(dimension_semantics=sem, vmem_limit_bytes=VMEM_LIMIT)


def _nt(a, b):
    return lax.dot_general(a, b, (((1,), (1,)), ((), ())), preferred_element_type=F32)


def _tn(a, b):
    return lax.dot_general(a, b, (((0,), (0,)), ((), ())), preferred_element_type=F32)


def _mm(a, b):
    return jnp.dot(a, b, preferred_element_type=F32)


def _rms(x, g):
    return x * lax.rsqrt(jnp.mean(x * x, axis=-1, keepdims=True) + EPS) * g


def _masked_softmax(s, mask, axis):
    s = jnp.where(mask, s, NEG)
    m = jnp.max(s, axis=axis, keepdims=True)
    e = jnp.where(mask, jnp.exp(s - m), 0.0)
    return e / jnp.maximum(jnp.sum(e, axis=axis, keepdims=True), 1e-20)


def _log2(n):
    l = int(n).bit_length() - 1
    assert (1 << l) == n, n
    return l


def _proj_in_kernel(x_ref, g_ref, w_ref, c_ref, sa_ref, sb_ref, oh_ref,
                    q_ref, qr_ref, kc_ref, vc_ref, ks_ref, vs_ref, kw_ref, vw_ref,
                    ksa_ref, kwb_ref, vwb_ref, hm_ref, gt_ref,
                    kct_ref, vct_ref, kst_ref, vst_ref, kwt_ref, vwt_ref):
    xn = _rms(x_ref[...], g_ref[...]).astype(BF16)
    c, sa, sb = c_ref[...], sa_ref[...], sb_ref[...]

    def mm(lo, hi):
        return _mm(xn, w_ref[:, lo:hi])

    def rope(y):
        n = y.shape[1] // LANES
        cc, aa, bb = (jnp.concatenate([t] * n, axis=1) if n > 1 else t for t in (c, sa, sb))
        w = y.shape[1]
        return y * cc + pltpu.roll(y, w - ROPE_DIM // 2, 1) * aa + pltpu.roll(y, ROPE_DIM // 2, 1) * bb

    yq = mm(C_Q, C_Q + NSA_W) * (HEAD_DIM ** -0.5)
    q_ref[...] = yq.astype(BF16)
    qr_ref[...] = rope(yq).astype(BF16)
    kw = vw = None
    for col, roped, ref, tref in ((C_KC, False, kc_ref, kct_ref), (C_VC, False, vc_ref, vct_ref),
                                  (C_KS, True, ks_ref, kst_ref), (C_VS, False, vs_ref, vst_ref),
                                  (C_KW, True, kw_ref, kwt_ref), (C_VW, False, vw_ref, vwt_ref)):
        y = mm(col, col + LANES)
        y = rope(y) if roped else y
        ref[...] = y
        tref[0] = y.T
        kw = y if col == C_KW else kw
        vw = y if col == C_VW else vw
    kwb_ref[...] = kw.astype(BF16)
    vwb_ref[...] = vw.astype(BF16)
    ksa_ref[...] = (rope(mm(C_AUG, C_AUG + 2 * LANES)) + oh_ref[...]).astype(BF16)
    hm_ref[...] = mm(C_HG, C_HG + 4 * HG_W)
    gt_ref[...] = mm(C_GT, C_GT + LANES)


def _proj_in(x2d, g, w, tabs, oh, tm, nb):
    m = x2d.shape[0]
    per = m // nb
    nt_tab = tabs[0].shape[0] // tm
    nt_seq = per // tm
    row = lambda i: (i, 0)
    tab = lambda i: (i % nt_tab, 0)
    const = lambda i: (0, 0)
    widths = [(NSA_W, BF16), (NSA_W, BF16)] + [(LANES, F32)] * 6 + [(2 * LANES, BF16)] + [(LANES, BF16)] * 2 \
        + [(4 * HG_W, F32), (LANES, F32)]
    return pl.pallas_call(
        _proj_in_kernel,
        grid=(m // tm,),
        in_specs=[pl.BlockSpec((tm, D_MODEL), row), pl.BlockSpec((1, D_MODEL), const),
                  pl.BlockSpec((D_MODEL, N_PROJ), const),
                  pl.BlockSpec((tm, LANES), tab), pl.BlockSpec((tm, LANES), tab), pl.BlockSpec((tm, LANES), tab),
                  pl.BlockSpec((tm, 2 * LANES), tab)],
        out_specs=[pl.BlockSpec((tm, wd), row) for wd, _ in widths]
        + [pl.BlockSpec((1, LANES, tm), lambda i: (i // nt_seq, 0, i % nt_seq))] * 6,
        out_shape=[jax.ShapeDtypeStruct((m, wd), dt) for wd, dt in widths]
        + [jax.ShapeDtypeStruct((nb, LANES, per), F32)] * 6,
        compiler_params=_cparams(("parallel",)),
        name="proj_in",
    )(x2d, g, w, *tabs, oh)


def _page_copy(pool_hbm, page, buf, slot, p, sem):
    cols = pl.ds(pl.multiple_of(p * PAGE_SIZE, PAGE_SIZE), PAGE_SIZE)
    return pltpu.make_async_copy(pool_hbm.at[page], buf.at[slot, :, cols], sem)


def _gather_start(pt_ref, pools, bufs, sems, b, slot, n_pages):
    def body(p, carry):
        page = pt_ref[b * n_pages + p]
        for pool, buf, sem in zip(pools, bufs, sems):
            _page_copy(pool, page, buf, slot, p, sem.at[slot]).start()
        return carry
    lax.fori_loop(0, n_pages, body, 0, unroll=PAGE_UNROLL)


def _gather_wait(pools, bufs, sems, slot, n_pages):
    def body(p, carry):
        for pool, buf, sem in zip(pools, bufs, sems):
            _page_copy(pool, 0, buf, slot, p, sem.at[slot]).wait()
        return carry
    lax.fori_loop(0, n_pages, body, 0, unroll=PAGE_UNROLL)


def _gather_step(pt_ref, pools, bufs, sems, n_pages):
    b = pl.program_id(0)
    nb = pl.num_programs(0)
    slot = b % 2

    @pl.when(b == 0)
    def _():
        _gather_start(pt_ref, pools, bufs, sems, 0, 0, n_pages)

    @pl.when(b + 1 < nb)
    def _():
        _gather_start(pt_ref, pools, bufs, sems, b + 1, 1 - slot, n_pages)

    _gather_wait(pools, bufs, sems, slot, n_pages)
    return slot


def _compress_core(tok, pe_ref, w1_ref, w2_ref, bias_scr, first, out, nseg, transposed_out=False):
    npair = CMP_STRIDE // 2
    nh = NSA_KV * CMP_HID

    @pl.when(first)
    def _():
        acc = _mm(pe_ref[:, 0:2 * LANES].astype(BF16), w1_ref[0])
        for jj in range(1, npair):
            acc = acc + _mm(pe_ref[:, jj * 2 * LANES:(jj + 1) * 2 * LANES].astype(BF16), w1_ref[jj])
        bias_scr[...] = acc

    acc = None
    for jj in range(npair):
        xp = jnp.concatenate([tok[pl.ds(2 * jj, nseg, stride=CMP_STRIDE), :],
                              tok[pl.ds(2 * jj + 1, nseg, stride=CMP_STRIDE), :]], axis=1).astype(BF16)
        part = _mm(xp, w1_ref[jj])
        acc = part if acc is None else acc + part
    ha = acc[:, 0:nh] + bias_scr[0:1, 0:nh]
    hb = acc[:, nh:2 * nh] + bias_scr[1:2, nh:2 * nh]
    h = jax.nn.gelu(ha + pltpu.roll(hb, nseg - 1, 0))
    row = lax.broadcasted_iota(I32, (nseg, 1), 0)
    h = jnp.where(row < nseg - 1, h, 0.0)
    res = _mm(h.astype(BF16), w2_ref[...]).astype(BF16)
    out[...] = res.T if transposed_out else res


def _compress_rows_kernel(k_ref, v_ref, pek_ref, pev_ref, w1k_ref, w1v_ref, w2k_ref, w2v_ref,
                          kc_ref, vct_ref, bias_scr, *, nseg):
    first = pl.program_id(0) == 0
    for i, (tok, pe, w1, w2, out) in enumerate(((k_ref, pek_ref, w1k_ref, w2k_ref, kc_ref),
                                                (v_ref, pev_ref, w1v_ref, w2v_ref, vct_ref))):
        _compress_core(tok.at[0], pe, w1, w2, bias_scr.at[i], first, out.at[0], nseg, transposed_out=i == 1)


def _compress_paged_kernel(pt_ref, poolk_hbm, poolv_hbm, pek_ref, pev_ref, w1k_ref, w1v_ref, w2k_ref, w2v_ref,
                           kc_ref, vc_ref, xk_buf, xv_buf, tokk_scr, tokv_scr, bias_scr, semk, semv, *, n_pages):
    slot = _gather_step(pt_ref, (poolk_hbm, poolv_hbm), (xk_buf, xv_buf), (semk, semv), n_pages)
    first = pl.program_id(0) == 0
    nseg = n_pages * (PAGE_SIZE // CMP_STRIDE)
    for i, (xbuf, tok, pe, w1, w2, out) in enumerate(((xk_buf, tokk_scr, pek_ref, w1k_ref, w2k_ref, kc_ref),
                                                      (xv_buf, tokv_scr, pev_ref, w1v_ref, w2v_ref, vc_ref))):
        tok[...] = xbuf[slot].T
        _compress_core(tok, pe, w1, w2, bias_scr.at[i], first, out.at[0], nseg)


def _compress_weight_specs(const2, const3):
    npair = CMP_STRIDE // 2
    return [pl.BlockSpec((8, CMP_STRIDE * LANES), const2), pl.BlockSpec((8, CMP_STRIDE * LANES), const2),
            pl.BlockSpec((npair, 2 * LANES, 4 * CMP_HID), const3), pl.BlockSpec((npair, 2 * LANES, 4 * CMP_HID), const3),
            pl.BlockSpec((NSA_KV * CMP_HID, LANES), const2), pl.BlockSpec((NSA_KV * CMP_HID, LANES), const2)]


def _compress_rows(k_rows, v_rows, weights):
    nb, ntok = k_rows.shape[:2]
    nseg = ntok // CMP_STRIDE
    per_b = lambda b: (b, 0, 0)
    return pl.pallas_call(
        functools.partial(_compress_rows_kernel, nseg=nseg),
        grid=(nb,),
        in_specs=[pl.BlockSpec((1, ntok, LANES), per_b), pl.BlockSpec((1, ntok, LANES), per_b)]
        + _compress_weight_specs(lambda b: (0, 0), lambda b: (0, 0, 0)),
        out_specs=[pl.BlockSpec((1, nseg, LANES), per_b), pl.BlockSpec((1, LANES, nseg), per_b)],
        out_shape=[jax.ShapeDtypeStruct((nb, nseg, LANES), BF16), jax.ShapeDtypeStruct((nb, LANES, nseg), BF16)],
        scratch_shapes=[pltpu.VMEM((2, 8, 4 * CMP_HID), F32)],
        compiler_params=_cparams(("arbitrary",)),
        name="compress_rows",
    )(k_rows, v_rows, *weights)


def _compress_paged(page_table_flat, pool_k, pool_v, weights, n_batch, n_pages):
    past = n_pages * PAGE_SIZE
    nseg = past // CMP_STRIDE
    gs = pltpu.PrefetchScalarGridSpec(
        num_scalar_prefetch=1,
        grid=(n_batch,),
        in_specs=[pl.BlockSpec(memory_space=pl.ANY), pl.BlockSpec(memory_space=pl.ANY)]
        + _compress_weight_specs(lambda b, pt: (0, 0), lambda b, pt: (0, 0, 0)),
        out_specs=[pl.BlockSpec((1, nseg, LANES), lambda b, pt: (b, 0, 0))] * 2,
        scratch_shapes=[pltpu.VMEM((2, LANES, past), F32), pltpu.VMEM((2, LANES, past), F32),
                        pltpu.VMEM((past, LANES), F32), pltpu.VMEM((past, LANES), F32),
                        pltpu.VMEM((2, 8, 4 * CMP_HID), F32),
                        pltpu.SemaphoreType.DMA((2,)), pltpu.SemaphoreType.DMA((2,))],
    )
    return pl.pallas_call(
        functools.partial(_compress_paged_kernel, n_pages=n_pages),
        grid_spec=gs,
        out_shape=[jax.ShapeDtypeStruct((n_batch, nseg, LANES), BF16)] * 2,
        compiler_params=_cparams(("arbitrary",)),
        name="compress_paged",
    )(page_table_flat, pool_k, pool_v, *weights)


def _nsa_prompt_kernel(q_ref, qr_ref, gt_ref, kc_ref, vct_ref, ksa_ref, vst_ref, kw_ref, vwt_ref, o_ref,
                       psum_scr, sc_scr, *, tq, tk, ncp, nsel, topk):
    i = pl.program_id(1)
    s0 = i * tq
    rows = NSA_GROUP * tq
    hk = tk // 2
    wl = WINDOW + tq
    upper = lax.broadcasted_iota(I32, (tq, LANES), 1) >= HEAD_DIM
    kc = kc_ref[0]
    t_cols = s0 + (lax.broadcasted_iota(I32, (1, rows), 1) & (tq - 1))
    cmask = lax.broadcasted_iota(I32, (ncp, rows), 0) * CMP_STRIDE + (CMP_LEN - 1) <= t_cols
    ws = pl.multiple_of(jnp.maximum(s0 - WINDOW, 0), tq)
    kpos_w = ws + lax.broadcasted_iota(I32, (wl, rows), 0)
    wmask = (kpos_w <= t_cols) & (kpos_w > t_cols - WINDOW)
    kw_t = kw_ref[0, pl.ds(ws, wl), :]
    o_ct, o_wt, o_st = [], [], []
    for g in range(NSA_KV):
        mine = upper if g == 1 else jnp.logical_not(upper)

        def stack(ref, fill):
            return jnp.concatenate(
                [jnp.where(mine, ref[0, :, r * LANES:(r + 1) * LANES], fill) for r in range(NSA_GROUP)], axis=0)

        def ones_rows(v):
            up = lax.broadcasted_iota(I32, v.shape, 0) >= HEAD_DIM
            return jnp.where(up if g == 1 else jnp.logical_not(up), v, 1.0).astype(BF16)

        def dim_rows(x):
            return x[g * HEAD_DIM:(g + 1) * HEAD_DIM]

        def den_rows(x):
            return x[(1 - g) * HEAD_DIM:(2 - g) * HEAD_DIM]

        p_c = _masked_softmax(_nt(kc, stack(q_ref, 0)), cmask, 0)
        o_ct.append(dim_rows(_mm(vct_ref[0], p_c.astype(BF16))))
        psum = p_c[:, 0:tq]
        for r in range(1, NSA_GROUP):
            psum = psum + p_c[:, r * tq:(r + 1) * tq]
        psum_scr[...] = psum
        cps = SEL_BLK // CMP_STRIDE
        imp = psum_scr[pl.ds(0, nsel, stride=cps), :]
        for u in range(1, cps):
            imp = imp + psum_scr[pl.ds(u, nsel, stride=cps), :]
        j = lax.broadcasted_iota(I32, (nsel, tq), 0)
        cur = (s0 + lax.broadcasted_iota(I32, (nsel, tq), 1)) >> _log2(SEL_BLK)
        avail = j <= cur
        forced = (j == 0) | (j == cur) | (j == cur - 1)
        score = jnp.where(avail, jnp.where(forced, FORCE_SCORE, imp), -FORCE_SCORE)
        sc_scr[...] = score

        def rank_body(grp, cnt):
            for u in range(RANK_UNROLL):
                jp = grp * RANK_UNROLL + u
                other = sc_scr[pl.ds(jp, 1), :]
                beats = (other > score) | ((other == score) & (jp < j))
                cnt = cnt + beats.astype(I32)
            return cnt

        n_live = ((s0 + tq - 1) >> _log2(SEL_BLK)) + 1
        n_grp = jnp.minimum((n_live + RANK_UNROLL - 1) // RANK_UNROLL, nsel // RANK_UNROLL)
        cnt = lax.fori_loop(0, n_grp, rank_body, jnp.zeros((nsel, tq), I32))
        bias_t = jnp.where((cnt < topk) & avail, 0.0, MASK_BIAS)
        if nsel < HEAD_DIM:
            bias_t = jnp.concatenate([bias_t, jnp.full((HEAD_DIM - nsel, tq), MASK_BIAS, F32)], axis=0)
        bias = jnp.concatenate([bias_t, bias_t], axis=0).T.astype(BF16)
        qs = stack(qr_ref, bias)
        glanes = slice(g * LANES, (g + 1) * LANES)

        def scores(k0):
            return _nt(ksa_ref[0, pl.ds(k0, tk), glanes], qs)

        def col_max(s):
            return tuple(jnp.max(s[c * hk:(c + 1) * hk], axis=0, keepdims=True) for c in range(2))

        def update(s, mx, k0, stats):
            new = []
            for c, (m, acc) in enumerate(stats):
                v_one = ones_rows(vst_ref[0, :, pl.ds(k0 + c * hk, hk)])
                m_new = jnp.maximum(m, mx[c])
                p = jnp.exp((s[c * hk:(c + 1) * hk] - m_new).astype(BF16))
                new.append((m_new, jnp.exp(m - m_new) * acc + _mm(v_one, p)))
            return tuple(new)

        def sel_body(kk, carry):
            s_cur, mx_cur, stats = carry
            k0 = pl.multiple_of(kk * tk, tk)
            s_next = scores(k0 + tk)
            stats = update(s_cur, mx_cur, k0, stats)
            return s_next, col_max(s_next), stats

        n_full = s0 // tk
        stats0 = tuple((jnp.full((1, rows), NEG, F32), jnp.zeros((LANES, rows), F32)) for _ in range(2))
        s_first = scores(0)
        s_last, _, stats = lax.fori_loop(0, n_full, sel_body, (s_first, col_max(s_first), stats0))
        k_last = pl.multiple_of(n_full * tk, tk)
        causal = k_last + lax.broadcasted_iota(I32, (tk, rows), 0) <= t_cols
        s_last = jnp.where(causal, s_last, NEG)
        (m_a, acc_a), (m_b, acc_b) = update(s_last, col_max(s_last), k_last, stats)
        m_s = jnp.maximum(m_a, m_b)
        acc_s = jnp.exp(m_a - m_s) * acc_a + jnp.exp(m_b - m_s) * acc_b
        o_st.append(dim_rows(acc_s) / jnp.maximum(den_rows(acc_s), 1e-20))
        s_w = jnp.where(wmask, _nt(kw_t, stack(qr_ref, 0)), NEG)
        e_w = jnp.exp((s_w - jnp.max(s_w, axis=0, keepdims=True)).astype(BF16))
        acc_w = _mm(ones_rows(vwt_ref[0, :, pl.ds(ws, wl)]), e_w)
        o_wt.append(dim_rows(acc_w) / jnp.maximum(den_rows(acc_w), 1e-20))
    gsig_t = jax.nn.sigmoid(gt_ref[0]).T
    out_slots = []
    for r in range(NSA_GROUP):
        rs = slice(r * tq, (r + 1) * tq)
        halves = []
        for g in range(NSA_KV):
            c0 = 3 * (g * NSA_GROUP + r)
            halves.append(gsig_t[c0:c0 + 1] * o_ct[g][:, rs] + gsig_t[c0 + 1:c0 + 2] * o_st[g][:, rs]
                          + gsig_t[c0 + 2:c0 + 3] * o_wt[g][:, rs])
        out_slots.append(jnp.concatenate(halves, axis=0).T)
    o_ref[0] = jnp.concatenate(out_slots, axis=1)


def _nsa_prompt(q, qr, gt, kc, vct, ksa, vst, kwb, vwt, tq, tk):
    b, t = q.shape[:2]
    ncp = t // CMP_STRIDE
    nsel = t // SEL_BLK
    assert t % tk == 0 and tk % tq == 0 and t >= WINDOW + tq and nsel <= HEAD_DIM and WINDOW % tq == 0
    assert nsel % RANK_UNROLL == 0
    blk = lambda bb, i: (bb, i, 0)
    whole = lambda bb, i: (bb, 0, 0)
    return pl.pallas_call(
        functools.partial(_nsa_prompt_kernel, tq=tq, tk=tk, ncp=ncp, nsel=nsel, topk=min(SEL_TOPK, nsel)),
        grid=(b, t // tq),
        in_specs=[pl.BlockSpec((1, tq, NSA_W), blk), pl.BlockSpec((1, tq, NSA_W), blk),
                  pl.BlockSpec((1, tq, LANES), blk),
                  pl.BlockSpec((1, ncp, LANES), whole), pl.BlockSpec((1, LANES, ncp), whole),
                  pl.BlockSpec((1, t, 2 * LANES), whole), pl.BlockSpec((1, LANES, t), whole),
                  pl.BlockSpec((1, t, LANES), whole), pl.BlockSpec((1, LANES, t), whole)],
        out_specs=pl.BlockSpec((1, tq, NSA_W), blk),
        out_shape=jax.ShapeDtypeStruct((b, t, NSA_W), F32),
        scratch_shapes=[pltpu.VMEM((ncp, tq), F32), pltpu.VMEM((nsel, tq), F32)],
        compiler_params=_cparams(("parallel", "parallel")),
        name="nsa_prompt",
    )(q, qr, gt, kc, vct, ksa, vst, kwb, vwt)


def _nsa_sample_kernel(pt_ref, q_ref, qr_ref, gt_ref, kc_ref, vc_ref, ksn_ref, vsn_ref, kwn_ref, vwn_ref,
                       wink_ref, winv_ref, oht_ref, gmat_ref, poolk_hbm, poolv_hbm,
                       o_ref, nwk_ref, nwv_ref, kbuf, vbuf, semk, semv,
                       *, n_pages, past, ts, tp, tn, wb, topk_past):
    slot = _gather_step(pt_ref, (poolk_hbm, poolv_hbm), (kbuf, vbuf), (semk, semv), n_pages)
    nseg = n_pages * (PAGE_SIZE // CMP_STRIDE)
    npb = past // SEL_BLK
    rows = NSA_HEADS * tp
    lane = lax.broadcasted_iota(I32, (tp, LANES), 1)
    upper = lane >= HEAD_DIM

    def stack(ref):
        parts = []
        for g in range(NSA_KV):
            mine = upper if g == 1 else jnp.logical_not(upper)
            parts += [jnp.where(mine, ref[0, :, r * LANES:(r + 1) * LANES], 0) for r in range(NSA_GROUP)]
        return jnp.concatenate(parts, axis=0)

    t_rows = lax.broadcasted_iota(I32, (rows, 1), 0) & (tp - 1)
    qc = stack(q_ref)
    cend = lax.broadcasted_iota(I32, (rows, nseg), 1) * CMP_STRIDE + (CMP_LEN - 1)
    p_c = _masked_softmax(_nt(qc, kc_ref[0]), cend <= past + t_rows, 1)
    o_c = _mm(p_c.astype(BF16), vc_ref[0])
    impn = []
    for g in range(NSA_KV):
        acc = p_c[(g * NSA_GROUP) * tp:(g * NSA_GROUP + 1) * tp]
        for r in range(1, NSA_GROUP):
            acc = acc + p_c[(g * NSA_GROUP + r) * tp:(g * NSA_GROUP + r + 1) * tp]
        impn.append(acc)
    impn = jnp.concatenate(impn, axis=0)
    imp = jnp.dot(impn, gmat_ref[...], precision=lax.Precision.HIGHEST, preferred_element_type=F32)
    j = lax.broadcasted_iota(I32, (NSA_KV * tp, LANES), 1)
    avail = j < npb
    forced = (j == 0) | (j == npb - 1)
    score = jnp.where(avail, jnp.where(forced, FORCE_SCORE, imp), -FORCE_SCORE)
    cnt = jnp.zeros((NSA_KV * tp, LANES), I32)
    for jp in range(npb):
        other = score[:, jp:jp + 1]
        beats = (other > score) | ((other == score) & (jp < j))
        cnt = cnt + beats.astype(I32)
    bias = jnp.where((cnt < topk_past) & avail, 0.0, MASK_BIAS).astype(BF16)
    bias = jnp.concatenate([bias[0:tp]] * NSA_GROUP + [bias[tp:2 * tp]] * NSA_GROUP, axis=0)
    qs = stack(qr_ref)
    key_t = lax.broadcasted_iota(I32, (rows, tn), 1)
    new_mask = (key_t <= t_rows) & (key_t < ts)
    s_p = _mm(qs, kbuf[slot].astype(BF16)) + _mm(bias, oht_ref[...])
    s_n = jnp.where(new_mask, _nt(qs, ksn_ref[0].astype(BF16)), NEG)
    m_s = jnp.maximum(jnp.max(s_p, axis=1, keepdims=True), jnp.max(s_n, axis=1, keepdims=True))
    e_p = jnp.exp(s_p - m_s)
    e_n = jnp.where(new_mask, jnp.exp(s_n - m_s), 0.0)
    l_s = jnp.sum(e_p, axis=1, keepdims=True) + jnp.sum(e_n, axis=1, keepdims=True)
    o_s = (_nt(e_p.astype(BF16), vbuf[slot].astype(BF16)) + _mm(e_n.astype(BF16), vsn_ref[0].astype(BF16))) \
        / jnp.maximum(l_s, 1e-20)
    wk = wink_ref[0]
    wv = winv_ref[0]
    iw = lax.broadcasted_iota(I32, (rows, wb), 1)
    wmask = (iw > t_rows + (wb - WINDOW)) & (iw >= wb - past)
    s_w = jnp.where(wmask, _nt(qs, wk.astype(BF16)), NEG)
    s_wn = jnp.where(new_mask, _nt(qs, kwn_ref[0].astype(BF16)), NEG)
    m_w = jnp.maximum(jnp.max(s_w, axis=1, keepdims=True), jnp.max(s_wn, axis=1, keepdims=True))
    e_w = jnp.where(wmask, jnp.exp(s_w - m_w), 0.0)
    e_wn = jnp.where(new_mask, jnp.exp(s_wn - m_w), 0.0)
    inv = 1.0 / jnp.maximum(jnp.sum(e_w, axis=1, keepdims=True) + jnp.sum(e_wn, axis=1, keepdims=True), 1e-20)
    o_w = _mm((e_w * inv).astype(BF16), wv.astype(BF16)) + _mm((e_wn * inv).astype(BF16), vwn_ref[0].astype(BF16))
    gsig = jax.nn.sigmoid(gt_ref[0])
    slots = []
    for r in range(NSA_GROUP):
        per_g = []
        for g in range(NSA_KV):
            h = g * NSA_GROUP + r
            rs = slice(h * tp, (h + 1) * tp)
            per_g.append(gsig[:, 3 * h:3 * h + 1] * o_c[rs] + gsig[:, 3 * h + 1:3 * h + 2] * o_s[rs]
                         + gsig[:, 3 * h + 2:3 * h + 3] * o_w[rs])
        slots.append(jnp.where(upper, per_g[1], per_g[0]))
    o_ref[0] = jnp.concatenate(slots, axis=1)
    sub = lax.broadcasted_iota(I32, (8, LANES), 0)
    for win, new_ref, out in ((wk, kwn_ref, nwk_ref), (wv, vwn_ref, nwv_ref)):
        rolled = pltpu.roll(win, wb - ts, 0)
        shifted = pltpu.roll(new_ref[0, 0:8, :], 8 - ts, 0)
        out[0, 0:wb - 8, :] = rolled[0:wb - 8]
        out[0, wb - 8:wb, :] = jnp.where(sub >= 8 - ts, shifted, rolled[wb - 8:wb])


def _nsa_sample(page_table_flat, q, qr, gt, kc, vc, ksn, vsn, kwn, vwn, win_k, win_v, oh, gmat, pool_k, pool_v,
                n_pages, past, ts):
    db, tp = q.shape[:2]
    tn = ksn.shape[1]
    wb = win_k.shape[1]
    nseg = n_pages * (PAGE_SIZE // CMP_STRIDE)
    npb = past // SEL_BLK
    assert ts <= 8 and npb <= LANES and wb >= 8 and past % PAGE_SIZE == 0
    per_b = lambda b, pt: (b, 0, 0)
    const2 = lambda b, pt: (0, 0)
    gs = pltpu.PrefetchScalarGridSpec(
        num_scalar_prefetch=1,
        grid=(db,),
        in_specs=[pl.BlockSpec((1, tp, NSA_W), per_b), pl.BlockSpec((1, tp, NSA_W), per_b),
                  pl.BlockSpec((1, tp, LANES), per_b),
                  pl.BlockSpec((1, nseg, LANES), per_b), pl.BlockSpec((1, nseg, LANES), per_b),
                  pl.BlockSpec((1, tn, LANES), per_b), pl.BlockSpec((1, tn, LANES), per_b),
                  pl.BlockSpec((1, tn, LANES), per_b), pl.BlockSpec((1, tn, LANES), per_b),
                  pl.BlockSpec((1, wb, LANES), per_b), pl.BlockSpec((1, wb, LANES), per_b),
                  pl.BlockSpec((LANES, past), const2), pl.BlockSpec((nseg, LANES), const2),
                  pl.BlockSpec(memory_space=pl.ANY), pl.BlockSpec(memory_space=pl.ANY)],
        out_specs=[pl.BlockSpec((1, tp, NSA_W), per_b), pl.BlockSpec((1, wb, LANES), per_b),
                   pl.BlockSpec((1, wb, LANES), per_b)],
        scratch_shapes=[pltpu.VMEM((2, LANES, past), F32), pltpu.VMEM((2, LANES, past), F32),
                        pltpu.SemaphoreType.DMA((2,)), pltpu.SemaphoreType.DMA((2,))],
    )
    return pl.pallas_call(
        functools.partial(_nsa_sample_kernel, n_pages=n_pages, past=past, ts=ts, tp=tp, tn=tn, wb=wb,
                          topk_past=min(SEL_TOPK, npb + 1) - 1),
        grid_spec=gs,
        out_shape=[jax.ShapeDtypeStruct((db, tp, NSA_W), F32), jax.ShapeDtypeStruct((db, wb, LANES), F32),
                   jax.ShapeDtypeStruct((db, wb, LANES), F32)],
        compiler_params=_cparams(("arbitrary",)),
        name="nsa_sample",
    )(page_table_flat, q, qr, gt, kc, vc, ksn, vsn, kwn, vwn, win_k, win_v, oh, gmat, pool_k, pool_v)


def _hgrn_kernel(hm_ref, lb_ref, og_ref, s0_ref, o_ref, sout_ref, st_scr, *, chunk, t_real, levels):
    c = pl.program_id(1)
    nc = pl.num_programs(1)

    @pl.when(c == 0)
    def _():
        for hd in range(HG_HEADS):
            st_scr[hd] = s0_ref[0, hd].T

    row = lax.broadcasted_iota(I32, (chunk, 1), 0)
    valid = (c * chunk + row) < t_real
    row_c = lax.broadcasted_iota(I32, (chunk, chunk), 0)
    col_c = lax.broadcasted_iota(I32, (chunk, chunk), 1)
    outs = []
    for hd in range(HG_HEADS):
        sl = slice(hd * HG_DK, (hd + 1) * HG_DK)
        q = hm_ref[0, :, hd * HG_DK:(hd + 1) * HG_DK]
        z = hm_ref[0, :, HG_W + hd * HG_DK:HG_W + (hd + 1) * HG_DK]
        v = hm_ref[0, :, 2 * HG_W + hd * HG_DV:2 * HG_W + (hd + 1) * HG_DV]
        gate = hm_ref[0, :, 3 * HG_W + hd * HG_DV:3 * HG_W + (hd + 1) * HG_DV]
        lb = lb_ref[:, sl]
        logf = jnp.where(valid, jnp.log(lb + (1.0 - lb) * jax.nn.sigmoid(z)), 0.0)
        k = jnp.where(valid, (1.0 - lb) * jax.nn.sigmoid(-z), 0.0)
        b = logf
        step = 1
        while step < chunk:
            b = b + jnp.where(row >= step, pltpu.roll(b, step, 0), 0.0)
            step *= 2
        st = st_scr[hd]
        o = _nt((q * jnp.exp(b)).astype(BF16), st.astype(BF16))
        if levels:
            a_mat = jnp.zeros((chunk, chunk), F32)
            for h in levels:
                if 2 * h <= 8:
                    b3 = b.reshape(chunk // 8, 8, HG_DK)
                    m = None
                    for blk in range(8 // (2 * h)):
                        mid_row = blk * 2 * h + h - 1
                        cand = jnp.broadcast_to(b3[:, mid_row:mid_row + 1, :], b3.shape).reshape(chunk, HG_DK)
                        m = cand if m is None else jnp.where((row & 7) >= blk * 2 * h, cand, m)
                else:
                    m = jnp.concatenate(
                        [jnp.broadcast_to(b[blk * 2 * h + h - 1:blk * 2 * h + h, :], (2 * h, HG_DK))
                         for blk in range(chunk // (2 * h))], axis=0)
                second = ((row >> _log2(h)) & 1) == 1
                qs = jnp.where(second, q * jnp.exp(jnp.minimum(b - m, 0.0)), 0.0)
                ks = jnp.where(second, 0.0, k * jnp.exp(jnp.minimum(m - b, 0.0)))
                same = (row_c >> _log2(2 * h)) == (col_c >> _log2(2 * h))
                a_mat = jnp.where(same, _nt(qs.astype(BF16), ks.astype(BF16)), a_mat)
            o = o + _mm(a_mat.astype(BF16), v.astype(BF16))
        dblk = min(levels) if levels else chunk
        for d in range(dblk):
            if d == 0:
                o = o + jnp.sum(q * k, axis=1, keepdims=True) * v
                continue
            kd, bd, vd = (pltpu.roll(x, d, 0) for x in (k, b, v))
            p = jnp.where((row & (dblk - 1)) >= d, q * kd * jnp.exp(jnp.minimum(b - bd, 0.0)), 0.0)
            o = o + jnp.sum(p, axis=1, keepdims=True) * vd
        bl = b[chunk - 1:chunk, :]
        st_scr[hd] = jnp.exp(bl) * st + _tn(v.astype(BF16), (k * jnp.exp(bl - b)).astype(BF16))
        outs.append(_rms(o, og_ref[:, sl]) * (gate * jax.nn.sigmoid(gate)))
    o_ref[0] = jnp.concatenate(outs, axis=1)

    @pl.when(c == nc - 1)
    def _():
        for hd in range(HG_HEADS):
            sout_ref[0, hd] = st_scr[hd].T


def _hgrn(hm, lb, og, s0, chunk, t_real):
    b, tpad = hm.shape[:2]
    assert chunk <= 8 or chunk % 16 == 0
    levels = tuple(h for h in (64, 32, 16, 8, 4, 2, 1) if 2 * h <= chunk and chunk >= 16)
    return pl.pallas_call(
        functools.partial(_hgrn_kernel, chunk=chunk, t_real=t_real, levels=levels),
        grid=(b, tpad // chunk),
        in_specs=[pl.BlockSpec((1, chunk, 4 * HG_W), lambda bb, c: (bb, c, 0)),
                  pl.BlockSpec((1, HG_W), lambda bb, c: (0, 0)), pl.BlockSpec((1, HG_W), lambda bb, c: (0, 0)),
                  pl.BlockSpec((1, HG_HEADS, HG_DK, HG_DV), lambda bb, c: (bb, 0, 0, 0))],
        out_specs=[pl.BlockSpec((1, chunk, HG_W), lambda bb, c: (bb, c, 0)),
                   pl.BlockSpec((1, HG_HEADS, HG_DK, HG_DV), lambda bb, c: (bb, 0, 0, 0))],
        out_shape=[jax.ShapeDtypeStruct((b, tpad, HG_W), F32),
                   jax.ShapeDtypeStruct((b, HG_HEADS, HG_DK, HG_DV), F32)],
        scratch_shapes=[pltpu.VMEM((HG_HEADS, HG_DV, HG_DK), F32)],
        compiler_params=_cparams(("parallel", "arbitrary")),
        name="hgrn",
    )(hm, lb, og, s0)


def _mixout_kernel(x_ref, on_ref, oh_ref, gn_ref, woa_ref, wob_ref, gca_ref, wq_ref, x1_ref, q_ref):
    a = _rms(on_ref[...], gn_ref[...]).astype(BF16)
    x1 = x_ref[...] + _mm(a, woa_ref[...]) + _mm(oh_ref[...].astype(BF16), wob_ref[...])
    x1_ref[...] = x1
    hn = _rms(x1, gca_ref[...]).astype(BF16)
    q_ref[...] = (_mm(hn, wq_ref[...]) * (CA_HD ** -0.5)).astype(BF16)


def _mixout(x, o_nsa, o_hg, gn, woa, wob, gca, wq, tm):
    m = x.shape[0]
    row = lambda i: (i, 0)
    const = lambda i: (0, 0)
    return pl.pallas_call(
        _mixout_kernel,
        grid=(m // tm,),
        in_specs=[pl.BlockSpec((tm, D_MODEL), row), pl.BlockSpec((tm, NSA_W), row), pl.BlockSpec((tm, HG_W), row),
                  pl.BlockSpec((1, NSA_W), const), pl.BlockSpec((NSA_W, D_MODEL), const),
                  pl.BlockSpec((HG_W, D_MODEL), const), pl.BlockSpec((1, D_MODEL), const),
                  pl.BlockSpec((D_MODEL, D_MODEL), const)],
        out_specs=[pl.BlockSpec((tm, D_MODEL), row), pl.BlockSpec((tm, D_MODEL), row)],
        out_shape=[jax.ShapeDtypeStruct((m, D_MODEL), F32), jax.ShapeDtypeStruct((m, D_MODEL), BF16)],
        compiler_params=_cparams(("parallel",)),
        name="mixout",
    )(x, o_nsa, o_hg, gn, woa, wob, gca, wq)


def _norm_mm_kernel(x_ref, g_ref, w_ref, o_ref):
    o_ref[...] = _mm(_rms(x_ref[...], g_ref[...]).astype(BF16), w_ref[...])


def _norm_mm(x, g, w, tm):
    m, n = x.shape[0], w.shape[1]
    return pl.pallas_call(
        _norm_mm_kernel,
        grid=(m // tm,),
        in_specs=[pl.BlockSpec((tm, D_MODEL), lambda i: (i, 0)), pl.BlockSpec((1, D_MODEL), lambda i: (0, 0)),
                  pl.BlockSpec((D_MODEL, n), lambda i: (0, 0))],
        out_specs=pl.BlockSpec((tm, n), lambda i: (i, 0)),
        out_shape=jax.ShapeDtypeStruct((m, n), F32),
        compiler_params=_cparams(("parallel",)),
        name="norm_mm",
    )(x, g, w)


def _ca_kernel(q_ref, mk_ref, mv_ref, *rest, ml, tiled):
    o_ref = rest[-1]
    nchunk = CA_HD // LANES

    def head(ref, hh):
        if tiled:
            return jnp.concatenate([ref[0, pl.ds(c * CA_HEADS + hh, ml, stride=nchunk * CA_HEADS), :]
                                    for c in range(nchunk)], axis=1)
        return ref[0, :, hh * CA_HD:(hh + 1) * CA_HD]

    outs = []
    for hh in range(CA_HEADS):
        s = _nt(q_ref[0, :, hh * CA_HD:(hh + 1) * CA_HD], head(mk_ref, hh).astype(BF16))
        e = jnp.exp(s - jnp.max(s, axis=1, keepdims=True))
        p = e / jnp.sum(e, axis=1, keepdims=True)
        outs.append(_mm(p.astype(BF16), head(mv_ref, hh).astype(BF16)))
    o = jnp.concatenate(outs, axis=1).astype(BF16)
    if len(rest) == 3:
        wo_ref, res_ref, _ = rest
        o_ref[0] = res_ref[0] + _mm(o, wo_ref[...])
    else:
        o_ref[0] = o


def _cross_attn(q, mk, mv, tm, ml, tiled, wo=None, res=None):
    b, t = q.shape[:2]
    mem_block = mk.shape[1:]
    rows = lambda bb, i: (bb, i, 0)
    in_specs = [pl.BlockSpec((1, tm, D_MODEL), rows),
                pl.BlockSpec((1,) + mem_block, lambda bb, i: (bb, 0, 0)),
                pl.BlockSpec((1,) + mem_block, lambda bb, i: (bb, 0, 0))]
    args = (q, mk, mv)
    if wo is not None:
        in_specs += [pl.BlockSpec((D_MODEL, D_MODEL), lambda bb, i: (0, 0)), pl.BlockSpec((1, tm, D_MODEL), rows)]
        args += (wo, res)
    return pl.pallas_call(
        functools.partial(_ca_kernel, ml=ml, tiled=tiled),
        grid=(b, t // tm),
        in_specs=in_specs,
        out_specs=pl.BlockSpec((1, tm, D_MODEL), rows),
        out_shape=jax.ShapeDtypeStruct((b, t, D_MODEL), BF16 if wo is None else F32),
        compiler_params=_cparams(("parallel", "parallel")),
        name="cross_attn",
    )(*args)


def _mm_res_kernel(a_ref, w_ref, r_ref, o_ref):
    o_ref[...] = r_ref[...] + _mm(a_ref[...], w_ref[...])


def _mm_res(a, w, res, tm):
    m, k = a.shape
    n = w.shape[1]
    return pl.pallas_call(
        _mm_res_kernel,
        grid=(m // tm,),
        in_specs=[pl.BlockSpec((tm, k), lambda i: (i, 0)), pl.BlockSpec((k, n), lambda i: (0, 0)),
                  pl.BlockSpec((tm, n), lambda i: (i, 0))],
        out_specs=pl.BlockSpec((tm, n), lambda i: (i, 0)),
        out_shape=jax.ShapeDtypeStruct((m, n), F32),
        compiler_params=_cparams(("parallel",)),
        name="mm_res",
    )(a, w, res)


def _ffn_kernel(x_ref, gn_ref, wg_ref, wu_ref, wd_ref, gf_ref, y_ref, h_scr, acc_scr):
    j = pl.program_id(1)

    @pl.when(j == 0)
    def _():
        h_scr[...] = _rms(x_ref[...], gn_ref[...]).astype(BF16)
        acc_scr[...] = jnp.zeros_like(acc_scr)

    h = h_scr[...]
    gate = _mm(h, wg_ref[...])
    act = (gate * jax.nn.sigmoid(gate)) * _mm(h, wu_ref[...])
    acc_scr[...] += _mm(act.astype(BF16), wd_ref[...])

    @pl.when(j == pl.num_programs(1) - 1)
    def _():
        y_ref[...] = _rms(x_ref[...] + acc_scr[...], gf_ref[...])


def _ffn(x, gn, wg, wu, wd, gf, tm, th):
    m = x.shape[0]
    hid = wg.shape[1]
    return pl.pallas_call(
        _ffn_kernel,
        grid=(m // tm, hid // th),
        in_specs=[pl.BlockSpec((tm, D_MODEL), lambda i, j: (i, 0)), pl.BlockSpec((1, D_MODEL), lambda i, j: (0, 0)),
                  pl.BlockSpec((D_MODEL, th), lambda i, j: (0, j)), pl.BlockSpec((D_MODEL, th), lambda i, j: (0, j)),
                  pl.BlockSpec((th, D_MODEL), lambda i, j: (j, 0)), pl.BlockSpec((1, D_MODEL), lambda i, j: (0, 0))],
        out_specs=pl.BlockSpec((tm, D_MODEL), lambda i, j: (i, 0)),
        out_shape=jax.ShapeDtypeStruct((m, D_MODEL), F32),
        scratch_shapes=[pltpu.VMEM((tm, D_MODEL), BF16), pltpu.VMEM((tm, D_MODEL), F32)],
        compiler_params=_cparams(("parallel", "arbitrary")),
        name="ffn",
    )(x, gn, wg, wu, wd, gf)


def _pair(a, axis):
    shp = a.shape
    a = a.reshape(shp[:axis] + (NSA_KV, NSA_GROUP, HEAD_DIM) + shp[axis + 1:])
    a = jnp.swapaxes(a, axis, axis + 1)
    return a.reshape(shp)


def _rope_tables(pos):
    half = ROPE_DIM // 2
    inv = ROPE_THETA ** (-jnp.arange(half, dtype=F32) / half)
    ang = pos.astype(F32)[:, None] * inv[None, :]
    cos, sin = jnp.cos(ang), jnp.sin(ang)
    l = np.arange(LANES) % HEAD_DIM
    idx = l % half
    c = jnp.where(l < ROPE_DIM, cos[:, idx], 1.0)
    sa = jnp.where(l < half, -sin[:, idx], 0.0)
    sb = jnp.where((l >= half) & (l < ROPE_DIM), sin[:, idx], 0.0)
    return c, sa, sb


def _prep_w_in(w):
    offs = np.cumsum([0, NSA_W, 3 * NSA_HEADS] + [NSA_KV * HEAD_DIM] * 6 + [HG_W] * 4)
    seg = lambda i: w[:, offs[i]:offs[i + 1]]
    wq = _pair(seg(0), 1)
    wks = seg(4)
    z = jnp.zeros((D_MODEL, HEAD_DIM), w.dtype)
    aug = jnp.concatenate([wks[:, :HEAD_DIM], z, z, wks[:, HEAD_DIM:]], axis=1)
    gates = jnp.pad(seg(1), ((0, 0), (0, LANES - 3 * NSA_HEADS)))
    out = jnp.concatenate([wq] + [seg(i) for i in range(2, 8)] + [aug] + [seg(i) for i in range(8, 12)] + [gates],
                          axis=1)
    assert out.shape[1] == N_PROJ
    return out.astype(BF16)


def _prep_compress(pe, w1, w2):
    npair = CMP_STRIDE // 2
    pe_big = jnp.broadcast_to(pe.reshape(2, CMP_STRIDE, 1, HEAD_DIM), (2, CMP_STRIDE, NSA_KV, HEAD_DIM))
    pe_big = jnp.pad(pe_big.reshape(2, CMP_STRIDE * LANES), ((0, 6), (0, 0)))
    w1r = w1.reshape(2, npair, 2, HEAD_DIM, CMP_HID)
    eye = jnp.eye(NSA_KV, dtype=w1.dtype)
    w1_big = jnp.einsum('cjldh,ge->jlgdceh', w1r, eye).reshape(npair, 2 * LANES, 4 * CMP_HID)
    w2_big = jnp.einsum('hd,ge->ghed', w2, eye).reshape(NSA_KV * CMP_HID, LANES)
    return pe_big, w1_big.astype(BF16), w2_big.astype(BF16)


def _pad_rows(a, n):
    return jnp.pad(a, ((0, 0), (0, n - a.shape[1]), (0, 0)))


def kernel(x_prompt, x_sample, cache_cmp_k, cache_cmp_v, cache_sel_k, cache_sel_v, cache_win_k, cache_win_v,
           state_hgrn, cache_mem_k, cache_mem_v, page_table, mem_prompt, norm_mix, w_in, cmp_pe_k, cmp_w1_k,
           cmp_w2_k, cmp_pe_v, cmp_w1_v, cmp_w2_v, nsa_out_norm, hg_lb_logits, hg_out_norm, w_out, norm_ca,
           norm_mem, ca_wq, ca_wk, ca_wv, ca_wo, norm_ffn, ffn_w_gate, ffn_w_up, ffn_w_down, final_norm):
    B, T = x_prompt.shape[:2]
    DB, TS = x_sample.shape[:2]
    n_pages = page_table.shape[1]
    past = n_pages * PAGE_SIZE
    n_pool = cache_cmp_k.shape[1]
    wb = cache_win_k.shape[2]
    ml = mem_prompt.shape[1]
    assert w_in.shape[0] == 1, "single layer"
    row2 = lambda a: a.reshape(1, -1)

    w_in_p = _prep_w_in(w_in[0])
    pek, w1k, w2k = _prep_compress(cmp_pe_k[0], cmp_w1_k[0], cmp_w2_k[0])
    pev, w1v, w2v = _prep_compress(cmp_pe_v[0], cmp_w1_v[0], cmp_w2_v[0])
    lb = jnp.cumsum(jax.nn.softmax(hg_lb_logits.astype(F32), axis=0), axis=0)[0].reshape(1, HG_W)
    gn_nsa = row2(_pair(nsa_out_norm[0], 0))
    wo_a = _pair(w_out[0][:NSA_W], 0).astype(BF16)
    wo_b = w_out[0][NSA_W:].astype(BF16)
    wq_ca = ca_wq[0].astype(BF16)
    wo_ca = ca_wo[0].astype(BF16)
    w_mem = jnp.concatenate([ca_wk[0], ca_wv[0]], axis=1).astype(BF16)
    wg, wu, wd = ffn_w_gate[0].astype(BF16), ffn_w_up[0].astype(BF16), ffn_w_down[0].astype(BF16)
    g_mix, g_ca, g_ffn, g_fin = row2(norm_mix[0]), row2(norm_ca[0]), row2(norm_ffn[0]), row2(final_norm)
    g_hg = row2(hg_out_norm[0])

    tabs_p = _rope_tables(jnp.arange(T))
    tabs_s = _rope_tables(past + (jnp.arange(DB * TS) % TS))
    blk = np.arange(T) // SEL_BLK
    lanes2 = np.arange(2 * LANES)
    oh_np = ((lanes2[None, :] >= HEAD_DIM) & (lanes2[None, :] < 3 * HEAD_DIM)
             & ((lanes2[None, :] - HEAD_DIM) % HEAD_DIM == blk[:, None]))
    oh_p = jnp.asarray(oh_np.astype(np.float32))
    oh_s = jnp.zeros((DB * TS, 2 * LANES), F32)

    def tail(x, o_nsa, o_hg, mk, mv, nb, tm, tiled):
        rows = x.shape[0]
        x1, qca = _mixout(x, o_nsa, o_hg, gn_nsa, wo_a, wo_b, g_ca, wq_ca, tm)
        per = rows // nb
        tq = min(per, ROW_TILE)
        if per % 16:
            tq = -(-per // 16) * 16
            qca3 = _pad_rows(qca.reshape(nb, per, D_MODEL), tq)
            oca = _cross_attn(qca3, mk, mv, tq, ml, tiled)[:, :per].reshape(rows, D_MODEL)
            x2 = _mm_res(oca, wo_ca, x1, tm)
        else:
            x2 = _cross_attn(qca.reshape(nb, per, D_MODEL), mk, mv, tq, ml, tiled, wo_ca,
                             x1.reshape(nb, per, D_MODEL)).reshape(rows, D_MODEL)
        return _ffn(x2, g_ffn, wg, wu, wd, g_fin, min(rows, FFN_ROW_TILE), wg.shape[1] // 2)

    M = B * T
    (q, qr, kcp, vcp, _, _, _, _, ksa, kwb, vwb, hm, gt, kct, vct, kst, vst, kwt, vwt) = _proj_in(
        x_prompt.reshape(M, D_MODEL), g_mix, w_in_p, tabs_p, oh_p, ROW_TILE, B)
    r3 = lambda a: a.reshape(B, T, -1)
    cmp_w = (pek, pev, w1k, w1v, w2k, w2v)
    kc_p, vct_p = _compress_rows(r3(kcp), r3(vcp), cmp_w)
    o_nsa = _nsa_prompt(r3(q), r3(qr), r3(gt), kc_p, vct_p, r3(ksa), vst, r3(kwb), vwt, NSA_Q_TILE, NSA_KEY_TILE)
    o_hg, s_p = _hgrn(r3(hm), lb, g_hg, jnp.zeros((B, HG_HEADS, HG_DK, HG_DV), F32), HG_CHUNK, T)
    mkv = _norm_mm(mem_prompt.reshape(B * ml, D_MODEL), row2(norm_mem[0]), w_mem, ROW_TILE)
    mk_p = mkv[:, :D_MODEL].reshape(B, ml, D_MODEL)
    mv_p = mkv[:, D_MODEL:].reshape(B, ml, D_MODEL)
    y_p = tail(x_prompt.reshape(M, D_MODEL), o_nsa.reshape(M, NSA_W), o_hg.reshape(M, HG_W), mk_p, mv_p, B, ROW_TILE, False)

    kv5 = lambda a, n: a.reshape(1, -1, n, NSA_KV, HEAD_DIM)
    wbp = min(WINDOW, T)
    nat = lambda a: jnp.transpose(a.reshape(B, NSA_KV, HEAD_DIM, -1), (0, 3, 1, 2))[None]
    outs_p = (nat(kct), nat(vct), nat(kst), nat(vst),
              nat(kwt[:, :, T - wbp:]), nat(vwt[:, :, T - wbp:]), s_p[None],
              mk_p.reshape(1, B, ml, CA_HEADS, CA_HD), mv_p.reshape(1, B, ml, CA_HEADS, CA_HD))

    MS = DB * TS
    (q, qr, kcs, vcs, kss, vss, kws, vws, _, _, _, hm, gt) = _proj_in(
        x_sample.reshape(MS, D_MODEL), g_mix, w_in_p, tabs_s, oh_s, MS, 1)[:13]
    pt_flat = page_table.reshape(-1).astype(I32)
    pool_t = lambda a: jnp.transpose(a[0], (0, 2, 3, 1)).reshape(n_pool, LANES, PAGE_SIZE)
    kc_s, vc_s = _compress_paged(pt_flat, pool_t(cache_cmp_k), pool_t(cache_cmp_v), cmp_w, DB, n_pages)
    tp, tn = SAMPLE_Q_PAD, SAMPLE_KEY_PAD
    s3 = lambda a, n: _pad_rows(a.reshape(DB, TS, -1), n)
    key_blk = np.arange(past) // SEL_BLK
    oh_keys = jnp.asarray((np.arange(LANES)[:, None] == key_blk[None, :]).astype(np.float32)).astype(BF16)
    nseg = past // CMP_STRIDE
    gmat = jnp.asarray((np.arange(nseg)[:, None] // (SEL_BLK // CMP_STRIDE)
                        == np.arange(LANES)[None, :]).astype(np.float32))
    o_nsa_s, nwk, nwv = _nsa_sample(
        pt_flat, s3(q, tp), s3(qr, tp), s3(gt, tp), kc_s, vc_s, s3(kss, tn), s3(vss, tn), s3(kws, tn), s3(vws, tn),
        cache_win_k[0].reshape(DB, wb, LANES), cache_win_v[0].reshape(DB, wb, LANES), oh_keys, gmat,
        pool_t(cache_sel_k), pool_t(cache_sel_v), n_pages, past, TS)
    o_hg_s, s_s = _hgrn(s3(hm, SAMPLE_Q_PAD), lb, g_hg, state_hgrn[0].astype(F32), SAMPLE_Q_PAD, TS)
    nchunk = CA_HD // LANES
    mem_t = lambda a: jnp.transpose(a[0].reshape(DB, ml, CA_HEADS, nchunk, LANES), (0, 1, 3, 2, 4)).reshape(
        DB, ml * nchunk * CA_HEADS, LANES)
    y_s = tail(x_sample.reshape(MS, D_MODEL), o_nsa_s[:, :TS].reshape(MS, NSA_W), o_hg_s[:, :TS].reshape(MS, HG_W),
               mem_t(cache_mem_k), mem_t(cache_mem_v), DB, MS, True)

    outs_s = (kv5(kcs, TS), kv5(vcs, TS), kv5(kss, TS), kv5(vss, TS),
              nwk.reshape(1, DB, wb, NSA_KV, HEAD_DIM), nwv.reshape(1, DB, wb, NSA_KV, HEAD_DIM), s_s[None])
    return (y_p.reshape(B, T, D_MODEL), y_s.reshape(DB, TS, D_MODEL)) + outs_p + outs_s
```
